```python
import math
import jax, jax.numpy as jnp
from jax import lax
import numpy as np

D_MODEL = 1024
BATCH = 8
SEQ = 2048
DEPTH = 2
DEC_BATCH = 128
DEC_SEQ = 1
PAST_LEN = 16384
PAGE_SIZE = 128

N_MIXERS = 2
N_POOL_LAYERS = (DEPTH + 1) // 2
N_DN_LAYERS = DEPTH // 2
POOL_WINDOWS = (2, 4, 8, 16)
N_POOL_GROUPS = len(POOL_WINDOWS)
POOL_GD = D_MODEL // N_POOL_GROUPS
POOL_BUF = max(POOL_WINDOWS) - 1
DN_HEAD_DIM = 128
DN_HEADS = D_MODEL // DN_HEAD_DIM
DN_DIM = DN_HEADS * DN_HEAD_DIM
DN_CONV = 4
DN_CHUNK = 64
DT_MIN = 0.001
DT_MAX = 0.1
D_FF = ((8 * D_MODEL // 3 + 127) // 128) * 128
FFN_CONV = 3
NORM_EPS = 1e-6

kernel_name = "hybrid_pool_gdn_convffn_step"


def rms_norm(x, w):
    xf = x.astype(jnp.float32)
    y = xf * lax.rsqrt(jnp.mean(xf * xf, axis=-1, keepdims=True) + NORM_EPS)
    return (y * w.astype(jnp.float32)).astype(x.dtype)


def l2_norm(x):
    xf = x.astype(jnp.float32)
    return xf * lax.rsqrt(jnp.sum(xf * xf, axis=-1, keepdims=True) + NORM_EPS)


def causal_depthwise_conv(xx, w, t):
    out = xx[:, 0:t] * w[0]
    for j in range(1, w.shape[0]):
        out = out + xx[:, j:j + t] * w[j]
    return out


def pool_mixer(xn, buf, pos0, w_grp, scale):
    b, t, d = xn.shape
    xx = jnp.concatenate([buf.astype(xn.dtype), xn], axis=1)
    cs = jnp.cumsum(xx.astype(jnp.float32), axis=1)
    cs = jnp.pad(cs, ((0, 0), (1, 0), (0, 0)))
    csg = cs.reshape(b, POOL_BUF + 1 + t, N_POOL_GROUPS, POOL_GD)
    xg = xn.astype(jnp.float32).reshape(b, t, N_POOL_GROUPS, POOL_GD)
    pos = pos0 + jnp.arange(t)
    diffs = []
    for g, w in enumerate(POOL_WINDOWS):
        s = csg[:, POOL_BUF + 1:POOL_BUF + 1 + t, g] - csg[:, POOL_BUF + 1 - w:POOL_BUF + 1 - w + t, g]
        cnt = jnp.minimum(w, pos + 1).astype(jnp.float32)[None, :, None]
        diffs.append(s / cnt - xg[:, :, g])
    dg = jnp.stack(diffs, axis=2)
    y = jnp.einsum('btgc,gcd->btgd', dg, w_grp.astype(jnp.float32)).reshape(b, t, d)
    out = (y * scale.astype(jnp.float32)).astype(xn.dtype)
    return out, xx[:, -POOL_BUF:]


def gated_delta_rule(q, k, v, g, beta, s0):
    b, t, h, _ = q.shape
    dv = v.shape[-1]
    c = min(DN_CHUNK, t)
    pad = (-t) % c
    nc = (t + pad) // c

    def chunks(a):
        a = jnp.pad(a.astype(jnp.float32), [(0, 0), (0, pad)] + [(0, 0)] * (a.ndim - 2))
        a = a.reshape((b, nc, c) + a.shape[2:])
        return jnp.moveaxis(a, 3, 2)

    q, k, v, g, beta = chunks(q), chunks(k), chunks(v), chunks(g), chunks(beta)
    gc = jnp.cumsum(g, axis=-1)
    incl = jnp.tril(jnp.ones((c, c), dtype=bool))
    strict = jnp.tril(jnp.ones((c, c), dtype=bool), k=-1)
    decay = jnp.exp(jnp.where(incl, gc[..., :, None] - gc[..., None, :], -jnp.inf))
    kb = k * beta[..., None]
    vb = v * beta[..., None]
    a_mat = jnp.where(strict, jnp.einsum('bnhid,bnhjd->bnhij', kb, k) * decay, 0.0) + jnp.eye(c, dtype=jnp.float32)
    rhs = jnp.concatenate([vb, kb * jnp.exp(gc)[..., None]], axis=-1)
    sol = lax.linalg.triangular_solve(a_mat, rhs, left_side=True, lower=True, unit_diagonal=True)
    u_in, w_in = sol[..., :dv], sol[..., dv:]
    qk = jnp.einsum('bnhid,bnhjd->bnhij', q, k) * decay
    q_dec = q * jnp.exp(gc)[..., None]
    g_last = gc[..., -1]
    k_dec = k * jnp.exp(g_last[..., None] - gc)[..., None]

    def step(s, xs):
        u_c, w_c, qk_c, qd_c, kd_c, gl_c = xs
        u = u_c - jnp.einsum('bhck,bhkv->bhcv', w_c, s)
        o = jnp.einsum('bhck,bhkv->bhcv', qd_c, s) + jnp.einsum('bhij,bhjv->bhiv', qk_c, u)
        s = s * jnp.exp(gl_c)[..., None, None] + jnp.einsum('bhck,bhcv->bhkv', kd_c, u)
        return s, o

    xs = tuple(jnp.moveaxis(a, 1, 0) for a in (u_in, w_in, qk, q_dec, k_dec, g_last))
    s_final, o = lax.scan(step, s0.astype(jnp.float32), xs)
    o = jnp.transpose(o, (1, 0, 3, 2, 4)).reshape(b, nc * c, h, dv)[:, :t]
    return o, s_final


def deltanet_mixer(xn, conv_buf, s0, w_in, conv_w, a_log, dt_bias, o_norm_w, w_out):
    b, t, _ = xn.shape
    proj = xn @ w_in
    qkv, z, a, bb = jnp.split(proj, [3 * DN_DIM, 4 * DN_DIM, 4 * DN_DIM + DN_HEADS], axis=-1)
    qkv_all = jnp.concatenate([conv_buf.astype(qkv.dtype), qkv], axis=1)
    qkv_c = jax.nn.silu(causal_depthwise_conv(qkv_all, conv_w, t))
    q, k, v = jnp.split(qkv_c, 3, axis=-1)
    q = l2_norm(q.reshape(b, t, DN_HEADS, DN_HEAD_DIM)) * (DN_HEAD_DIM ** -0.5)
    k = l2_norm(k.reshape(b, t, DN_HEADS, DN_HEAD_DIM))
    v = v.reshape(b, t, DN_HEADS, DN_HEAD_DIM)
    beta = jax.nn.sigmoid(bb.astype(jnp.float32))
    g = -jnp.exp(a_log.astype(jnp.float32)) * jax.nn.softplus(a.astype(jnp.float32) + dt_bias.astype(jnp.float32))
    o, s_new = gated_delta_rule(q, k, v, g, beta, s0)
    o = rms_norm(o, o_norm_w) * jax.nn.silu(z.astype(jnp.float32).reshape(b, t, DN_HEADS, DN_HEAD_DIM))
    out = o.reshape(b, t, DN_DIM).astype(xn.dtype) @ w_out
    return out, qkv_all[:, -(DN_CONV - 1):], s_new.astype(s0.dtype)


def conv_ffn(xn, buf, w_up, conv_w, conv_b, w_down):
    t = xn.shape[1]
    hh = jnp.concatenate([buf.astype(xn.dtype), xn @ w_up], axis=1)
    c = causal_depthwise_conv(hh, conv_w, t) + conv_b
    gate, val = jnp.split(c, 2, axis=-1)
    return (jax.nn.silu(gate) * val) @ w_down, hh[:, -(FFN_CONV - 1):]


def trunk(x, pool_buf, dn_conv, dn_ssm, ffn_conv, pos0, params):
    (norm1_w, norm2_w, final_norm_w, pool_w, pool_scale, dn_w_in, dn_conv_w, dn_a_log,
     dn_dt_bias, dn_o_norm_w, dn_w_out, ffn_w_up, ffn_conv_w, ffn_conv_b, ffn_w_down) = params
    new_pool, new_dnc, new_dns, new_ffn = [], [], [], []
    for i in range(DEPTH):
        h = rms_norm(x, norm1_w[i])
        j = i // N_MIXERS
        if i % N_MIXERS == 0:
            out, nb = pool_mixer(h, pool_buf[j], pos0, pool_w[j], pool_scale[j])
            new_pool.append(nb)
        else:
            out, ncb, ns = deltanet_mixer(h, dn_conv[j], dn_ssm[j], dn_w_in[j], dn_conv_w[j], dn_a_log[j],
                                          dn_dt_bias[j], dn_o_norm_w[j], dn_w_out[j])
            new_dnc.append(ncb)
            new_dns.append(ns)
        x = x + out
        out, nf = conv_ffn(rms_norm(x, norm2_w[i]), ffn_conv[i], ffn_w_up[i], ffn_conv_w[i], ffn_conv_b[i], ffn_w_down[i])
        x = x + out
        new_ffn.append(nf)
    y = rms_norm(x, final_norm_w)
    return y, jnp.stack(new_pool), jnp.stack(new_dnc), jnp.stack(new_dns), jnp.stack(new_ffn)


def setup_inputs(seed: int = 0) -> dict:
    key = jax.random.key(seed)
    ks = jax.random.split(key, 22)
    f32 = jnp.float32

    def nrm(k, shape, s):
        return jax.random.normal(k, shape, f32) * s

    dn_in_cols = 4 * DN_DIM + 2 * DN_HEADS
    dt = jnp.exp(jax.random.uniform(ks[10], (N_DN_LAYERS, DN_HEADS), f32)
                 * (math.log(DT_MAX) - math.log(DT_MIN)) + math.log(DT_MIN))
    return {
        'x_prompt': nrm(ks[0], (BATCH, SEQ, D_MODEL), 1.0),
        'x_sample': nrm(ks[1], (DEC_BATCH, DEC_SEQ, D_MODEL), 1.0),
        'state_pool_buf': nrm(ks[2], (N_POOL_LAYERS, DEC_BATCH, POOL_BUF, D_MODEL), 1.0),
        'state_dn_conv': nrm(ks[3], (N_DN_LAYERS, DEC_BATCH, DN_CONV - 1, 3 * DN_DIM), 1.0),
        'state_dn_ssm': nrm(ks[4], (N_DN_LAYERS, DEC_BATCH, DN_HEADS, DN_HEAD_DIM, DN_HEAD_DIM), DN_HEAD_DIM ** -0.5),
        'state_ffn_conv': nrm(ks[5], (DEPTH, DEC_BATCH, FFN_CONV - 1, 2 * D_FF), 1.0),
        'norm1_w': 1.0 + nrm(ks[6], (DEPTH, D_MODEL), 0.02),
        'norm2_w': 1.0 + nrm(ks[7], (DEPTH, D_MODEL), 0.02),
        'final_norm_w': 1.0 + nrm(ks[8], (D_MODEL,), 0.02),
        'pool_w': nrm(ks[9], (N_POOL_LAYERS, N_POOL_GROUPS, POOL_GD, POOL_GD), POOL_GD ** -0.5),
        'pool_scale': 1.0 + nrm(ks[11], (N_POOL_LAYERS, D_MODEL), 0.1),
        'dn_w_in': nrm(ks[12], (N_DN_LAYERS, D_MODEL, dn_in_cols), D_MODEL ** -0.5),
        'dn_conv_w': nrm(ks[13], (N_DN_LAYERS, DN_CONV, 3 * DN_DIM), DN_CONV ** -0.5),
        'dn_a_log': jnp.log(jax.random.uniform(ks[14], (N_DN_LAYERS, DN_HEADS), f32, 1.0, 16.0)),
        'dn_dt_bias': dt + jnp.log(-jnp.expm1(-dt)),
        'dn_o_norm_w': 1.0 + nrm(ks[15], (N_DN_LAYERS, DN_HEAD_DIM), 0.02),
        'dn_w_out': nrm(ks[16], (N_DN_LAYERS, DN_DIM, D_MODEL), DN_DIM ** -0.5),
        'ffn_w_up': nrm(ks[17], (DEPTH, D_MODEL, 2 * D_FF), D_MODEL ** -0.5),
        'ffn_conv_w': nrm(ks[18], (DEPTH, FFN_CONV, 2 * D_FF), FFN_CONV ** -0.5),
        'ffn_conv_b': nrm(ks[19], (DEPTH, 2 * D_FF), 0.01),
        'ffn_w_down': nrm(ks[20], (DEPTH, D_FF, D_MODEL), D_FF ** -0.5),
    }


def reference(x_prompt, x_sample, state_pool_buf, state_dn_conv, state_dn_ssm, state_ffn_conv,
              norm1_w, norm2_w, final_norm_w, pool_w, pool_scale, dn_w_in, dn_conv_w, dn_a_log,
              dn_dt_bias, dn_o_norm_w, dn_w_out, ffn_w_up, ffn_conv_w, ffn_conv_b, ffn_w_down):
    params = (norm1_w, norm2_w, final_norm_w, pool_w, pool_scale, dn_w_in, dn_conv_w, dn_a_log,
              dn_dt_bias, dn_o_norm_w, dn_w_out, ffn_w_up, ffn_conv_w, ffn_conv_b, ffn_w_down)
    bp, dt = x_prompt.shape[0], x_prompt.dtype
    zero_pool = jnp.zeros((N_POOL_LAYERS, bp, POOL_BUF, D_MODEL), dt)
    zero_dnc = jnp.zeros((N_DN_LAYERS, bp, DN_CONV - 1, 3 * DN_DIM), dt)
    zero_dns = jnp.zeros((N_DN_LAYERS, bp, DN_HEADS, DN_HEAD_DIM, DN_HEAD_DIM), jnp.float32)
    zero_ffn = jnp.zeros((DEPTH, bp, FFN_CONV - 1, 2 * D_FF), dt)
    y_prompt, pool_p, dnc_p, dns_p, ffn_p = trunk(x_prompt, zero_pool, zero_dnc, zero_dns, zero_ffn, 0, params)
    y_sample, pool_s, dnc_s, dns_s, ffn_s = trunk(x_sample, state_pool_buf, state_dn_conv, state_dn_ssm,
                                                  state_ffn_conv, PAST_LEN, params)
    return (y_prompt, y_sample, pool_p, pool_s, dnc_p, dnc_s, dns_p, dns_s, ffn_p, ffn_s)
```

```python
import functools

import jax
import jax.numpy as jnp
from jax import lax
from jax.experimental import pallas as pl
from jax.experimental.pallas import tpu as pltpu

F32 = jnp.float32
BF16 = jnp.bfloat16

NORM_EPS = 1e-6
POOL_WINDOWS = (2, 4, 8, 16)
POOL_HALO = 16
CONV_HALO = 8
DN_HEAD_DIM = 128
DN_CHUNK = 64
FFN_CHUNK = 256
QKV_CHUNK = 512
PROMPT_TILE = 256
SAMPLE_BLOCK = 8
VMEM_LIMIT = 56 * 1024 * 1024


def _dot(a, b):
    return jnp.dot(a.astype(BF16), b.astype(BF16), preferred_element_type=F32)


def _dot_nt(a, b):
    return lax.dot_general(a.astype(BF16), b.astype(BF16), (((1,), (1,)), ((), ())),
                           preferred_element_type=F32)


def _split3(x):
    hi = x.astype(BF16)
    r = x - hi.astype(F32)
    mid = r.astype(BF16)
    lo = (r - mid.astype(F32)).astype(BF16)
    return hi, mid, lo


def _dot_exact_rhs(x, e):
    hi, mid, lo = _split3(x)
    return (jnp.dot(lo, e, preferred_element_type=F32) + jnp.dot(mid, e, preferred_element_type=F32)
            + jnp.dot(hi, e, preferred_element_type=F32))


def _dot_exact_lhs(e, x):
    hi, mid, lo = _split3(x)
    return (jnp.dot(e, lo, preferred_element_type=F32) + jnp.dot(e, mid, preferred_element_type=F32)
            + jnp.dot(e, hi, preferred_element_type=F32))


def _dot3(a, b):
    a_hi = a.astype(BF16)
    a_lo = (a - a_hi.astype(F32)).astype(BF16)
    b_hi = b.astype(BF16)
    b_lo = (b - b_hi.astype(F32)).astype(BF16)
    return (jnp.dot(a_lo, b_hi, preferred_element_type=F32) + jnp.dot(a_hi, b_lo, preferred_element_type=F32)
            + jnp.dot(a_hi, b_hi, preferred_element_type=F32))


def _rms(x, w):
    return x * lax.rsqrt(jnp.mean(x * x, axis=-1, keepdims=True) + NORM_EPS) * w


def _silu(x):
    return x / (1.0 + jnp.exp(-x))


def _softplus(x):
    return jnp.maximum(x, 0.0) + jnp.log1p(jnp.exp(-jnp.abs(x)))


def _full(shape):
    return pl.BlockSpec(shape, lambda *_: (0,) * len(shape))


def _resident(shape):
    return pl.BlockSpec(shape, lambda *_: (0,) * len(shape), pipeline_mode=pl.Buffered(1))


def _ffn_tile(xn_b, wup_ref, cw_ref, cb_ref, wdn_ref, carry_ref):
    tm = xn_b.shape[0]
    dff = wdn_ref.shape[0]
    acc = None
    for c in range(dff // FFN_CHUNK):
        conv = []
        for half in range(2):
            lo = half * dff + c * FFN_CHUNK
            cols = slice(lo, lo + FFN_CHUNK)
            h = jnp.dot(xn_b, wup_ref[:, cols], preferred_element_type=F32)
            ext = jnp.concatenate([carry_ref[:, cols], h], axis=0)
            y = (ext * cw_ref[2:3, cols] + pltpu.roll(ext, 1, 0) * cw_ref[1:2, cols]
                 + pltpu.roll(ext, 2, 0) * cw_ref[0:1, cols])
            conv.append(y[CONV_HALO:] + cb_ref[:, cols])
            carry_ref[:, cols] = h[tm - CONV_HALO:]
        a = (_silu(conv[0]) * conv[1]).astype(BF16)
        d = jnp.dot(a, wdn_ref[c * FFN_CHUNK:(c + 1) * FFN_CHUNK, :], preferred_element_type=F32)
        acc = d if acc is None else acc + d
    return acc


def _prompt_layer0_kernel(x_ref, n1_ref, pw_ref, ps_ref, n2_ref, wup_ref, cw_ref, cb_ref, wdn_ref,
                          xo_ref, pool_ref, fst_ref, hbuf, carry):
    t = pl.program_id(1)
    tm = x_ref.shape[1]
    gd = pw_ref.shape[1]

    @pl.when(t == 0)
    def _():
        hbuf[0:POOL_HALO, :] = jnp.zeros((POOL_HALO, hbuf.shape[1]), F32)
        carry[...] = jnp.zeros(carry.shape, F32)

    x = x_ref[0]
    h = _rms(x, n1_ref[...])
    hbuf[POOL_HALO:POOL_HALO + tm, :] = h
    pos1 = t * tm + lax.broadcasted_iota(jnp.int32, (tm, 1), 0) + 1
    parts = []
    for g, w in enumerate(POOL_WINDOWS):
        cols = slice(g * gd, (g + 1) * gd)
        s = hbuf[:, cols]
        sh = 1
        while sh < w:
            s = s + pltpu.roll(s, sh, 0)
            sh *= 2
        cnt = jnp.minimum(w, pos1).astype(F32)
        dg = s[POOL_HALO:] / cnt - h[:, cols]
        parts.append(x[:, cols] + _dot(dg, pw_ref[g]) * ps_ref[:, cols])
    x1 = jnp.concatenate(parts, axis=1)
    tail = hbuf[tm:tm + POOL_HALO, :]
    pool_ref[0] = tail
    hbuf[0:POOL_HALO, :] = tail

    xn = _rms(x1, n2_ref[...]).astype(BF16)
    xo_ref[0] = x1 + _ffn_tile(xn, wup_ref, cw_ref, cb_ref, wdn_ref, carry)
    fst_ref[0] = carry[...]


def _prompt_ffn_final_kernel(x_ref, n2_ref, wup_ref, cw_ref, cb_ref, wdn_ref, fn_ref,
                             yo_ref, fst_ref, carry):
    @pl.when(pl.program_id(1) == 0)
    def _():
        carry[...] = jnp.zeros(carry.shape, F32)

    x = x_ref[0]
    xn = _rms(x, n2_ref[...]).astype(BF16)
    x = x + _ffn_tile(xn, wup_ref, cw_ref, cb_ref, wdn_ref, carry)
    yo_ref[0] = _rms(x, fn_ref[...])
    fst_ref[0] = carry[...]


def _prompt_deltanet_kernel(x_ref, n1_ref, wqkv_ref, wz_ref, wab_ref, cw_ref, alog_ref, dtb_ref, onw_ref,
                            wout_ref, ltri_ref, selg_ref, selb_ref,
                            xo_ref, cst_ref, ssm_ref,
                            carry, s_ref, qkv_s, u_s, o_s):
    t = pl.program_id(1)
    tm = x_ref.shape[1]
    dk = DN_HEAD_DIM
    nh = s_ref.shape[0]
    dn = nh * dk
    ch = DN_CHUNK
    nch = tm // ch

    @pl.when(t == 0)
    def _():
        carry[...] = jnp.zeros(carry.shape, F32)
        s_ref[...] = jnp.zeros(s_ref.shape, F32)
        u_s[...] = jnp.zeros(u_s.shape, F32)

    x = x_ref[0]
    xb = _rms(x, n1_ref[...]).astype(BF16)

    for c in range(3 * dn // QKV_CHUNK):
        cols = slice(c * QKV_CHUNK, (c + 1) * QKV_CHUNK)
        pre = jnp.dot(xb, wqkv_ref[:, cols], preferred_element_type=F32)
        ext = jnp.concatenate([carry[:, cols], pre], axis=0)
        y = (ext * cw_ref[3:4, cols] + pltpu.roll(ext, 1, 0) * cw_ref[2:3, cols]
             + pltpu.roll(ext, 2, 0) * cw_ref[1:2, cols] + pltpu.roll(ext, 3, 0) * cw_ref[0:1, cols])
        qkv_s[:, cols] = _silu(y[CONV_HALO:])
        carry[:, cols] = pre[tm - CONV_HALO:]
    cst_ref[0] = carry[...]

    ab = jnp.dot(xb, wab_ref[...], preferred_element_type=F32)
    g = -jnp.exp(alog_ref[...]) * _softplus(ab + dtb_ref[...])
    gc = _dot_exact_lhs(ltri_ref[...], g)
    gc_t = gc.T
    gcb = _dot_exact_rhs(gc, selg_ref[...])
    betab = _dot_exact_rhs(1.0 / (1.0 + jnp.exp(-ab)), selb_ref[...])
    glast = jnp.concatenate(
        [jnp.broadcast_to(gcb[(c + 1) * ch - 1:(c + 1) * ch, :], (ch, dn)) for c in range(nch)], axis=0)
    egc = jnp.exp(gcb)
    kdec = jnp.exp(glast - gcb)
    elast = jnp.exp(glast)

    row = lax.broadcasted_iota(jnp.int32, (tm, tm), 0)
    col = lax.broadcasted_iota(jnp.int32, (tm, tm), 1)
    same = (row // ch) == (col // ch)
    incl = same & (row >= col)
    strict = same & (row > col)
    crow = lax.broadcasted_iota(jnp.int32, (ch, tm), 0)
    ccol = lax.broadcasted_iota(jnp.int32, (ch, tm), 1)
    eye_c = (crow == (ccol % ch)).astype(F32)

    def compress(m):
        out = m[0:ch]
        for c in range(1, nch):
            out = out + m[c * ch:(c + 1) * ch]
        return out

    def expand(mc):
        return jnp.where(same, jnp.concatenate([mc] * nch, axis=0), 0.0)

    per_head = []
    for h in range(nh):
        hs = slice(h * dk, (h + 1) * dk)
        qh = qkv_s[:, hs]
        kh = qkv_s[:, dn + h * dk:dn + (h + 1) * dk]
        vh = qkv_s[:, 2 * dn + h * dk:2 * dn + (h + 1) * dk]
        qh = qh * lax.rsqrt(jnp.sum(qh * qh, axis=-1, keepdims=True) + NORM_EPS) * (dk ** -0.5)
        kh = kh * lax.rsqrt(jnp.sum(kh * kh, axis=-1, keepdims=True) + NORM_EPS)
        beta = betab[:, hs]
        kb = kh * beta
        vb = vh * beta
        sc = _dot_nt(jnp.concatenate([kb, qh], axis=0), kh)
        diff = jnp.concatenate([gcb[:, hs]] * (tm // dk), axis=1) - jnp.broadcast_to(gc_t[h:h + 1, :], (tm, tm))
        decay = jnp.exp(jnp.where(incl, diff, -jnp.inf))
        n_bd = jnp.where(strict, sc[0:tm] * decay, 0.0)
        qk_bd = sc[tm:2 * tm] * decay

        n_c = compress(n_bd)
        m_c = _dot(n_c, n_bd)
        p_c = eye_c - n_c
        sq = 2
        while sq < ch:
            m_bd = expand(m_c)
            if 2 * sq < ch:
                both = _dot(jnp.concatenate([m_c, p_c], axis=0), m_bd)
                m_c = both[0:ch]
                p_c = p_c + both[ch:2 * ch]
            else:
                p_c = p_c + _dot(p_c, m_bd)
            sq *= 2
        r_c = eye_c - p_c - _dot3(p_c, n_bd)
        t_c = p_c + _dot(r_c, expand(p_c))
        uw = _dot3(expand(t_c), jnp.concatenate([vb, kb * egc[:, hs]], axis=1))
        per_head.append(dict(u=uw[:, 0:dk], w=uw[:, dk:2 * dk], qd=qh * egc[:, hs],
                             kdt=(kh * kdec[:, hs]).T, qk=qk_bd))

    lane_chunk = lax.broadcasted_iota(jnp.int32, (dk, tm), 1) // ch
    for c in range(nch):
        rows = slice(c * ch, (c + 1) * ch)
        for h in range(nh):
            hs = slice(h * dk, (h + 1) * dk)
            ph = per_head[h]
            s = s_ref[h]
            ws_qs = _dot(jnp.concatenate([ph["w"][rows], ph["qd"][rows]], axis=0), s)
            u = ph["u"][rows] - ws_qs[0:ch]
            u_s[h, rows, :] = u
            u_all = u_s[h]
            o_s[rows, hs] = ws_qs[ch:2 * ch] + _dot(ph["qk"][rows], u_all)
            kdt_c = jnp.where(lane_chunk == c, ph["kdt"], 0.0)
            s_ref[h] = s * elast[c * ch:c * ch + 1, hs] + _dot(kdt_c, u_all)

    z = jnp.dot(xb, wz_ref[...], preferred_element_type=F32)
    outs = []
    for h in range(nh):
        hs = slice(h * dk, (h + 1) * dk)
        outs.append(_rms(o_s[:, hs], onw_ref[...]))
    o = jnp.concatenate(outs, axis=1) * _silu(z)
    xo_ref[0] = x + _dot(o, wout_ref[...])
    ssm_ref[0] = s_ref[...]


def _ffn_sample_chunk(xn_ref, wg_ref, wv_ref, cwg_ref, cwv_ref, cbg_ref, cbv_ref, wdn_ref, st_ref, sto_ref):
    conv = []
    for half, (w_ref, cw_ref, cb_ref) in enumerate(((wg_ref, cwg_ref, cbg_ref), (wv_ref, cwv_ref, cbv_ref))):
        h = jnp.dot(xn_ref[...], w_ref[...], preferred_element_type=F32)
        conv.append(st_ref[half, 0] * cw_ref[0:1, :] + st_ref[half, 1] * cw_ref[1:2, :] + h * cw_ref[2:3, :]
                    + cb_ref[...])
        sto_ref[half, 0] = st_ref[half, 1]
        sto_ref[half, 1] = h
    a = (_silu(conv[0]) * conv[1]).astype(BF16)
    return jnp.dot(a, wdn_ref[...], preferred_element_type=F32)


def _sample_layer0_kernel(x_ref, pb_ref, n1_ref, pw_ref, ps_ref, n2_ref,
                          wg_ref, wv_ref, cwg_ref, cwv_ref, cbg_ref, cbv_ref, wdn_ref, st_ref,
                          xo_ref, pbo_ref, sto_ref, xn_s):
    c = pl.program_id(0)
    nbuf = pb_ref.shape[0]
    gd = pw_ref.shape[1]

    @pl.when(c == 0)
    def _():
        x = x_ref[...]
        h = _rms(x, n1_ref[...])
        parts = []
        for g, w in enumerate(POOL_WINDOWS):
            cols = slice(g * gd, (g + 1) * gd)
            s = h[:, cols]
            for j in range(1, w):
                s = s + pb_ref[nbuf - j, :, cols]
            dg = s / float(w) - h[:, cols]
            parts.append(x[:, cols] + _dot(dg, pw_ref[g]) * ps_ref[:, cols])
        x1 = jnp.concatenate(parts, axis=1)
        for j in range(nbuf - 1):
            pbo_ref[j] = pb_ref[j + 1]
        pbo_ref[nbuf - 1] = h
        xn_s[...] = _rms(x1, n2_ref[...]).astype(BF16)
        xo_ref[...] = x1

    xo_ref[...] += _ffn_sample_chunk(xn_s, wg_ref, wv_ref, cwg_ref, cwv_ref, cbg_ref, cbv_ref, wdn_ref,
                                     st_ref, sto_ref)


def _sample_dn_proj_kernel(x_ref, n1_ref, wqkv_ref, wz_ref, wab_ref, cw_ref, alog_ref, dtb_ref,
                           selg_ref, selb_ref, cst_ref,
                           w_ref, qd_ref, k_ref, vb_ref, qk_ref, eg_ref, z_ref, csto_ref):
    dk = DN_HEAD_DIM
    dn = wz_ref.shape[1]
    nh = dn // dk
    xb = _rms(x_ref[...], n1_ref[...]).astype(BF16)
    pre = jnp.dot(xb, wqkv_ref[...], preferred_element_type=F32)
    y = (cst_ref[0] * cw_ref[0:1, :] + cst_ref[1] * cw_ref[1:2, :] + cst_ref[2] * cw_ref[2:3, :]
         + pre * cw_ref[3:4, :])
    qkv = _silu(y)
    csto_ref[0] = cst_ref[1]
    csto_ref[1] = cst_ref[2]
    csto_ref[2] = pre
    z_ref[...] = jnp.dot(xb, wz_ref[...], preferred_element_type=F32)
    ab = jnp.dot(xb, wab_ref[...], preferred_element_type=F32)
    g = -jnp.exp(alog_ref[...]) * _softplus(ab + dtb_ref[...])
    eg = jnp.exp(_dot_exact_rhs(g, selg_ref[...]))
    betab = _dot_exact_rhs(1.0 / (1.0 + jnp.exp(-ab)), selb_ref[...])
    eg_ref[...] = eg
    for h in range(nh):
        hs = slice(h * dk, (h + 1) * dk)
        q = qkv[:, hs]
        k = qkv[:, dn + h * dk:dn + (h + 1) * dk]
        v = qkv[:, 2 * dn + h * dk:2 * dn + (h + 1) * dk]
        q = q * lax.rsqrt(jnp.sum(q * q, axis=-1, keepdims=True) + NORM_EPS) * (dk ** -0.5)
        k = k * lax.rsqrt(jnp.sum(k * k, axis=-1, keepdims=True) + NORM_EPS)
        kb = k * betab[:, hs]
        qk = jnp.sum(q.astype(BF16).astype(F32) * k.astype(BF16).astype(F32), axis=-1, keepdims=True)
        w_ref[:, hs] = kb * eg[:, hs]
        qd_ref[:, hs] = q * eg[:, hs]
        k_ref[:, hs] = k
        vb_ref[:, hs] = v * betab[:, hs]
        qk_ref[:, hs] = jnp.broadcast_to(qk, (q.shape[0], dk))


def _sample_dn_state_kernel(w_ref, qd_ref, k_ref, vb_ref, qk_ref, eg_ref, s_ref, o_ref, so_ref):
    dk = DN_HEAD_DIM
    nb, nh = s_ref.shape[0], s_ref.shape[1]
    eye = (lax.broadcasted_iota(jnp.int32, (dk, dk), 0) == lax.broadcasted_iota(jnp.int32, (dk, dk), 1))
    for b in range(nb):
        for h in range(nh):
            hs = slice(h * dk, (h + 1) * dk)
            s = s_ref[b, h]
            lhs = jnp.concatenate([jnp.broadcast_to(w_ref[b:b + 1, hs], (8, dk)),
                                   jnp.broadcast_to(qd_ref[b:b + 1, hs], (8, dk))], axis=0)
            r = _dot(lhs, s)
            u = vb_ref[b:b + 1, hs] - r[0:1]
            ub = u.astype(BF16).astype(F32)
            o_ref[b:b + 1, hs] = r[8:9] + qk_ref[b:b + 1, hs] * ub
            kdiag = jnp.where(eye, jnp.broadcast_to(k_ref[b:b + 1, hs], (dk, dk)), 0.0)
            so_ref[b, h] = s * eg_ref[b:b + 1, hs] + _dot(kdiag, jnp.broadcast_to(u, (dk, dk)))


def _sample_tail_kernel(x_ref, o_ref, z_ref, onw_ref, wout_ref, n2_ref,
                        wg_ref, wv_ref, cwg_ref, cwv_ref, cbg_ref, cbv_ref, wdn_ref, st_ref, fn_ref,
                        yo_ref, sto_ref, xn_s, acc_s):
    c = pl.program_id(0)
    dk = DN_HEAD_DIM

    @pl.when(c == 0)
    def _():
        outs = []
        for h in range(o_ref.shape[1] // dk):
            outs.append(_rms(o_ref[:, h * dk:(h + 1) * dk], onw_ref[...]))
        o = jnp.concatenate(outs, axis=1) * _silu(z_ref[...])
        x = x_ref[...] + _dot(o, wout_ref[...])
        acc_s[...] = x
        xn_s[...] = _rms(x, n2_ref[...]).astype(BF16)

    acc_s[...] += _ffn_sample_chunk(xn_s, wg_ref, wv_ref, cwg_ref, cwv_ref, cbg_ref, cbv_ref, wdn_ref,
                                    st_ref, sto_ref)

    @pl.when(c == pl.num_programs(0) - 1)
    def _():
        yo_ref[...] = _rms(acc_s[...], fn_ref[...])


def _params(*sem):
    return pltpu.CompilerParams(dimension_semantics=sem, vmem_limit_bytes=VMEM_LIMIT)


def kernel(x_prompt, x_sample, state_pool_buf, state_dn_conv, state_dn_ssm, state_ffn_conv, norm1_w, norm2_w,
           final_norm_w, pool_w, pool_scale, dn_w_in, dn_conv_w, dn_a_log, dn_dt_bias, dn_o_norm_w, dn_w_out,
           ffn_w_up, ffn_conv_w, ffn_conv_b, ffn_w_down):
    bp, seq, d = x_prompt.shape
    bs = x_sample.shape[0]
    nbuf = state_pool_buf.shape[2]
    dff = ffn_w_down.shape[1]
    nh = dn_a_log.shape[1]
    dk = DN_HEAD_DIM
    dn = nh * dk
    gd = pool_w.shape[2]
    tm = PROMPT_TILE
    nt = seq // tm
    nfc = dff // FFN_CHUNK
    assert seq % tm == 0 and dff % FFN_CHUNK == 0 and (3 * dn) % QKV_CHUNK == 0 and dk == 128
    assert nbuf + 1 == max(POOL_WINDOWS) and 2 * nh <= 128 and bs % SAMPLE_BLOCK == 0

    row = lambda v: v.reshape(1, -1)
    wup = ffn_w_up.astype(BF16)
    wdn = ffn_w_down.astype(BF16)
    pw = pool_w.astype(BF16)
    w_in = dn_w_in[0]
    wqkv = w_in[:, :3 * dn].astype(BF16)
    wz = w_in[:, 3 * dn:4 * dn].astype(BF16)
    wab = jnp.pad(w_in[:, 4 * dn:], ((0, 0), (0, 128 - 2 * nh))).astype(BF16)
    wout = dn_w_out[0].astype(BF16)
    alog = jnp.pad(dn_a_log[0], (0, 128 - nh)).reshape(1, 128)
    dtb = jnp.pad(dn_dt_bias[0], (0, 128 - nh)).reshape(1, 128)
    onw = row(dn_o_norm_w[0])
    lane_head = jnp.arange(dn) // dk
    selg = (jnp.arange(128)[:, None] == lane_head[None, :]).astype(BF16)
    selb = (jnp.arange(128)[:, None] == (lane_head[None, :] + nh)).astype(BF16)
    ti = jnp.arange(tm)
    ltri = ((ti[:, None] // DN_CHUNK == ti[None, :] // DN_CHUNK) & (ti[:, None] >= ti[None, :])).astype(BF16)

    x_spec = pl.BlockSpec((1, tm, d), lambda b, t: (b, t, 0))
    ffn_w_specs = [_resident((d, 2 * dff)), _resident((3, 2 * dff)), _resident((1, 2 * dff)), _resident((dff, d))]
    fst_spec = pl.BlockSpec((1, CONV_HALO, 2 * dff), lambda b, t: (b, 0, 0))
    fst_shape = jax.ShapeDtypeStruct((bp, CONV_HALO, 2 * dff), F32)

    x2, pool_tail, fst0 = pl.pallas_call(
        _prompt_layer0_kernel,
        grid=(bp, nt),
        in_specs=[x_spec, _resident((1, d)), _resident(pw.shape[1:]), _resident((1, d)), _resident((1, d))]
        + ffn_w_specs,
        out_specs=[x_spec, pl.BlockSpec((1, POOL_HALO, d), lambda b, t: (b, 0, 0)), fst_spec],
        out_shape=[jax.ShapeDtypeStruct((bp, seq, d), F32), jax.ShapeDtypeStruct((bp, POOL_HALO, d), F32), fst_shape],
        scratch_shapes=[pltpu.VMEM((POOL_HALO + tm, d), F32), pltpu.VMEM((CONV_HALO, 2 * dff), F32)],
        compiler_params=_params("arbitrary", "arbitrary"),
        name="prompt_layer0",
    )(x_prompt, row(norm1_w[0]), pw[0], row(pool_scale[0]), row(norm2_w[0]),
      wup[0], ffn_conv_w[0], row(ffn_conv_b[0]), wdn[0])

    x3, cst, ssm = pl.pallas_call(
        _prompt_deltanet_kernel,
        grid=(bp, nt),
        in_specs=[x_spec, _resident((1, d)), _resident((d, 3 * dn)), _resident((d, dn)), _resident((d, 128)),
                  _resident((4, 3 * dn)), _resident((1, 128)), _resident((1, 128)), _resident((1, dk)),
                  _resident((dn, d)), _resident((tm, tm)), _resident((128, dn)), _resident((128, dn))],
        out_specs=[x_spec, pl.BlockSpec((1, CONV_HALO, 3 * dn), lambda b, t: (b, 0, 0)),
                   pl.BlockSpec((1, nh, dk, dk), lambda b, t: (b, 0, 0, 0))],
        out_shape=[jax.ShapeDtypeStruct((bp, seq, d), F32), jax.ShapeDtypeStruct((bp, CONV_HALO, 3 * dn), F32),
                   jax.ShapeDtypeStruct((bp, nh, dk, dk), F32)],
        scratch_shapes=[pltpu.VMEM((CONV_HALO, 3 * dn), F32), pltpu.VMEM((nh, dk, dk), F32),
                        pltpu.VMEM((tm, 3 * dn), F32), pltpu.VMEM((nh, tm, dk), F32), pltpu.VMEM((tm, dn), F32)],
        compiler_params=_params("arbitrary", "arbitrary"),
        name="prompt_deltanet",
    )(x2, row(norm1_w[1]), wqkv, wz, wab, dn_conv_w[0], alog, dtb, onw, wout, ltri, selg, selb)

    y_prompt, fst1 = pl.pallas_call(
        _prompt_ffn_final_kernel,
        grid=(bp, nt),
        in_specs=[x_spec, _resident((1, d))] + ffn_w_specs + [_resident((1, d))],
        out_specs=[x_spec, fst_spec],
        out_shape=[jax.ShapeDtypeStruct((bp, seq, d), F32), fst_shape],
        scratch_shapes=[pltpu.VMEM((CONV_HALO, 2 * dff), F32)],
        compiler_params=_params("arbitrary", "arbitrary"),
        name="prompt_ffn_final",
    )(x3, row(norm2_w[1]), wup[1], ffn_conv_w[1], row(ffn_conv_b[1]), wdn[1], row(final_norm_w))

    xs = x_sample[:, 0, :]
    pb = jnp.transpose(state_pool_buf[0], (1, 0, 2))
    cs = jnp.transpose(state_dn_conv[0], (1, 0, 2))
    fs = jnp.transpose(state_ffn_conv.reshape(2, bs, 2, 2, dff), (0, 3, 2, 1, 4))

    chunk_w = lambda half: pl.BlockSpec((d, FFN_CHUNK), lambda c, half=half: (0, half * nfc + c))
    chunk_cw = lambda half: pl.BlockSpec((3, FFN_CHUNK), lambda c, half=half: (0, half * nfc + c))
    chunk_cb = lambda half: pl.BlockSpec((1, FFN_CHUNK), lambda c, half=half: (0, half * nfc + c))
    ffn_chunk_specs = [chunk_w(0), chunk_w(1), chunk_cw(0), chunk_cw(1), chunk_cb(0), chunk_cb(1),
                       pl.BlockSpec((FFN_CHUNK, d), lambda c: (c, 0)),
                       pl.BlockSpec((2, 2, bs, FFN_CHUNK), lambda c: (0, 0, 0, c))]
    sto_spec = pl.BlockSpec((2, 2, bs, FFN_CHUNK), lambda c: (0, 0, 0, c))
    sto_shape = jax.ShapeDtypeStruct((2, 2, bs, dff), F32)

    def ffn_chunk_args(layer):
        cb = row(ffn_conv_b[layer])
        return (wup[layer], wup[layer], ffn_conv_w[layer], ffn_conv_w[layer], cb, cb, wdn[layer], fs[layer])

    xs2, pbo, sto0 = pl.pallas_call(
        _sample_layer0_kernel,
        grid=(nfc,),
        in_specs=[_full((bs, d)), _full((nbuf, bs, d)), _full((1, d)), _full(pw.shape[1:]), _full((1, d)),
                  _full((1, d))] + ffn_chunk_specs,
        out_specs=[_full((bs, d)), _full((nbuf, bs, d)), sto_spec],
        out_shape=[jax.ShapeDtypeStruct((bs, d), F32), jax.ShapeDtypeStruct((nbuf, bs, d), F32), sto_shape],
        scratch_shapes=[pltpu.VMEM((bs, d), BF16)],
        compiler_params=_params("arbitrary"),
        name="sample_layer0",
    )(xs, pb, row(norm1_w[0]), pw[0], row(pool_scale[0]), row(norm2_w[0]), *ffn_chunk_args(0))

    vec = jax.ShapeDtypeStruct((bs, dn), F32)
    w_s, qd_s, k_s, vb_s, qk_s, eg_s, z_s, cso = pl.pallas_call(
        _sample_dn_proj_kernel,
        out_shape=[vec] * 7 + [jax.ShapeDtypeStruct((3, bs, 3 * dn), F32)],
        compiler_params=pltpu.CompilerParams(vmem_limit_bytes=VMEM_LIMIT),
        name="sample_dn_proj",
    )(xs2, row(norm1_w[1]), wqkv, wz, wab, dn_conv_w[0], alog, dtb, selg, selb, cs)

    nb = SAMPLE_BLOCK
    vspec = pl.BlockSpec((nb, dn), lambda i: (i, 0))
    sspec = pl.BlockSpec((nb, nh, dk, dk), lambda i: (i, 0, 0, 0))
    o_s, ssm_s = pl.pallas_call(
        _sample_dn_state_kernel,
        grid=(bs // nb,),
        in_specs=[vspec] * 6 + [sspec],
        out_specs=[vspec, sspec],
        out_shape=[vec, jax.ShapeDtypeStruct((bs, nh, dk, dk), F32)],
        compiler_params=_params("arbitrary"),
        name="sample_dn_state",
    )(w_s, qd_s, k_s, vb_s, qk_s, eg_s, state_dn_ssm[0])

    ys, sto1 = pl.pallas_call(
        _sample_tail_kernel,
        grid=(nfc,),
        in_specs=[_full((bs, d)), _full((bs, dn)), _full((bs, dn)), _full((1, dk)), _full((dn, d)), _full((1, d))]
        + ffn_chunk_specs + [_full((1, d))],
        out_specs=[_full((bs, d)), sto_spec],
        out_shape=[jax.ShapeDtypeStruct((bs, d), F32), sto_shape],
        scratch_shapes=[pltpu.VMEM((bs, d), BF16), pltpu.VMEM((bs, d), F32)],
        compiler_params=_params("arbitrary"),
        name="sample_tail",
    )(xs2, o_s, z_s, onw, wout, row(norm2_w[1]), *ffn_chunk_args(1), row(final_norm_w))

    ffn_state = lambda sto: jnp.transpose(sto, (2, 1, 0, 3)).reshape(bs, 2, 2 * dff)
    return (
        y_prompt,
        ys[:, None, :],
        pool_tail[None, :, POOL_HALO - nbuf:, :],
        jnp.transpose(pbo, (1, 0, 2))[None],
        cst[None, :, CONV_HALO - 3:, :],
        jnp.transpose(cso, (1, 0, 2))[None],
        ssm[None],
        ssm_s[None],
        jnp.stack([fst0[:, CONV_HALO - 2:, :], fst1[:, CONV_HALO - 2:, :]]),
        jnp.stack([ffn_state(sto0), ffn_state(sto1)]),
    )
```

```python
import functools

import jax
import jax.numpy as jnp
from jax import lax
from jax.experimental import pallas as pl
from jax.experimental.pallas import tpu as pltpu

F32 = jnp.float32
BF16 = jnp.bfloat16

NORM_EPS = 1e-6
POOL_WINDOWS = (2, 4, 8, 16)
POOL_HALO = 16
CONV_HALO = 8
DN_HEAD_DIM = 128
DN_CHUNK = 64
FFN_CHUNK = 256
QKV_CHUNK = 512
PROMPT_TILE = 256
HEAD_GROUP = 4
SAMPLE_BLOCK = 8
VMEM_LIMIT = 56 * 1024 * 1024


def _dot(a, b):
    return jnp.dot(a.astype(BF16), b.astype(BF16), preferred_element_type=F32)


def _dot_nt(a, b):
    return lax.dot_general(a.astype(BF16), b.astype(BF16), (((1,), (1,)), ((), ())),
                           preferred_element_type=F32)


def _split3(x):
    hi = x.astype(BF16)
    r = x - hi.astype(F32)
    mid = r.astype(BF16)
    lo = (r - mid.astype(F32)).astype(BF16)
    return hi, mid, lo


def _dot_exact_rhs(x, e):
    hi, mid, lo = _split3(x)
    return (jnp.dot(lo, e, preferred_element_type=F32) + jnp.dot(mid, e, preferred_element_type=F32)
            + jnp.dot(hi, e, preferred_element_type=F32))


def _dot_exact_lhs(e, x):
    hi, mid, lo = _split3(x)
    return (jnp.dot(e, lo, preferred_element_type=F32) + jnp.dot(e, mid, preferred_element_type=F32)
            + jnp.dot(e, hi, preferred_element_type=F32))


def _dot3(a, b):
    a_hi = a.astype(BF16)
    a_lo = (a - a_hi.astype(F32)).astype(BF16)
    b_hi = b.astype(BF16)
    b_lo = (b - b_hi.astype(F32)).astype(BF16)
    return (jnp.dot(a_lo, b_hi, preferred_element_type=F32) + jnp.dot(a_hi, b_lo, preferred_element_type=F32)
            + jnp.dot(a_hi, b_hi, preferred_element_type=F32))


def _rms(x, w):
    return x * lax.rsqrt(jnp.mean(x * x, axis=-1, keepdims=True) + NORM_EPS) * w


def _silu(x):
    return x / (1.0 + jnp.exp(-x))


def _softplus(x):
    return jnp.maximum(x, 0.0) + jnp.log1p(jnp.exp(-jnp.abs(x)))


def _full(shape):
    return pl.BlockSpec(shape, lambda *_: (0,) * len(shape))


def _resident(shape):
    return pl.BlockSpec(shape, lambda *_: (0,) * len(shape), pipeline_mode=pl.Buffered(1))


def _ffn_tile(xn_b, wup_ref, cw_ref, cb_ref, wdn_ref, carry_ref):
    tm = xn_b.shape[0]
    dff = wdn_ref.shape[0]
    acc = None
    for c in range(dff // FFN_CHUNK):
        conv = []
        for half in range(2):
            lo = half * dff + c * FFN_CHUNK
            cols = slice(lo, lo + FFN_CHUNK)
            h = jnp.dot(xn_b, wup_ref[:, cols], preferred_element_type=F32)
            ext = jnp.concatenate([carry_ref[:, cols], h], axis=0)
            y = (ext * cw_ref[2:3, cols] + pltpu.roll(ext, 1, 0) * cw_ref[1:2, cols]
                 + pltpu.roll(ext, 2, 0) * cw_ref[0:1, cols])
            conv.append(y[CONV_HALO:] + cb_ref[:, cols])
            carry_ref[:, cols] = h[tm - CONV_HALO:]
        a = (_silu(conv[0]) * conv[1]).astype(BF16)
        d = jnp.dot(a, wdn_ref[c * FFN_CHUNK:(c + 1) * FFN_CHUNK, :], preferred_element_type=F32)
        acc = d if acc is None else acc + d
    return acc


def _prompt_layer0_kernel(x_ref, n1_ref, pw_ref, ps_ref, n2_ref, wup_ref, cw_ref, cb_ref, wdn_ref,
                          xo_ref, pool_ref, fst_ref, hbuf, carry):
    t = pl.program_id(1)
    tm = x_ref.shape[1]
    gd = pw_ref.shape[1]

    @pl.when(t == 0)
    def _():
        hbuf[0:POOL_HALO, :] = jnp.zeros((POOL_HALO, hbuf.shape[1]), F32)
        carry[...] = jnp.zeros(carry.shape, F32)

    x = x_ref[0]
    h = _rms(x, n1_ref[...])
    hbuf[POOL_HALO:POOL_HALO + tm, :] = h
    pos1 = t * tm + lax.broadcasted_iota(jnp.int32, (tm, 1), 0) + 1
    parts = []
    for g, w in enumerate(POOL_WINDOWS):
        cols = slice(g * gd, (g + 1) * gd)
        s = hbuf[:, cols]
        sh = 1
        while sh < w:
            s = s + pltpu.roll(s, sh, 0)
            sh *= 2
        cnt = jnp.minimum(w, pos1).astype(F32)
        dg = s[POOL_HALO:] / cnt - h[:, cols]
        parts.append(x[:, cols] + _dot(dg, pw_ref[g]) * ps_ref[:, cols])
    x1 = jnp.concatenate(parts, axis=1)
    tail = hbuf[tm:tm + POOL_HALO, :]
    pool_ref[0] = tail
    hbuf[0:POOL_HALO, :] = tail

    xn = _rms(x1, n2_ref[...]).astype(BF16)
    xo_ref[0] = x1 + _ffn_tile(xn, wup_ref, cw_ref, cb_ref, wdn_ref, carry)
    fst_ref[0] = carry[...]


def _prompt_ffn_final_kernel(x_ref, n2_ref, wup_ref, cw_ref, cb_ref, wdn_ref, fn_ref,
                             yo_ref, fst_ref, carry):
    @pl.when(pl.program_id(1) == 0)
    def _():
        carry[...] = jnp.zeros(carry.shape, F32)

    x = x_ref[0]
    xn = _rms(x, n2_ref[...]).astype(BF16)
    x = x + _ffn_tile(xn, wup_ref, cw_ref, cb_ref, wdn_ref, carry)
    yo_ref[0] = _rms(x, fn_ref[...])
    fst_ref[0] = carry[...]


def _prompt_deltanet_kernel(x_ref, n1_ref, wqkv_ref, wz_ref, wab_ref, cw_ref, alog_ref, dtb_ref, onw_ref,
                            wout_ref, ltri_ref,
                            xo_ref, cst_ref, ssm_ref,
                            carry, s_ref, qkv_s, kb_s, q_s, k_s, qd_s, kd_s, vb_s, wr_s, gcol_s, uin_s, win_s,
                            qk_s, o_s):
    t = pl.program_id(1)
    tm = x_ref.shape[1]
    dk = DN_HEAD_DIM
    dn = s_ref.shape[1]
    nh = dn // dk
    ch = DN_CHUNK
    nch = tm // ch
    gh = HEAD_GROUP
    gw = gh * dk
    pw = gh * ch

    @pl.when(t == 0)
    def _():
        carry[...] = jnp.zeros(carry.shape, F32)
        s_ref[...] = jnp.zeros(s_ref.shape, F32)

    x = x_ref[0]
    xb = _rms(x, n1_ref[...]).astype(BF16)

    for c in range(3 * dn // QKV_CHUNK):
        cols = slice(c * QKV_CHUNK, (c + 1) * QKV_CHUNK)
        pre = jnp.dot(xb, wqkv_ref[:, cols], preferred_element_type=F32)
        ext = jnp.concatenate([carry[:, cols], pre], axis=0)
        y = (ext * cw_ref[3:4, cols] + pltpu.roll(ext, 1, 0) * cw_ref[2:3, cols]
             + pltpu.roll(ext, 2, 0) * cw_ref[1:2, cols] + pltpu.roll(ext, 3, 0) * cw_ref[0:1, cols])
        qkv_s[:, cols] = _silu(y[CONV_HALO:])
        carry[:, cols] = pre[tm - CONV_HALO:]
    cst_ref[0] = carry[...]

    ab = jnp.dot(xb, wab_ref[...], preferred_element_type=F32)
    g = -jnp.exp(alog_ref[...]) * _softplus(ab + dtb_ref[...])
    gc = _dot_exact_lhs(ltri_ref[...], g)
    sig = 1.0 / (1.0 + jnp.exp(-ab))

    for h in range(nh):
        hs = slice(h * dk, (h + 1) * dk)
        gcol = jnp.broadcast_to(gc[:, h:h + 1], (tm, dk))
        beta = jnp.broadcast_to(sig[:, nh + h:nh + h + 1], (tm, dk))
        glast = jnp.concatenate(
            [jnp.broadcast_to(gcol[(c + 1) * ch - 1:(c + 1) * ch, :], (ch, dk)) for c in range(nch)], axis=0)
        egc = jnp.exp(gcol)
        qh = qkv_s[:, hs]
        kh = qkv_s[:, dn + h * dk:dn + (h + 1) * dk]
        vh = qkv_s[:, 2 * dn + h * dk:2 * dn + (h + 1) * dk]
        qh = qh * (lax.rsqrt(jnp.sum(qh * qh, axis=-1, keepdims=True) + NORM_EPS) * (dk ** -0.5))
        kh = kh * lax.rsqrt(jnp.sum(kh * kh, axis=-1, keepdims=True) + NORM_EPS)
        kb = kh * beta
        gcol_s[:, hs] = gcol
        q_s[:, hs] = qh.astype(BF16)
        k_s[:, hs] = kh.astype(BF16)
        kb_s[:, hs] = kb.astype(BF16)
        qd_s[:, hs] = (qh * egc).astype(BF16)
        kd_s[:, hs] = (kh * jnp.exp(glast - gcol)).astype(BF16)
        vb_s[:, hs] = vh * beta
        wr_s[:, hs] = kb * egc

    def iota2(shape, d):
        return lax.broadcasted_iota(jnp.int32, shape, d)

    pi, pj = iota2((ch, pw), 0), iota2((ch, pw), 1) % ch
    eye_p = pi == pj
    incl_p = pi >= pj
    strict_p = pi > pj
    eye_f = eye_p.astype(F32)
    head_of_lane = iota2((ch, pw), 1) // ch
    bd_p = (iota2((pw, pw), 0) // ch) == (iota2((pw, pw), 1) // ch)
    bd_k = (iota2((2 * ch, 2 * dk), 0) // ch) == (iota2((2 * ch, 2 * dk), 1) // dk)
    bd_u = (iota2((pw, gw), 0) // ch) == (iota2((pw, gw), 1) // dk)
    bd_s = (iota2((2 * dk, 2 * dk), 0) // dk) == (iota2((2 * dk, 2 * dk), 1) // dk)

    def expand(mc):
        return jnp.where(bd_p, jnp.concatenate([mc] * gh, axis=0), 0.0).astype(BF16)

    units = [(c, gi) for c in range(nch) for gi in range(nh // gh)]
    n_cs = []
    for c, gi in units:
        rows = slice(c * ch, (c + 1) * ch)
        h0 = gi * gh
        sc = []
        for p in range(gh // 2):
            ps = slice((h0 + 2 * p) * dk, (h0 + 2 * p + 2) * dk)
            kc = k_s[rows, ps]
            kbd = jnp.where(bd_k, jnp.concatenate([kc, kc], axis=0), jnp.zeros((), BF16))
            sc.append(lax.dot_general(jnp.concatenate([kb_s[rows, ps], q_s[rows, ps]], axis=0), kbd,
                                      (((1,), (1,)), ((), ())), preferred_element_type=F32))
        sc = jnp.concatenate(sc, axis=1)
        gcol_p = jnp.broadcast_to(gcol_s[rows, h0 * dk:h0 * dk + 1], (ch, pw))
        for hl in range(1, gh):
            gcol_p = jnp.where(head_of_lane == hl,
                               jnp.broadcast_to(gcol_s[rows, (h0 + hl) * dk:(h0 + hl) * dk + 1], (ch, pw)),
                               gcol_p)
        grow_p = jnp.sum(jnp.where(eye_p, gcol_p, 0.0), axis=0, keepdims=True)
        decay = jnp.exp(jnp.where(incl_p, gcol_p - grow_p, -jnp.inf))
        n_cs.append(jnp.where(strict_p, sc[0:ch] * decay, 0.0))
        qk_s[c, gi] = (sc[ch:2 * ch] * decay).astype(BF16)

    m_cs = [jnp.dot(n_c.astype(BF16), expand(n_c), preferred_element_type=F32) for n_c in n_cs]
    p_cs = [eye_f - n_c for n_c in n_cs]
    sq = 2
    while sq < ch:
        for i in range(len(units)):
            m_bd = expand(m_cs[i])
            if 2 * sq < ch:
                both = jnp.dot(jnp.concatenate([m_cs[i], p_cs[i]], axis=0).astype(BF16), m_bd,
                               preferred_element_type=F32)
                m_cs[i] = both[0:ch]
                p_cs[i] = p_cs[i] + both[ch:2 * ch]
            else:
                p_cs[i] = p_cs[i] + jnp.dot(p_cs[i].astype(BF16), m_bd, preferred_element_type=F32)
        sq *= 2
    for (c, gi), p_c in zip(units, p_cs):
        rows = slice(c * ch, (c + 1) * ch)
        h0 = gi * gh
        rhs = jnp.concatenate(
            [jnp.concatenate([vb_s[rows, (h0 + hl) * dk:(h0 + hl + 1) * dk],
                              wr_s[rows, (h0 + hl) * dk:(h0 + hl + 1) * dk]], axis=1) for hl in range(gh)],
            axis=0)
        uw = rhs + jnp.dot(expand(p_c - eye_f), rhs.astype(BF16), preferred_element_type=F32)
        for hl in range(gh):
            hs = slice((h0 + hl) * dk, (h0 + hl + 1) * dk)
            uin_s[rows, hs] = uw[hl * ch:(hl + 1) * ch, 0:dk]
            win_s[rows, hs] = uw[hl * ch:(hl + 1) * ch, dk:2 * dk].astype(BF16)

    for c in range(nch):
        rows = slice(c * ch, (c + 1) * ch)
        last = (c + 1) * ch - 1
        for gi in range(nh // gh):
            gs = slice(gi * gw, (gi + 1) * gw)
            u_parts, qs_parts = [], []
            for p in range(gh // 2):
                ps = slice(gi * gw + 2 * p * dk, gi * gw + (2 * p + 2) * dk)
                s_pair = s_ref[:, ps]
                s_bd = jnp.where(bd_s, jnp.concatenate([s_pair, s_pair], axis=0), 0.0).astype(BF16)
                r = jnp.dot(jnp.concatenate([win_s[rows, ps], qd_s[rows, ps]], axis=0), s_bd,
                            preferred_element_type=F32)
                u_parts.append(uin_s[rows, ps] - r[0:ch])
                qs_parts.append(r[ch:2 * ch])
            u_g = jnp.concatenate(u_parts, axis=1)
            u_bd = jnp.where(bd_u, jnp.concatenate([u_g] * gh, axis=0), 0.0).astype(BF16)
            o_s[rows, gs] = jnp.concatenate(qs_parts, axis=1) + jnp.dot(qk_s[c, gi], u_bd,
                                                                          preferred_element_type=F32)
            kd_stack = jnp.concatenate(
                [kd_s[rows, gi * gw + hl * dk:gi * gw + (hl + 1) * dk] for hl in range(gh)], axis=0)
            s_ref[:, gs] = (s_ref[:, gs] * jnp.exp(gcol_s[last:last + 1, gs])
                            + lax.dot_general(kd_stack, u_bd, (((0,), (0,)), ((), ())),
                                              preferred_element_type=F32))

    z = jnp.dot(xb, wz_ref[...], preferred_element_type=F32)
    outs = []
    for h in range(nh):
        outs.append(_rms(o_s[:, h * dk:(h + 1) * dk], onw_ref[...]))
    o = jnp.concatenate(outs, axis=1) * _silu(z)
    xo_ref[0] = x + _dot(o, wout_ref[...])
    for h in range(nh):
        ssm_ref[0, h] = s_ref[:, h * dk:(h + 1) * dk]


def _ffn_sample_chunk(xn_ref, wg_ref, wv_ref, cwg_ref, cwv_ref, cbg_ref, cbv_ref, wdn_ref, st_ref, sto_ref):
    conv = []
    for half, (w_ref, cw_ref, cb_ref) in enumerate(((wg_ref, cwg_ref, cbg_ref), (wv_ref, cwv_ref, cbv_ref))):
        h = jnp.dot(xn_ref[...], w_ref[...], preferred_element_type=F32)
        conv.append(st_ref[half, 0] * cw_ref[0:1, :] + st_ref[half, 1] * cw_ref[1:2, :] + h * cw_ref[2:3, :]
                    + cb_ref[...])
        sto_ref[half, 0] = st_ref[half, 1]
        sto_ref[half, 1] = h
    a = (_silu(conv[0]) * conv[1]).astype(BF16)
    return jnp.dot(a, wdn_ref[...], preferred_element_type=F32)


def _sample_layer0_kernel(x_ref, pb_ref, n1_ref, pw_ref, ps_ref, n2_ref,
                          wg_ref, wv_ref, cwg_ref, cwv_ref, cbg_ref, cbv_ref, wdn_ref, st_ref,
                          xo_ref, pbo_ref, sto_ref, xn_s):
    c = pl.program_id(0)
    nbuf = pb_ref.shape[0]
    gd = pw_ref.shape[1]

    @pl.when(c == 0)
    def _():
        x = x_ref[...]
        h = _rms(x, n1_ref[...])
        parts = []
        for g, w in enumerate(POOL_WINDOWS):
            cols = slice(g * gd, (g + 1) * gd)
            s = h[:, cols]
            for j in range(1, w):
                s = s + pb_ref[nbuf - j, :, cols]
            dg = s / float(w) - h[:, cols]
            parts.append(x[:, cols] + _dot(dg, pw_ref[g]) * ps_ref[:, cols])
        x1 = jnp.concatenate(parts, axis=1)
        for j in range(nbuf - 1):
            pbo_ref[j] = pb_ref[j + 1]
        pbo_ref[nbuf - 1] = h
        xn_s[...] = _rms(x1, n2_ref[...]).astype(BF16)
        xo_ref[...] = x1

    xo_ref[...] += _ffn_sample_chunk(xn_s, wg_ref, wv_ref, cwg_ref, cwv_ref, cbg_ref, cbv_ref, wdn_ref,
                                     st_ref, sto_ref)


def _sample_dn_proj_kernel(x_ref, n1_ref, wqkv_ref, wz_ref, wab_ref, cw_ref, alog_ref, dtb_ref,
                           selg_ref, selb_ref, cst_ref,
                           w_ref, qd_ref, k_ref, vb_ref, qk_ref, eg_ref, z_ref, csto_ref):
    dk = DN_HEAD_DIM
    dn = wz_ref.shape[1]
    nh = dn // dk
    xb = _rms(x_ref[...], n1_ref[...]).astype(BF16)
    pre = jnp.dot(xb, wqkv_ref[...], preferred_element_type=F32)
    y = (cst_ref[0] * cw_ref[0:1, :] + cst_ref[1] * cw_ref[1:2, :] + cst_ref[2] * cw_ref[2:3, :]
         + pre * cw_ref[3:4, :])
    qkv = _silu(y)
    csto_ref[0] = cst_ref[1]
    csto_ref[1] = cst_ref[2]
    csto_ref[2] = pre
    z_ref[...] = jnp.dot(xb, wz_ref[...], preferred_element_type=F32)
    ab = jnp.dot(xb, wab_ref[...], preferred_element_type=F32)
    g = -jnp.exp(alog_ref[...]) * _softplus(ab + dtb_ref[...])
    eg = jnp.exp(_dot_exact_rhs(g, selg_ref[...]))
    betab = _dot_exact_rhs(1.0 / (1.0 + jnp.exp(-ab)), selb_ref[...])
    eg_ref[...] = eg
    for h in range(nh):
        hs = slice(h * dk, (h + 1) * dk)
        q = qkv[:, hs]
        k = qkv[:, dn + h * dk:dn + (h + 1) * dk]
        v = qkv[:, 2 * dn + h * dk:2 * dn + (h + 1) * dk]
        q = q * lax.rsqrt(jnp.sum(q * q, axis=-1, keepdims=True) + NORM_EPS) * (dk ** -0.5)
        k = k * lax.rsqrt(jnp.sum(k * k, axis=-1, keepdims=True) + NORM_EPS)
        kb = k * betab[:, hs]
        qk = jnp.sum(q.astype(BF16).astype(F32) * k.astype(BF16).astype(F32), axis=-1, keepdims=True)
        w_ref[:, hs] = kb * eg[:, hs]
        qd_ref[:, hs] = q * eg[:, hs]
        k_ref[:, hs] = k
        vb_ref[:, hs] = v * betab[:, hs]
        qk_ref[:, hs] = jnp.broadcast_to(qk, (q.shape[0], dk))


def _sample_dn_state_kernel(w_ref, qd_ref, k_ref, vb_ref, qk_ref, eg_ref, s_ref, o_ref, so_ref):
    dk = DN_HEAD_DIM
    nb, nh = s_ref.shape[0], s_ref.shape[1]
    eye = (lax.broadcasted_iota(jnp.int32, (dk, dk), 0) == lax.broadcasted_iota(jnp.int32, (dk, dk), 1))
    for b in range(nb):
        for h in range(nh):
            hs = slice(h * dk, (h + 1) * dk)
            s = s_ref[b, h]
            lhs = jnp.concatenate([jnp.broadcast_to(w_ref[b:b + 1, hs], (8, dk)),
                                   jnp.broadcast_to(qd_ref[b:b + 1, hs], (8, dk))], axis=0)
            r = _dot(lhs, s)
            u = vb_ref[b:b + 1, hs] - r[0:1]
            ub = u.astype(BF16).astype(F32)
            o_ref[b:b + 1, hs] = r[8:9] + qk_ref[b:b + 1, hs] * ub
            kdiag = jnp.where(eye, jnp.broadcast_to(k_ref[b:b + 1, hs], (dk, dk)), 0.0)
            so_ref[b, h] = s * eg_ref[b:b + 1, hs] + _dot(kdiag, jnp.broadcast_to(u, (dk, dk)))


def _sample_tail_kernel(x_ref, o_ref, z_ref, onw_ref, wout_ref, n2_ref,
                        wg_ref, wv_ref, cwg_ref, cwv_ref, cbg_ref, cbv_ref, wdn_ref, st_ref, fn_ref,
                        yo_ref, sto_ref, xn_s, acc_s):
    c = pl.program_id(0)
    dk = DN_HEAD_DIM

    @pl.when(c == 0)
    def _():
        outs = []
        for h in range(o_ref.shape[1] // dk):
            outs.append(_rms(o_ref[:, h * dk:(h + 1) * dk], onw_ref[...]))
        o = jnp.concatenate(outs, axis=1) * _silu(z_ref[...])
        x = x_ref[...] + _dot(o, wout_ref[...])
        acc_s[...] = x
        xn_s[...] = _rms(x, n2_ref[...]).astype(BF16)

    acc_s[...] += _ffn_sample_chunk(xn_s, wg_ref, wv_ref, cwg_ref, cwv_ref, cbg_ref, cbv_ref, wdn_ref,
                                    st_ref, sto_ref)

    @pl.when(c == pl.num_programs(0) - 1)
    def _():
        yo_ref[...] = _rms(acc_s[...], fn_ref[...])


def _params(*sem):
    return pltpu.CompilerParams(dimension_semantics=sem, vmem_limit_bytes=VMEM_LIMIT)


def kernel(x_prompt, x_sample, state_pool_buf, state_dn_conv, state_dn_ssm, state_ffn_conv, norm1_w, norm2_w,
           final_norm_w, pool_w, pool_scale, dn_w_in, dn_conv_w, dn_a_log, dn_dt_bias, dn_o_norm_w, dn_w_out,
           ffn_w_up, ffn_conv_w, ffn_conv_b, ffn_w_down):
    bp, seq, d = x_prompt.shape
    bs = x_sample.shape[0]
    nbuf = state_pool_buf.shape[2]
    dff = ffn_w_down.shape[1]
    nh = dn_a_log.shape[1]
    dk = DN_HEAD_DIM
    dn = nh * dk
    gd = pool_w.shape[2]
    tm = PROMPT_TILE
    nt = seq // tm
    nfc = dff // FFN_CHUNK
    assert seq % tm == 0 and dff % FFN_CHUNK == 0 and (3 * dn) % QKV_CHUNK == 0 and dk == 128
    assert nbuf + 1 == max(POOL_WINDOWS) and 2 * nh <= 128 and bs % SAMPLE_BLOCK == 0

    row = lambda v: v.reshape(1, -1)
    wup = ffn_w_up.astype(BF16)
    wdn = ffn_w_down.astype(BF16)
    pw = pool_w.astype(BF16)
    w_in = dn_w_in[0]
    wqkv = w_in[:, :3 * dn].astype(BF16)
    wz = w_in[:, 3 * dn:4 * dn].astype(BF16)
    wab = jnp.pad(w_in[:, 4 * dn:], ((0, 0), (0, 128 - 2 * nh))).astype(BF16)
    wout = dn_w_out[0].astype(BF16)
    alog = jnp.pad(dn_a_log[0], (0, 128 - nh)).reshape(1, 128)
    dtb = jnp.pad(dn_dt_bias[0], (0, 128 - nh)).reshape(1, 128)
    onw = row(dn_o_norm_w[0])
    lane_head = jnp.arange(dn) // dk
    selg = (jnp.arange(128)[:, None] == lane_head[None, :]).astype(BF16)
    selb = (jnp.arange(128)[:, None] == (lane_head[None, :] + nh)).astype(BF16)
    ti = jnp.arange(tm)
    ltri = ((ti[:, None] // DN_CHUNK == ti[None, :] // DN_CHUNK) & (ti[:, None] >= ti[None, :])).astype(BF16)

    x_spec = pl.BlockSpec((1, tm, d), lambda b, t: (b, t, 0))
    ffn_w_specs = [_resident((d, 2 * dff)), _resident((3, 2 * dff)), _resident((1, 2 * dff)), _resident((dff, d))]
    fst_spec = pl.BlockSpec((1, CONV_HALO, 2 * dff), lambda b, t: (b, 0, 0))
    fst_shape = jax.ShapeDtypeStruct((bp, CONV_HALO, 2 * dff), F32)

    x2, pool_tail, fst0 = pl.pallas_call(
        _prompt_layer0_kernel,
        grid=(bp, nt),
        in_specs=[x_spec, _resident((1, d)), _resident(pw.shape[1:]), _resident((1, d)), _resident((1, d))]
        + ffn_w_specs,
        out_specs=[x_spec, pl.BlockSpec((1, POOL_HALO, d), lambda b, t: (b, 0, 0)), fst_spec],
        out_shape=[jax.ShapeDtypeStruct((bp, seq, d), F32), jax.ShapeDtypeStruct((bp, POOL_HALO, d), F32), fst_shape],
        scratch_shapes=[pltpu.VMEM((POOL_HALO + tm, d), F32), pltpu.VMEM((CONV_HALO, 2 * dff), F32)],
        compiler_params=_params("arbitrary", "arbitrary"),
        name="prompt_layer0",
    )(x_prompt, row(norm1_w[0]), pw[0], row(pool_scale[0]), row(norm2_w[0]),
      wup[0], ffn_conv_w[0], row(ffn_conv_b[0]), wdn[0])

    x3, cst, ssm = pl.pallas_call(
        _prompt_deltanet_kernel,
        grid=(bp, nt),
        in_specs=[x_spec, _resident((1, d)), _resident((d, 3 * dn)), _resident((d, dn)), _resident((d, 128)),
                  _resident((4, 3 * dn)), _resident((1, 128)), _resident((1, 128)), _resident((1, dk)),
                  _resident((dn, d)), _resident((tm, tm))],
        out_specs=[x_spec, pl.BlockSpec((1, CONV_HALO, 3 * dn), lambda b, t: (b, 0, 0)),
                   pl.BlockSpec((1, nh, dk, dk), lambda b, t: (b, 0, 0, 0))],
        out_shape=[jax.ShapeDtypeStruct((bp, seq, d), F32), jax.ShapeDtypeStruct((bp, CONV_HALO, 3 * dn), F32),
                   jax.ShapeDtypeStruct((bp, nh, dk, dk), F32)],
        scratch_shapes=[pltpu.VMEM((CONV_HALO, 3 * dn), F32), pltpu.VMEM((dk, dn), F32),
                        pltpu.VMEM((tm, 3 * dn), F32)]
        + [pltpu.VMEM((tm, dn), BF16)] * 5 + [pltpu.VMEM((tm, dn), F32)] * 4 + [pltpu.VMEM((tm, dn), BF16)]
        + [pltpu.VMEM((tm // DN_CHUNK, nh // HEAD_GROUP, DN_CHUNK, HEAD_GROUP * DN_CHUNK), BF16),
           pltpu.VMEM((tm, dn), F32)],
        compiler_params=_params("arbitrary", "arbitrary"),
        name="prompt_deltanet",
    )(x2, row(norm1_w[1]), wqkv, wz, wab, dn_conv_w[0], alog, dtb, onw, wout, ltri)

    y_prompt, fst1 = pl.pallas_call(
        _prompt_ffn_final_kernel,
        grid=(bp, nt),
        in_specs=[x_spec, _resident((1, d))] + ffn_w_specs + [_resident((1, d))],
        out_specs=[x_spec, fst_spec],
        out_shape=[jax.ShapeDtypeStruct((bp, seq, d), F32), fst_shape],
        scratch_shapes=[pltpu.VMEM((CONV_HALO, 2 * dff), F32)],
        compiler_params=_params("arbitrary", "arbitrary"),
        name="prompt_ffn_final",
    )(x3, row(norm2_w[1]), wup[1], ffn_conv_w[1], row(ffn_conv_b[1]), wdn[1], row(final_norm_w))

    xs = x_sample[:, 0, :]
    pb = jnp.transpose(state_pool_buf[0], (1, 0, 2))
    cs = jnp.transpose(state_dn_conv[0], (1, 0, 2))
    fs = jnp.transpose(state_ffn_conv.reshape(2, bs, 2, 2, dff), (0, 3, 2, 1, 4))

    chunk_w = lambda half: pl.BlockSpec((d, FFN_CHUNK), lambda c, half=half: (0, half * nfc + c))
    chunk_cw = lambda half: pl.BlockSpec((3, FFN_CHUNK), lambda c, half=half: (0, half * nfc + c))
    chunk_cb = lambda half: pl.BlockSpec((1, FFN_CHUNK), lambda c, half=half: (0, half * nfc + c))
    ffn_chunk_specs = [chunk_w(0), chunk_w(1), chunk_cw(0), chunk_cw(1), chunk_cb(0), chunk_cb(1),
                       pl.BlockSpec((FFN_CHUNK, d), lambda c: (c, 0)),
                       pl.BlockSpec((2, 2, bs, FFN_CHUNK), lambda c: (0, 0, 0, c))]
    sto_spec = pl.BlockSpec((2, 2, bs, FFN_CHUNK), lambda c: (0, 0, 0, c))
    sto_shape = jax.ShapeDtypeStruct((2, 2, bs, dff), F32)

    def ffn_chunk_args(layer):
        cb = row(ffn_conv_b[layer])
        return (wup[layer], wup[layer], ffn_conv_w[layer], ffn_conv_w[layer], cb, cb, wdn[layer], fs[layer])

    xs2, pbo, sto0 = pl.pallas_call(
        _sample_layer0_kernel,
        grid=(nfc,),
        in_specs=[_full((bs, d)), _full((nbuf, bs, d)), _full((1, d)), _full(pw.shape[1:]), _full((1, d)),
                  _full((1, d))] + ffn_chunk_specs,
        out_specs=[_full((bs, d)), _full((nbuf, bs, d)), sto_spec],
        out_shape=[jax.ShapeDtypeStruct((bs, d), F32), jax.ShapeDtypeStruct((nbuf, bs, d), F32), sto_shape],
        scratch_shapes=[pltpu.VMEM((bs, d), BF16)],
        compiler_params=_params("arbitrary"),
        name="sample_layer0",
    )(xs, pb, row(norm1_w[0]), pw[0], row(pool_scale[0]), row(norm2_w[0]), *ffn_chunk_args(0))

    vec = jax.ShapeDtypeStruct((bs, dn), F32)
    w_s, qd_s, k_s, vb_s, qk_s, eg_s, z_s, cso = pl.pallas_call(
        _sample_dn_proj_kernel,
        out_shape=[vec] * 7 + [jax.ShapeDtypeStruct((3, bs, 3 * dn), F32)],
        compiler_params=pltpu.CompilerParams(vmem_limit_bytes=VMEM_LIMIT),
        name="sample_dn_proj",
    )(xs2, row(norm1_w[1]), wqkv, wz, wab, dn_conv_w[0], alog, dtb, selg, selb, cs)

    nb = SAMPLE_BLOCK
    vspec = pl.BlockSpec((nb, dn), lambda i: (i, 0))
    sspec = pl.BlockSpec((nb, nh, dk, dk), lambda i: (i, 0, 0, 0))
    o_s, ssm_s = pl.pallas_call(
        _sample_dn_state_kernel,
        grid=(bs // nb,),
        in_specs=[vspec] * 6 + [sspec],
        out_specs=[vspec, sspec],
        out_shape=[vec, jax.ShapeDtypeStruct((bs, nh, dk, dk), F32)],
        compiler_params=_params("arbitrary"),
        name="sample_dn_state",
    )(w_s, qd_s, k_s, vb_s, qk_s, eg_s, state_dn_ssm[0])

    ys, sto1 = pl.pallas_call(
        _sample_tail_kernel,
        grid=(nfc,),
        in_specs=[_full((bs, d)), _full((bs, dn)), _full((bs, dn)), _full((1, dk)), _full((dn, d)), _full((1, d))]
        + ffn_chunk_specs + [_full((1, d))],
        out_specs=[_full((bs, d)), sto_spec],
        out_shape=[jax.ShapeDtypeStruct((bs, d), F32), sto_shape],
        scratch_shapes=[pltpu.VMEM((bs, d), BF16), pltpu.VMEM((bs, d), F32)],
        compiler_params=_params("arbitrary"),
        name="sample_tail",
    )(xs2, o_s, z_s, onw, wout, row(norm2_w[1]), *ffn_chunk_args(1), row(final_norm_w))

    ffn_state = lambda sto: jnp.transpose(sto, (2, 1, 0, 3)).reshape(bs, 2, 2 * dff)
    return (
        y_prompt,
        ys[:, None, :],
        pool_tail[None, :, POOL_HALO - nbuf:, :],
        jnp.transpose(pbo, (1, 0, 2))[None],
        cst[None, :, CONV_HALO - 3:, :],
        jnp.transpose(cso, (1, 0, 2))[None],
        ssm[None],
        ssm_s[None],
        jnp.stack([fst0[:, CONV_HALO - 2:, :], fst1[:, CONV_HALO - 2:, :]]),
        jnp.stack([ffn_state(sto0), ffn_state(sto1)]),
    )
```

```python
import functools

import jax
import jax.numpy as jnp
from jax import lax
from jax.experimental import pallas as pl
from jax.experimental.pallas import tpu as pltpu

F32 = jnp.float32
BF16 = jnp.bfloat16

NORM_EPS = 1e-6
POOL_WINDOWS = (2, 4, 8, 16)
POOL_HALO = 16
CONV_HALO = 8
DN_HEAD_DIM = 128
DN_CHUNK = 64
FFN_CHUNK = 256
FFN_DOWN_GROUP = 2
FFN_LOOKAHEAD = 3
QKV_CHUNK = 512
PROMPT_TILE = 256
HEAD_GROUP = 4
SAMPLE_BLOCK = 8
VMEM_LIMIT = 56 * 1024 * 1024


def _dot(a, b):
    return jnp.dot(a.astype(BF16), b.astype(BF16), preferred_element_type=F32)


def _dot_nt(a, b):
    return lax.dot_general(a.astype(BF16), b.astype(BF16), (((1,), (1,)), ((), ())),
                           preferred_element_type=F32)


def _split3(x):
    hi = x.astype(BF16)
    r = x - hi.astype(F32)
    mid = r.astype(BF16)
    lo = (r - mid.astype(F32)).astype(BF16)
    return hi, mid, lo


def _dot_exact_rhs(x, e):
    hi, mid, lo = _split3(x)
    return (jnp.dot(lo, e, preferred_element_type=F32) + jnp.dot(mid, e, preferred_element_type=F32)
            + jnp.dot(hi, e, preferred_element_type=F32))


def _dot_exact_lhs(e, x):
    hi, mid, lo = _split3(x)
    return (jnp.dot(e, lo, preferred_element_type=F32) + jnp.dot(e, mid, preferred_element_type=F32)
            + jnp.dot(e, hi, preferred_element_type=F32))


def _dot3(a, b):
    a_hi = a.astype(BF16)
    a_lo = (a - a_hi.astype(F32)).astype(BF16)
    b_hi = b.astype(BF16)
    b_lo = (b - b_hi.astype(F32)).astype(BF16)
    return (jnp.dot(a_lo, b_hi, preferred_element_type=F32) + jnp.dot(a_hi, b_lo, preferred_element_type=F32)
            + jnp.dot(a_hi, b_hi, preferred_element_type=F32))


def _rms(x, w):
    return x * lax.rsqrt(jnp.mean(x * x, axis=-1, keepdims=True) + NORM_EPS) * w


def _silu(x):
    return x / (1.0 + jnp.exp(-x))


def _softplus(x):
    return jnp.maximum(x, 0.0) + jnp.log1p(jnp.exp(-jnp.abs(x)))


def _full(shape):
    return pl.BlockSpec(shape, lambda *_: (0,) * len(shape))


def _resident(shape):
    return pl.BlockSpec(shape, lambda *_: (0,) * len(shape), pipeline_mode=pl.Buffered(1))


def _ffn_tile(xn_b, wup_ref, cw_ref, cb_ref, wdn_ref, carry_ref):
    tm = xn_b.shape[0]
    dff = wdn_ref.shape[0]
    nchunk = dff // FFN_CHUNK

    def up(c):
        return [jnp.dot(xn_b, wup_ref[:, half * dff + c * FFN_CHUNK:half * dff + (c + 1) * FFN_CHUNK],
                        preferred_element_type=F32) for half in range(2)]

    def conv_act(c, hs):
        conv = []
        for half, h in enumerate(hs):
            cols = slice(half * dff + c * FFN_CHUNK, half * dff + (c + 1) * FFN_CHUNK)
            ext = jnp.concatenate([carry_ref[:, cols], h], axis=0)
            y = (ext * cw_ref[2:3, cols] + pltpu.roll(ext, 1, 0) * cw_ref[1:2, cols]
                 + pltpu.roll(ext, 2, 0) * cw_ref[0:1, cols])
            conv.append(y[CONV_HALO:] + cb_ref[:, cols])
            carry_ref[:, cols] = h[tm - CONV_HALO:]
        return (_silu(conv[0]) * conv[1]).astype(BF16)

    acc = None
    acts = []
    ahead = [up(c) for c in range(min(FFN_LOOKAHEAD, nchunk))]
    for c in range(nchunk):
        cur = ahead.pop(0)
        if c + FFN_LOOKAHEAD < nchunk:
            ahead.append(up(c + FFN_LOOKAHEAD))
        acts.append(conv_act(c, cur))
        if len(acts) == FFN_DOWN_GROUP or c + 1 == nchunk:
            c0 = c + 1 - len(acts)
            a = acts[0] if len(acts) == 1 else jnp.concatenate(acts, axis=1)
            d = jnp.dot(a, wdn_ref[c0 * FFN_CHUNK:(c + 1) * FFN_CHUNK, :], preferred_element_type=F32)
            acc = d if acc is None else acc + d
            acts = []
    return acc


def _prompt_layer0_kernel(x_ref, n1_ref, pw_ref, ps_ref, n2_ref, wup_ref, cw_ref, cb_ref, wdn_ref,
                          xo_ref, pool_ref, fst_ref, hbuf, carry):
    t = pl.program_id(1)
    tm = x_ref.shape[1]
    gd = pw_ref.shape[1]

    @pl.when(t == 0)
    def _():
        hbuf[0:POOL_HALO, :] = jnp.zeros((POOL_HALO, hbuf.shape[1]), F32)
        carry[...] = jnp.zeros(carry.shape, F32)

    x = x_ref[0]
    h = _rms(x, n1_ref[...])
    hbuf[POOL_HALO:POOL_HALO + tm, :] = h
    pos1 = t * tm + lax.broadcasted_iota(jnp.int32, (tm, 1), 0) + 1
    parts = []
    for g, w in enumerate(POOL_WINDOWS):
        cols = slice(g * gd, (g + 1) * gd)
        s = hbuf[:, cols]
        sh = 1
        while sh < w:
            s = s + pltpu.roll(s, sh, 0)
            sh *= 2
        cnt = jnp.minimum(w, pos1).astype(F32)
        dg = s[POOL_HALO:] / cnt - h[:, cols]
        parts.append(x[:, cols] + _dot(dg, pw_ref[g]) * ps_ref[:, cols])
    x1 = jnp.concatenate(parts, axis=1)
    tail = hbuf[tm:tm + POOL_HALO, :]
    pool_ref[0] = tail
    hbuf[0:POOL_HALO, :] = tail

    xn = _rms(x1, n2_ref[...]).astype(BF16)
    xo_ref[0] = x1 + _ffn_tile(xn, wup_ref, cw_ref, cb_ref, wdn_ref, carry)
    fst_ref[0] = carry[...]


def _prompt_ffn_final_kernel(x_ref, n2_ref, wup_ref, cw_ref, cb_ref, wdn_ref, fn_ref,
                             yo_ref, fst_ref, carry):
    @pl.when(pl.program_id(1) == 0)
    def _():
        carry[...] = jnp.zeros(carry.shape, F32)

    x = x_ref[0]
    xn = _rms(x, n2_ref[...]).astype(BF16)
    x = x + _ffn_tile(xn, wup_ref, cw_ref, cb_ref, wdn_ref, carry)
    yo_ref[0] = _rms(x, fn_ref[...])
    fst_ref[0] = carry[...]


def _prompt_deltanet_kernel(x_ref, n1_ref, wqkv_ref, wz_ref, wab_ref, cw_ref, alog_ref, dtb_ref, onw_ref,
                            wout_ref, ltri_ref,
                            xo_ref, cst_ref, ssm_ref,
                            carry, s_ref, qkv_s, kb_s, q_s, k_s, qd_s, kd_s, vb_s, wr_s, gcol_s, uin_s, win_s,
                            qk_s, o_s):
    t = pl.program_id(1)
    tm = x_ref.shape[1]
    dk = DN_HEAD_DIM
    dn = s_ref.shape[1]
    nh = dn // dk
    ch = DN_CHUNK
    nch = tm // ch
    gh = HEAD_GROUP
    gw = gh * dk
    pw = gh * ch

    @pl.when(t == 0)
    def _():
        carry[...] = jnp.zeros(carry.shape, F32)
        s_ref[...] = jnp.zeros(s_ref.shape, F32)

    x = x_ref[0]
    xb = _rms(x, n1_ref[...]).astype(BF16)

    for c in range(3 * dn // QKV_CHUNK):
        cols = slice(c * QKV_CHUNK, (c + 1) * QKV_CHUNK)
        pre = jnp.dot(xb, wqkv_ref[:, cols], preferred_element_type=F32)
        ext = jnp.concatenate([carry[:, cols], pre], axis=0)
        prev = pltpu.roll(ext, 1, 0)
        y = (ext * cw_ref[3:4, cols] + prev * cw_ref[2:3, cols]
             + pltpu.roll(ext * cw_ref[1:2, cols] + prev * cw_ref[0:1, cols], 2, 0))
        qkv_s[:, cols] = _silu(y[CONV_HALO:])
        carry[:, cols] = pre[tm - CONV_HALO:]
    cst_ref[0] = carry[...]

    ab = jnp.dot(xb, wab_ref[...], preferred_element_type=F32)
    g = -jnp.exp(alog_ref[...]) * _softplus(ab + dtb_ref[...])
    gc = _dot_exact_lhs(ltri_ref[...], g)
    sig = 1.0 / (1.0 + jnp.exp(-ab))

    for h in range(nh):
        hs = slice(h * dk, (h + 1) * dk)
        gcol = jnp.broadcast_to(gc[:, h:h + 1], (tm, dk))
        beta = jnp.broadcast_to(sig[:, nh + h:nh + h + 1], (tm, dk))
        glast = jnp.concatenate(
            [jnp.broadcast_to(gcol[(c + 1) * ch - 1:(c + 1) * ch, :], (ch, dk)) for c in range(nch)], axis=0)
        egc = jnp.exp(gcol)
        qh = qkv_s[:, hs]
        kh = qkv_s[:, dn + h * dk:dn + (h + 1) * dk]
        vh = qkv_s[:, 2 * dn + h * dk:2 * dn + (h + 1) * dk]
        qh = qh * (lax.rsqrt(jnp.sum(qh * qh, axis=-1, keepdims=True) + NORM_EPS) * (dk ** -0.5))
        kh = kh * lax.rsqrt(jnp.sum(kh * kh, axis=-1, keepdims=True) + NORM_EPS)
        kb = kh * beta
        gcol_s[:, hs] = gcol
        q_s[:, hs] = qh.astype(BF16)
        k_s[:, hs] = kh.astype(BF16)
        kb_s[:, hs] = kb.astype(BF16)
        qd_s[:, hs] = (qh * egc).astype(BF16)
        kd_s[:, hs] = (kh * jnp.exp(glast - gcol)).astype(BF16)
        vb_s[:, hs] = vh * beta
        wr_s[:, hs] = kb * egc

    def iota2(shape, d):
        return lax.broadcasted_iota(jnp.int32, shape, d)

    pi, pj = iota2((ch, pw), 0), iota2((ch, pw), 1) % ch
    eye_p = pi == pj
    incl_p = pi >= pj
    strict_p = pi > pj
    eye_f = eye_p.astype(F32)
    head_of_lane = iota2((ch, pw), 1) // ch
    bd_p = (iota2((pw, pw), 0) // ch) == (iota2((pw, pw), 1) // ch)
    bd_k = (iota2((2 * ch, 2 * dk), 0) // ch) == (iota2((2 * ch, 2 * dk), 1) // dk)
    bd_u = (iota2((pw, gw), 0) // ch) == (iota2((pw, gw), 1) // dk)
    bd_s = (iota2((2 * dk, 2 * dk), 0) // dk) == (iota2((2 * dk, 2 * dk), 1) // dk)

    def expand(mc):
        return jnp.where(bd_p, jnp.concatenate([mc] * gh, axis=0), 0.0).astype(BF16)

    units = [(c, gi) for c in range(nch) for gi in range(nh // gh)]
    n_cs = []
    for c, gi in units:
        rows = slice(c * ch, (c + 1) * ch)
        h0 = gi * gh
        sc = []
        for p in range(gh // 2):
            ps = slice((h0 + 2 * p) * dk, (h0 + 2 * p + 2) * dk)
            kc = k_s[rows, ps]
            kbd = jnp.where(bd_k, jnp.concatenate([kc, kc], axis=0), jnp.zeros((), BF16))
            sc.append(lax.dot_general(jnp.concatenate([kb_s[rows, ps], q_s[rows, ps]], axis=0), kbd,
                                      (((1,), (1,)), ((), ())), preferred_element_type=F32))
        sc = jnp.concatenate(sc, axis=1)
        gcol_p = jnp.broadcast_to(gcol_s[rows, h0 * dk:h0 * dk + 1], (ch, pw))
        for hl in range(1, gh):
            gcol_p = jnp.where(head_of_lane == hl,
                               jnp.broadcast_to(gcol_s[rows, (h0 + hl) * dk:(h0 + hl) * dk + 1], (ch, pw)),
                               gcol_p)
        grow_p = jnp.sum(jnp.where(eye_p, gcol_p, 0.0), axis=0, keepdims=True)
        decay = jnp.exp(jnp.where(incl_p, gcol_p - grow_p, -jnp.inf))
        n_cs.append(jnp.where(strict_p, sc[0:ch] * decay, 0.0))
        qk_s[c, gi] = (sc[ch:2 * ch] * decay).astype(BF16)

    m_cs = [jnp.dot(n_c.astype(BF16), expand(n_c), preferred_element_type=F32) for n_c in n_cs]
    p_cs = [eye_f - n_c for n_c in n_cs]
    sq = 2
    while sq < ch:
        for i in range(len(units)):
            m_bd = expand(m_cs[i])
            if 2 * sq < ch:
                both = jnp.dot(jnp.concatenate([m_cs[i], p_cs[i]], axis=0).astype(BF16), m_bd,
                               preferred_element_type=F32)
                m_cs[i] = both[0:ch]
                p_cs[i] = p_cs[i] + both[ch:2 * ch]
            else:
                p_cs[i] = p_cs[i] + jnp.dot(p_cs[i].astype(BF16), m_bd, preferred_element_type=F32)
        sq *= 2
    for (c, gi), p_c in zip(units, p_cs):
        rows = slice(c * ch, (c + 1) * ch)
        h0 = gi * gh
        rhs = jnp.concatenate(
            [jnp.concatenate([vb_s[rows, (h0 + hl) * dk:(h0 + hl + 1) * dk],
                              wr_s[rows, (h0 + hl) * dk:(h0 + hl + 1) * dk]], axis=1) for hl in range(gh)],
            axis=0)
        uw = rhs + jnp.dot(expand(p_c - eye_f), rhs.astype(BF16), preferred_element_type=F32)
        for hl in range(gh):
            hs = slice((h0 + hl) * dk, (h0 + hl + 1) * dk)
            uin_s[rows, hs] = uw[hl * ch:(hl + 1) * ch, 0:dk]
            win_s[rows, hs] = uw[hl * ch:(hl + 1) * ch, dk:2 * dk].astype(BF16)

    for c in range(nch):
        rows = slice(c * ch, (c + 1) * ch)
        last = (c + 1) * ch - 1
        for gi in range(nh // gh):
            gs = slice(gi * gw, (gi + 1) * gw)
            u_parts, qs_parts = [], []
            for p in range(gh // 2):
                ps = slice(gi * gw + 2 * p * dk, gi * gw + (2 * p + 2) * dk)
                s_pair = s_ref[:, ps]
                s_bd = jnp.where(bd_s, jnp.concatenate([s_pair, s_pair], axis=0), 0.0).astype(BF16)
                r = jnp.dot(jnp.concatenate([win_s[rows, ps], qd_s[rows, ps]], axis=0), s_bd,
                            preferred_element_type=F32)
                u_parts.append(uin_s[rows, ps] - r[0:ch])
                qs_parts.append(r[ch:2 * ch])
            u_g = jnp.concatenate(u_parts, axis=1)
            u_bd = jnp.where(bd_u, jnp.concatenate([u_g] * gh, axis=0), 0.0).astype(BF16)
            o_s[rows, gs] = jnp.concatenate(qs_parts, axis=1) + jnp.dot(qk_s[c, gi], u_bd,
                                                                          preferred_element_type=F32)
            kd_stack = jnp.concatenate(
                [kd_s[rows, gi * gw + hl * dk:gi * gw + (hl + 1) * dk] for hl in range(gh)], axis=0)
            s_ref[:, gs] = (s_ref[:, gs] * jnp.exp(gcol_s[last:last + 1, gs])
                            + lax.dot_general(kd_stack, u_bd, (((0,), (0,)), ((), ())),
                                              preferred_element_type=F32))

    z = jnp.dot(xb, wz_ref[...], preferred_element_type=F32)
    outs = []
    for h in range(nh):
        outs.append(_rms(o_s[:, h * dk:(h + 1) * dk], onw_ref[...]))
    o = jnp.concatenate(outs, axis=1) * _silu(z)
    xo_ref[0] = x + _dot(o, wout_ref[...])
    for h in range(nh):
        ssm_ref[0, h] = s_ref[:, h * dk:(h + 1) * dk]


def _ffn_sample_chunk(xn_ref, wg_ref, wv_ref, cwg_ref, cwv_ref, cbg_ref, cbv_ref, wdn_ref, st_ref, sto_ref):
    conv = []
    for half, (w_ref, cw_ref, cb_ref) in enumerate(((wg_ref, cwg_ref, cbg_ref), (wv_ref, cwv_ref, cbv_ref))):
        h = jnp.dot(xn_ref[...], w_ref[...], preferred_element_type=F32)
        conv.append(st_ref[half, 0] * cw_ref[0:1, :] + st_ref[half, 1] * cw_ref[1:2, :] + h * cw_ref[2:3, :]
                    + cb_ref[...])
        sto_ref[half, 0] = st_ref[half, 1]
        sto_ref[half, 1] = h
    a = (_silu(conv[0]) * conv[1]).astype(BF16)
    return jnp.dot(a, wdn_ref[...], preferred_element_type=F32)


def _sample_layer0_kernel(x_ref, pb_ref, n1_ref, pw_ref, ps_ref, n2_ref,
                          wg_ref, wv_ref, cwg_ref, cwv_ref, cbg_ref, cbv_ref, wdn_ref, st_ref,
                          xo_ref, pbo_ref, sto_ref, xn_s):
    c = pl.program_id(0)
    nbuf = pb_ref.shape[0]
    gd = pw_ref.shape[1]

    @pl.when(c == 0)
    def _():
        x = x_ref[...]
        h = _rms(x, n1_ref[...])
        parts = []
        for g, w in enumerate(POOL_WINDOWS):
            cols = slice(g * gd, (g + 1) * gd)
            s = h[:, cols]
            for j in range(1, w):
                s = s + pb_ref[nbuf - j, :, cols]
            dg = s / float(w) - h[:, cols]
            parts.append(x[:, cols] + _dot(dg, pw_ref[g]) * ps_ref[:, cols])
        x1 = jnp.concatenate(parts, axis=1)
        for j in range(nbuf - 1):
            pbo_ref[j] = pb_ref[j + 1]
        pbo_ref[nbuf - 1] = h
        xn_s[...] = _rms(x1, n2_ref[...]).astype(BF16)
        xo_ref[...] = x1

    xo_ref[...] += _ffn_sample_chunk(xn_s, wg_ref, wv_ref, cwg_ref, cwv_ref, cbg_ref, cbv_ref, wdn_ref,
                                     st_ref, sto_ref)


def _sample_dn_proj_kernel(x_ref, n1_ref, wqkv_ref, wz_ref, wab_ref, cw_ref, alog_ref, dtb_ref,
                           selg_ref, selb_ref, cst_ref,
                           w_ref, qd_ref, k_ref, vb_ref, qk_ref, eg_ref, z_ref, csto_ref):
    dk = DN_HEAD_DIM
    dn = wz_ref.shape[1]
    nh = dn // dk
    xb = _rms(x_ref[...], n1_ref[...]).astype(BF16)
    pre = jnp.dot(xb, wqkv_ref[...], preferred_element_type=F32)
    y = (cst_ref[0] * cw_ref[0:1, :] + cst_ref[1] * cw_ref[1:2, :] + cst_ref[2] * cw_ref[2:3, :]
         + pre * cw_ref[3:4, :])
    qkv = _silu(y)
    csto_ref[0] = cst_ref[1]
    csto_ref[1] = cst_ref[2]
    csto_ref[2] = pre
    z_ref[...] = jnp.dot(xb, wz_ref[...], preferred_element_type=F32)
    ab = jnp.dot(xb, wab_ref[...], preferred_element_type=F32)
    g = -jnp.exp(alog_ref[...]) * _softplus(ab + dtb_ref[...])
    eg = jnp.exp(_dot_exact_rhs(g, selg_ref[...]))
    betab = _dot_exact_rhs(1.0 / (1.0 + jnp.exp(-ab)), selb_ref[...])
    eg_ref[...] = eg
    for h in range(nh):
        hs = slice(h * dk, (h + 1) * dk)
        q = qkv[:, hs]
        k = qkv[:, dn + h * dk:dn + (h + 1) * dk]
        v = qkv[:, 2 * dn + h * dk:2 * dn + (h + 1) * dk]
        q = q * lax.rsqrt(jnp.sum(q * q, axis=-1, keepdims=True) + NORM_EPS) * (dk ** -0.5)
        k = k * lax.rsqrt(jnp.sum(k * k, axis=-1, keepdims=True) + NORM_EPS)
        kb = k * betab[:, hs]
        qk = jnp.sum(q.astype(BF16).astype(F32) * k.astype(BF16).astype(F32), axis=-1, keepdims=True)
        w_ref[:, hs] = kb * eg[:, hs]
        qd_ref[:, hs] = q * eg[:, hs]
        k_ref[:, hs] = k
        vb_ref[:, hs] = v * betab[:, hs]
        qk_ref[:, hs] = jnp.broadcast_to(qk, (q.shape[0], dk))


def _sample_dn_state_kernel(w_ref, qd_ref, k_ref, vb_ref, qk_ref, eg_ref, s_ref, o_ref, so_ref):
    dk = DN_HEAD_DIM
    nb, nh = s_ref.shape[0], s_ref.shape[1]
    eye = (lax.broadcasted_iota(jnp.int32, (dk, dk), 0) == lax.broadcasted_iota(jnp.int32, (dk, dk), 1))
    for b in range(nb):
        for h in range(nh):
            hs = slice(h * dk, (h + 1) * dk)
            s = s_ref[b, h]
            lhs = jnp.concatenate([jnp.broadcast_to(w_ref[b:b + 1, hs], (8, dk)),
                                   jnp.broadcast_to(qd_ref[b:b + 1, hs], (8, dk))], axis=0)
            r = _dot(lhs, s)
            u = vb_ref[b:b + 1, hs] - r[0:1]
            ub = u.astype(BF16).astype(F32)
            o_ref[b:b + 1, hs] = r[8:9] + qk_ref[b:b + 1, hs] * ub
            kdiag = jnp.where(eye, jnp.broadcast_to(k_ref[b:b + 1, hs], (dk, dk)), 0.0)
            so_ref[b, h] = s * eg_ref[b:b + 1, hs] + _dot(kdiag, jnp.broadcast_to(u, (dk, dk)))


def _sample_tail_kernel(x_ref, o_ref, z_ref, onw_ref, wout_ref, n2_ref,
                        wg_ref, wv_ref, cwg_ref, cwv_ref, cbg_ref, cbv_ref, wdn_ref, st_ref, fn_ref,
                        yo_ref, sto_ref, xn_s, acc_s):
    c = pl.program_id(0)
    dk = DN_HEAD_DIM

    @pl.when(c == 0)
    def _():
        outs = []
        for h in range(o_ref.shape[1] // dk):
            outs.append(_rms(o_ref[:, h * dk:(h + 1) * dk], onw_ref[...]))
        o = jnp.concatenate(outs, axis=1) * _silu(z_ref[...])
        x = x_ref[...] + _dot(o, wout_ref[...])
        acc_s[...] = x
        xn_s[...] = _rms(x, n2_ref[...]).astype(BF16)

    acc_s[...] += _ffn_sample_chunk(xn_s, wg_ref, wv_ref, cwg_ref, cwv_ref, cbg_ref, cbv_ref, wdn_ref,
                                    st_ref, sto_ref)

    @pl.when(c == pl.num_programs(0) - 1)
    def _():
        yo_ref[...] = _rms(acc_s[...], fn_ref[...])


def _params(*sem):
    return pltpu.CompilerParams(dimension_semantics=sem, vmem_limit_bytes=VMEM_LIMIT)


def kernel(x_prompt, x_sample, state_pool_buf, state_dn_conv, state_dn_ssm, state_ffn_conv, norm1_w, norm2_w,
           final_norm_w, pool_w, pool_scale, dn_w_in, dn_conv_w, dn_a_log, dn_dt_bias, dn_o_norm_w, dn_w_out,
           ffn_w_up, ffn_conv_w, ffn_conv_b, ffn_w_down):
    bp, seq, d = x_prompt.shape
    bs = x_sample.shape[0]
    nbuf = state_pool_buf.shape[2]
    dff = ffn_w_down.shape[1]
    nh = dn_a_log.shape[1]
    dk = DN_HEAD_DIM
    dn = nh * dk
    gd = pool_w.shape[2]
    tm = PROMPT_TILE
    nt = seq // tm
    nfc = dff // FFN_CHUNK
    assert seq % tm == 0 and dff % FFN_CHUNK == 0 and (3 * dn) % QKV_CHUNK == 0 and dk == 128
    assert nbuf + 1 == max(POOL_WINDOWS) and 2 * nh <= 128 and bs % SAMPLE_BLOCK == 0

    row = lambda v: v.reshape(1, -1)
    wup = [ffn_w_up[layer].astype(BF16) for layer in range(ffn_w_up.shape[0])]
    wdn = [ffn_w_down[layer].astype(BF16) for layer in range(ffn_w_down.shape[0])]
    pw = pool_w.astype(BF16)
    w_in = dn_w_in[0]
    wqkv = w_in[:, :3 * dn].astype(BF16)
    wz = w_in[:, 3 * dn:4 * dn].astype(BF16)
    wab = jnp.pad(w_in[:, 4 * dn:], ((0, 0), (0, 128 - 2 * nh))).astype(BF16)
    wout = dn_w_out[0].astype(BF16)
    alog = jnp.pad(dn_a_log[0], (0, 128 - nh)).reshape(1, 128)
    dtb = jnp.pad(dn_dt_bias[0], (0, 128 - nh)).reshape(1, 128)
    onw = row(dn_o_norm_w[0])
    lane_head = jnp.arange(dn) // dk
    selg = (jnp.arange(128)[:, None] == lane_head[None, :]).astype(BF16)
    selb = (jnp.arange(128)[:, None] == (lane_head[None, :] + nh)).astype(BF16)
    ti = jnp.arange(tm)
    ltri = ((ti[:, None] // DN_CHUNK == ti[None, :] // DN_CHUNK) & (ti[:, None] >= ti[None, :])).astype(BF16)

    x_spec = pl.BlockSpec((1, tm, d), lambda b, t: (b, t, 0))
    ffn_w_specs = [_resident((d, 2 * dff)), _resident((3, 2 * dff)), _resident((1, 2 * dff)), _resident((dff, d))]
    fst_spec = pl.BlockSpec((1, CONV_HALO, 2 * dff), lambda b, t: (b, 0, 0))
    fst_shape = jax.ShapeDtypeStruct((bp, CONV_HALO, 2 * dff), F32)

    x2, pool_tail, fst0 = pl.pallas_call(
        _prompt_layer0_kernel,
        grid=(bp, nt),
        in_specs=[x_spec, _resident((1, d)), _resident(pw.shape[1:]), _resident((1, d)), _resident((1, d))]
        + ffn_w_specs,
        out_specs=[x_spec, pl.BlockSpec((1, POOL_HALO, d), lambda b, t: (b, 0, 0)), fst_spec],
        out_shape=[jax.ShapeDtypeStruct((bp, seq, d), F32), jax.ShapeDtypeStruct((bp, POOL_HALO, d), F32), fst_shape],
        scratch_shapes=[pltpu.VMEM((POOL_HALO + tm, d), F32), pltpu.VMEM((CONV_HALO, 2 * dff), F32)],
        compiler_params=_params("arbitrary", "arbitrary"),
        name="prompt_layer0",
    )(x_prompt, row(norm1_w[0]), pw[0], row(pool_scale[0]), row(norm2_w[0]),
      wup[0], ffn_conv_w[0], row(ffn_conv_b[0]), wdn[0])

    x3, cst, ssm = pl.pallas_call(
        _prompt_deltanet_kernel,
        grid=(bp, nt),
        in_specs=[x_spec, _resident((1, d)), _resident((d, 3 * dn)), _resident((d, dn)), _resident((d, 128)),
                  _resident((4, 3 * dn)), _resident((1, 128)), _resident((1, 128)), _resident((1, dk)),
                  _resident((dn, d)), _resident((tm, tm))],
        out_specs=[x_spec, pl.BlockSpec((1, CONV_HALO, 3 * dn), lambda b, t: (b, 0, 0)),
                   pl.BlockSpec((1, nh, dk, dk), lambda b, t: (b, 0, 0, 0))],
        out_shape=[jax.ShapeDtypeStruct((bp, seq, d), F32), jax.ShapeDtypeStruct((bp, CONV_HALO, 3 * dn), F32),
                   jax.ShapeDtypeStruct((bp, nh, dk, dk), F32)],
        scratch_shapes=[pltpu.VMEM((CONV_HALO, 3 * dn), F32), pltpu.VMEM((dk, dn), F32),
                        pltpu.VMEM((tm, 3 * dn), F32)]
        + [pltpu.VMEM((tm, dn), BF16)] * 5 + [pltpu.VMEM((tm, dn), F32)] * 4 + [pltpu.VMEM((tm, dn), BF16)]
        + [pltpu.VMEM((tm // DN_CHUNK, nh // HEAD_GROUP, DN_CHUNK, HEAD_GROUP * DN_CHUNK), BF16),
           pltpu.VMEM((tm, dn), F32)],
        compiler_params=_params("arbitrary", "arbitrary"),
        name="prompt_deltanet",
    )(x2, row(norm1_w[1]), wqkv, wz, wab, dn_conv_w[0], alog, dtb, onw, wout, ltri)

    y_prompt, fst1 = pl.pallas_call(
        _prompt_ffn_final_kernel,
        grid=(bp, nt),
        in_specs=[x_spec, _resident((1, d))] + ffn_w_specs + [_resident((1, d))],
        out_specs=[x_spec, fst_spec],
        out_shape=[jax.ShapeDtypeStruct((bp, seq, d), F32), fst_shape],
        scratch_shapes=[pltpu.VMEM((CONV_HALO, 2 * dff), F32)],
        compiler_params=_params("arbitrary", "arbitrary"),
        name="prompt_ffn_final",
    )(x3, row(norm2_w[1]), wup[1], ffn_conv_w[1], row(ffn_conv_b[1]), wdn[1], row(final_norm_w))

    xs = x_sample[:, 0, :]
    pb = jnp.transpose(state_pool_buf[0], (1, 0, 2))
    cs = jnp.transpose(state_dn_conv[0], (1, 0, 2))
    fs = jnp.transpose(state_ffn_conv.reshape(2, bs, 2, 2, dff), (0, 3, 2, 1, 4))

    chunk_w = lambda half: pl.BlockSpec((d, FFN_CHUNK), lambda c, half=half: (0, half * nfc + c))
    chunk_cw = lambda half: pl.BlockSpec((3, FFN_CHUNK), lambda c, half=half: (0, half * nfc + c))
    chunk_cb = lambda half: pl.BlockSpec((1, FFN_CHUNK), lambda c, half=half: (0, half * nfc + c))
    ffn_chunk_specs = [chunk_w(0), chunk_w(1), chunk_cw(0), chunk_cw(1), chunk_cb(0), chunk_cb(1),
                       pl.BlockSpec((FFN_CHUNK, d), lambda c: (c, 0)),
                       pl.BlockSpec((2, 2, bs, FFN_CHUNK), lambda c: (0, 0, 0, c))]
    sto_spec = pl.BlockSpec((2, 2, bs, FFN_CHUNK), lambda c: (0, 0, 0, c))
    sto_shape = jax.ShapeDtypeStruct((2, 2, bs, dff), F32)

    def ffn_chunk_args(layer):
        cb = row(ffn_conv_b[layer])
        return (wup[layer], wup[layer], ffn_conv_w[layer], ffn_conv_w[layer], cb, cb, wdn[layer], fs[layer])

    xs2, pbo, sto0 = pl.pallas_call(
        _sample_layer0_kernel,
        grid=(nfc,),
        in_specs=[_full((bs, d)), _full((nbuf, bs, d)), _full((1, d)), _full(pw.shape[1:]), _full((1, d)),
                  _full((1, d))] + ffn_chunk_specs,
        out_specs=[_full((bs, d)), _full((nbuf, bs, d)), sto_spec],
        out_shape=[jax.ShapeDtypeStruct((bs, d), F32), jax.ShapeDtypeStruct((nbuf, bs, d), F32), sto_shape],
        scratch_shapes=[pltpu.VMEM((bs, d), BF16)],
        compiler_params=_params("arbitrary"),
        name="sample_layer0",
    )(xs, pb, row(norm1_w[0]), pw[0], row(pool_scale[0]), row(norm2_w[0]), *ffn_chunk_args(0))

    vec = jax.ShapeDtypeStruct((bs, dn), F32)
    w_s, qd_s, k_s, vb_s, qk_s, eg_s, z_s, cso = pl.pallas_call(
        _sample_dn_proj_kernel,
        out_shape=[vec] * 7 + [jax.ShapeDtypeStruct((3, bs, 3 * dn), F32)],
        compiler_params=pltpu.CompilerParams(vmem_limit_bytes=VMEM_LIMIT),
        name="sample_dn_proj",
    )(xs2, row(norm1_w[1]), wqkv, wz, wab, dn_conv_w[0], alog, dtb, selg, selb, cs)

    nb = SAMPLE_BLOCK
    vspec = pl.BlockSpec((nb, dn), lambda i: (i, 0))
    sspec = pl.BlockSpec((nb, nh, dk, dk), lambda i: (i, 0, 0, 0))
    o_s, ssm_s = pl.pallas_call(
        _sample_dn_state_kernel,
        grid=(bs // nb,),
        in_specs=[vspec] * 6 + [sspec],
        out_specs=[vspec, sspec],
        out_shape=[vec, jax.ShapeDtypeStruct((bs, nh, dk, dk), F32)],
        compiler_params=_params("arbitrary"),
        name="sample_dn_state",
    )(w_s, qd_s, k_s, vb_s, qk_s, eg_s, state_dn_ssm[0])

    ys, sto1 = pl.pallas_call(
        _sample_tail_kernel,
        grid=(nfc,),
        in_specs=[_full((bs, d)), _full((bs, dn)), _full((bs, dn)), _full((1, dk)), _full((dn, d)), _full((1, d))]
        + ffn_chunk_specs + [_full((1, d))],
        out_specs=[_full((bs, d)), sto_spec],
        out_shape=[jax.ShapeDtypeStruct((bs, d), F32), sto_shape],
        scratch_shapes=[pltpu.VMEM((bs, d), BF16), pltpu.VMEM((bs, d), F32)],
        compiler_params=_params("arbitrary"),
        name="sample_tail",
    )(xs2, o_s, z_s, onw, wout, row(norm2_w[1]), *ffn_chunk_args(1), row(final_norm_w))

    ffn_state = lambda sto: jnp.transpose(sto, (2, 1, 0, 3)).reshape(bs, 2, 2 * dff)
    return (
        y_prompt,
        ys[:, None, :],
        pool_tail[None, :, POOL_HALO - nbuf:, :],
        jnp.transpose(pbo, (1, 0, 2))[None],
        cst[None, :, CONV_HALO - 3:, :],
        jnp.transpose(cso, (1, 0, 2))[None],
        ssm[None],
        ssm_s[None],
        jnp.stack([fst0[:, CONV_HALO - 2:, :], fst1[:, CONV_HALO - 2:, :]]),
        jnp.stack([ffn_state(sto0), ffn_state(sto1)]),
    )
```

```python
import functools

import jax
import jax.numpy as jnp
from jax import lax
from jax.experimental import pallas as pl
from jax.experimental.pallas import tpu as pltpu

F32 = jnp.float32
BF16 = jnp.bfloat16

NORM_EPS = 1e-6
POOL_WINDOWS = (2, 4, 8, 16)
POOL_HALO = 16
CONV_HALO = 8
DN_HEAD_DIM = 128
DN_CHUNK = 64
FFN_CHUNK = 256
FFN_DOWN_GROUP = 2
FFN_LOOKAHEAD = 3
QKV_CHUNK = 512
PROMPT_TILE = 256
HEAD_GROUP = 4
SAMPLE_BLOCK = 8
VMEM_LIMIT = 56 * 1024 * 1024


def _dot(a, b):
    return jnp.dot(a.astype(BF16), b.astype(BF16), preferred_element_type=F32)


def _dot_nt(a, b):
    return lax.dot_general(a.astype(BF16), b.astype(BF16), (((1,), (1,)), ((), ())),
                           preferred_element_type=F32)


def _split3(x):
    hi = x.astype(BF16)
    r = x - hi.astype(F32)
    mid = r.astype(BF16)
    lo = (r - mid.astype(F32)).astype(BF16)
    return hi, mid, lo


def _dot_exact_rhs(x, e):
    hi, mid, lo = _split3(x)
    return (jnp.dot(lo, e, preferred_element_type=F32) + jnp.dot(mid, e, preferred_element_type=F32)
            + jnp.dot(hi, e, preferred_element_type=F32))


def _dot_exact_lhs(e, x):
    hi, mid, lo = _split3(x)
    return (jnp.dot(e, lo, preferred_element_type=F32) + jnp.dot(e, mid, preferred_element_type=F32)
            + jnp.dot(e, hi, preferred_element_type=F32))


def _dot3(a, b):
    a_hi = a.astype(BF16)
    a_lo = (a - a_hi.astype(F32)).astype(BF16)
    b_hi = b.astype(BF16)
    b_lo = (b - b_hi.astype(F32)).astype(BF16)
    return (jnp.dot(a_lo, b_hi, preferred_element_type=F32) + jnp.dot(a_hi, b_lo, preferred_element_type=F32)
            + jnp.dot(a_hi, b_hi, preferred_element_type=F32))


def _rms(x, w):
    return x * lax.rsqrt(jnp.mean(x * x, axis=-1, keepdims=True) + NORM_EPS) * w


def _silu(x):
    return x / (1.0 + jnp.exp(-x))


def _softplus(x):
    return jnp.maximum(x, 0.0) + jnp.log1p(jnp.exp(-jnp.abs(x)))


def _full(shape):
    return pl.BlockSpec(shape, lambda *_: (0,) * len(shape))


def _resident(shape):
    return pl.BlockSpec(shape, lambda *_: (0,) * len(shape), pipeline_mode=pl.Buffered(1))


def _ffn_tile(xn_b, wup_ref, cw_ref, cb_ref, wdn_ref, carry_ref):
    tm = xn_b.shape[0]
    dff = wdn_ref.shape[0]
    nchunk = dff // FFN_CHUNK

    def up(c):
        return [jnp.dot(xn_b, wup_ref[:, half * dff + c * FFN_CHUNK:half * dff + (c + 1) * FFN_CHUNK],
                        preferred_element_type=F32) for half in range(2)]

    def conv_act(c, hs):
        conv = []
        for half, h in enumerate(hs):
            cols = slice(half * dff + c * FFN_CHUNK, half * dff + (c + 1) * FFN_CHUNK)
            ext = jnp.concatenate([carry_ref[:, cols], h], axis=0)
            y = (ext * cw_ref[2:3, cols] + pltpu.roll(ext, 1, 0) * cw_ref[1:2, cols]
                 + pltpu.roll(ext, 2, 0) * cw_ref[0:1, cols])
            conv.append(y[CONV_HALO:] + cb_ref[:, cols])
            carry_ref[:, cols] = h[tm - CONV_HALO:]
        return (_silu(conv[0]) * conv[1]).astype(BF16)

    acc = None
    acts = []
    ahead = [up(c) for c in range(min(FFN_LOOKAHEAD, nchunk))]
    for c in range(nchunk):
        cur = ahead.pop(0)
        if c + FFN_LOOKAHEAD < nchunk:
            ahead.append(up(c + FFN_LOOKAHEAD))
        acts.append(conv_act(c, cur))
        if len(acts) == FFN_DOWN_GROUP or c + 1 == nchunk:
            c0 = c + 1 - len(acts)
            a = acts[0] if len(acts) == 1 else jnp.concatenate(acts, axis=1)
            d = jnp.dot(a, wdn_ref[c0 * FFN_CHUNK:(c + 1) * FFN_CHUNK, :], preferred_element_type=F32)
            acc = d if acc is None else acc + d
            acts = []
    return acc


def _prompt_layer0_kernel(x_ref, n1_ref, pw_ref, ps_ref, n2_ref, wup_ref, cw_ref, cb_ref, wdn_ref,
                          xo_ref, pool_ref, fst_ref, hbuf, carry):
    t = pl.program_id(1)
    tm = x_ref.shape[1]
    gd = pw_ref.shape[1]

    @pl.when(t == 0)
    def _():
        hbuf[0:POOL_HALO, :] = jnp.zeros((POOL_HALO, hbuf.shape[1]), F32)
        carry[...] = jnp.zeros(carry.shape, F32)

    x = x_ref[0]
    h = _rms(x, n1_ref[...])
    hbuf[POOL_HALO:POOL_HALO + tm, :] = h
    pos1 = t * tm + lax.broadcasted_iota(jnp.int32, (tm, 1), 0) + 1
    parts = []
    for g, w in enumerate(POOL_WINDOWS):
        cols = slice(g * gd, (g + 1) * gd)
        s = hbuf[:, cols]
        sh = 1
        while sh < w:
            s = s + pltpu.roll(s, sh, 0)
            sh *= 2
        cnt = jnp.minimum(w, pos1).astype(F32)
        dg = s[POOL_HALO:] / cnt - h[:, cols]
        parts.append(x[:, cols] + _dot(dg, pw_ref[g]) * ps_ref[:, cols])
    x1 = jnp.concatenate(parts, axis=1)
    tail = hbuf[tm:tm + POOL_HALO, :]
    pool_ref[0] = tail
    hbuf[0:POOL_HALO, :] = tail

    xn = _rms(x1, n2_ref[...]).astype(BF16)
    xo_ref[0] = x1 + _ffn_tile(xn, wup_ref, cw_ref, cb_ref, wdn_ref, carry)
    fst_ref[0] = carry[...]


def _prompt_ffn_final_kernel(x_ref, n2_ref, wup_ref, cw_ref, cb_ref, wdn_ref, fn_ref,
                             yo_ref, fst_ref, carry):
    @pl.when(pl.program_id(1) == 0)
    def _():
        carry[...] = jnp.zeros(carry.shape, F32)

    x = x_ref[0]
    xn = _rms(x, n2_ref[...]).astype(BF16)
    x = x + _ffn_tile(xn, wup_ref, cw_ref, cb_ref, wdn_ref, carry)
    yo_ref[0] = _rms(x, fn_ref[...])
    fst_ref[0] = carry[...]


def _interleave(streams):
    merged = []
    for si, steps in enumerate(streams):
        total = float(sum(w for w, _ in steps))
        done = 0.0
        for w, thunk in steps:
            merged.append(((done + 0.5 * w) / total, si, len(merged), thunk))
            done += w
    for _, _, _, thunk in sorted(merged, key=lambda m: m[:3]):
        thunk()


def _prompt_deltanet_kernel(xa_ref, xb_ref, n1_ref, win_ref, wab_ref, cw_ref, alog_ref, dtb_ref, onw_ref,
                            wout_ref, ltri_ref,
                            xo_ref, cst_ref, ssm_ref,
                            carry, s_ref, qkv_s, kb_s, q_s, k_s, qd_s, kd_s, vb_s, wr_s, gcol_s, z_s, uin_s, win_s,
                            qk_s, o_s, *, tiles_per_seq):
    i = pl.program_id(0)
    tm = xa_ref.shape[1]
    dk = DN_HEAD_DIM
    dn = s_ref.shape[1]
    nh = dn // dk
    ch = DN_CHUNK
    nch = tm // ch
    gh = HEAD_GROUP
    gw = gh * dk
    pw = gh * ch
    wr_slot = i % 2
    rd_slot = 1 - wr_slot

    @pl.when(i == 0)
    def _():
        for ref in (kb_s, q_s, k_s, qd_s, kd_s, vb_s, wr_s, gcol_s, z_s):
            ref[1] = jnp.zeros(ref.shape[1:], ref.dtype)
        s_ref[...] = jnp.zeros(s_ref.shape, F32)

    @pl.when(i % tiles_per_seq == 0)
    def _():
        carry[...] = jnp.zeros(carry.shape, F32)

    @pl.when((i + tiles_per_seq - 1) % tiles_per_seq == 0)
    def _():
        s_ref[...] = jnp.zeros(s_ref.shape, F32)

    front, back = [], []
    env = {}

    def f_norm():
        env["xb"] = _rms(xa_ref[0], n1_ref[...]).astype(BF16)

    def f_qkv(c):
        cols = slice(c * QKV_CHUNK, (c + 1) * QKV_CHUNK)
        pre = jnp.dot(env["xb"], win_ref[:, cols], preferred_element_type=F32)
        ext = jnp.concatenate([carry[:, cols], pre], axis=0)
        prev = pltpu.roll(ext, 1, 0)
        y = (ext * cw_ref[3:4, cols] + prev * cw_ref[2:3, cols]
             + pltpu.roll(ext * cw_ref[1:2, cols] + prev * cw_ref[0:1, cols], 2, 0))
        qkv_s[:, cols] = _silu(y[CONV_HALO:])
        carry[:, cols] = pre[tm - CONV_HALO:]
        cst_ref[0, :, cols] = pre[tm - CONV_HALO:]

    def f_gates():
        ab = jnp.dot(env["xb"], wab_ref[...], preferred_element_type=F32)
        g = -jnp.exp(alog_ref[...]) * _softplus(ab + dtb_ref[...])
        env["gc"] = _dot_exact_lhs(ltri_ref[...], g)
        env["sig"] = 1.0 / (1.0 + jnp.exp(-ab))

    def f_z():
        z_s[wr_slot] = _silu(jnp.dot(env["xb"], win_ref[:, 3 * dn:4 * dn], preferred_element_type=F32))

    def f_head(h):
        hs = slice(h * dk, (h + 1) * dk)
        gcol = jnp.broadcast_to(env["gc"][:, h:h + 1], (tm, dk))
        beta = jnp.broadcast_to(env["sig"][:, nh + h:nh + h + 1], (tm, dk))
        glast = jnp.concatenate(
            [jnp.broadcast_to(gcol[(c + 1) * ch - 1:(c + 1) * ch, :], (ch, dk)) for c in range(nch)], axis=0)
        egc = jnp.exp(gcol)
        qh = qkv_s[:, hs]
        kh = qkv_s[:, dn + h * dk:dn + (h + 1) * dk]
        vh = qkv_s[:, 2 * dn + h * dk:2 * dn + (h + 1) * dk]
        qh = qh * (lax.rsqrt(jnp.sum(qh * qh, axis=-1, keepdims=True) + NORM_EPS) * (dk ** -0.5))
        kh = kh * lax.rsqrt(jnp.sum(kh * kh, axis=-1, keepdims=True) + NORM_EPS)
        kb = kh * beta
        gcol_s[wr_slot, :, hs] = gcol
        q_s[wr_slot, :, hs] = qh.astype(BF16)
        k_s[wr_slot, :, hs] = kh.astype(BF16)
        kb_s[wr_slot, :, hs] = kb.astype(BF16)
        qd_s[wr_slot, :, hs] = (qh * egc).astype(BF16)
        kd_s[wr_slot, :, hs] = (kh * jnp.exp(glast - gcol)).astype(BF16)
        vb_s[wr_slot, :, hs] = vh * beta
        wr_s[wr_slot, :, hs] = kb * egc

    front.append((2, f_norm))
    per_part = dn // QKV_CHUNK
    for c in [part * per_part + j for j in range(per_part) for part in range(3)]:
        front.append((10, functools.partial(f_qkv, c)))
    front.append((3, f_gates))
    for h in range(nh):
        front.append((3, functools.partial(f_head, h)))
    front.append((5, f_z))

    def iota2(shape, d):
        return lax.broadcasted_iota(jnp.int32, shape, d)

    pi, pj = iota2((ch, pw), 0), iota2((ch, pw), 1) % ch
    eye_p = pi == pj
    incl_p = pi >= pj
    strict_p = pi > pj
    eye_f = eye_p.astype(F32)
    head_of_lane = iota2((ch, pw), 1) // ch
    bd_p = (iota2((pw, pw), 0) // ch) == (iota2((pw, pw), 1) // ch)
    bd_k = (iota2((2 * ch, 2 * dk), 0) // ch) == (iota2((2 * ch, 2 * dk), 1) // dk)
    bd_u = (iota2((pw, gw), 0) // ch) == (iota2((pw, gw), 1) // dk)
    bd_s = (iota2((2 * dk, 2 * dk), 0) // dk) == (iota2((2 * dk, 2 * dk), 1) // dk)

    def expand(mc):
        return jnp.where(bd_p, jnp.concatenate([mc] * gh, axis=0), 0.0).astype(BF16)

    units = [(c, gi) for c in range(nch) for gi in range(nh // gh)]
    n_cs, m_cs, p_cs = [], [None] * len(units), [None] * len(units)

    def b_scores(c, gi):
        rows = slice(c * ch, (c + 1) * ch)
        h0 = gi * gh
        sc = []
        for p in range(gh // 2):
            ps = slice((h0 + 2 * p) * dk, (h0 + 2 * p + 2) * dk)
            kc = k_s[rd_slot, rows, ps]
            kbd = jnp.where(bd_k, jnp.concatenate([kc, kc], axis=0), jnp.zeros((), BF16))
            sc.append(lax.dot_general(jnp.concatenate([kb_s[rd_slot, rows, ps], q_s[rd_slot, rows, ps]], axis=0),
                                      kbd, (((1,), (1,)), ((), ())), preferred_element_type=F32))
        sc = jnp.concatenate(sc, axis=1)
        gcol_p = jnp.broadcast_to(gcol_s[rd_slot, rows, h0 * dk:h0 * dk + 1], (ch, pw))
        for hl in range(1, gh):
            gcol_p = jnp.where(
                head_of_lane == hl,
                jnp.broadcast_to(gcol_s[rd_slot, rows, (h0 + hl) * dk:(h0 + hl) * dk + 1], (ch, pw)), gcol_p)
        grow_p = jnp.sum(jnp.where(eye_p, gcol_p, 0.0), axis=0, keepdims=True)
        decay = jnp.exp(jnp.where(incl_p, gcol_p - grow_p, -jnp.inf))
        n_cs.append(jnp.where(strict_p, sc[0:ch] * decay, 0.0))
        qk_s[c, gi] = (sc[ch:2 * ch] * decay).astype(BF16)

    def b_square(u):
        m_cs[u] = jnp.dot(n_cs[u].astype(BF16), expand(n_cs[u]), preferred_element_type=F32)
        p_cs[u] = eye_f - n_cs[u]

    def b_level(u, last):
        m_bd = expand(m_cs[u])
        if last:
            p_cs[u] = p_cs[u] + jnp.dot(p_cs[u].astype(BF16), m_bd, preferred_element_type=F32)
        else:
            both = jnp.dot(jnp.concatenate([m_cs[u], p_cs[u]], axis=0).astype(BF16), m_bd,
                           preferred_element_type=F32)
            m_cs[u] = both[0:ch]
            p_cs[u] = p_cs[u] + both[ch:2 * ch]

    def b_apply(u):
        c, gi = units[u]
        rows = slice(c * ch, (c + 1) * ch)
        h0 = gi * gh
        rhs = jnp.concatenate(
            [jnp.concatenate([vb_s[rd_slot, rows, (h0 + hl) * dk:(h0 + hl + 1) * dk],
                              wr_s[rd_slot, rows, (h0 + hl) * dk:(h0 + hl + 1) * dk]], axis=1)
             for hl in range(gh)], axis=0)
        uw = rhs + jnp.dot(expand(p_cs[u] - eye_f), rhs.astype(BF16), preferred_element_type=F32)
        for hl in range(gh):
            hs = slice((h0 + hl) * dk, (h0 + hl + 1) * dk)
            uin_s[rows, hs] = uw[hl * ch:(hl + 1) * ch, 0:dk]
            win_s[rows, hs] = uw[hl * ch:(hl + 1) * ch, dk:2 * dk].astype(BF16)

    def b_recur(c, gi):
        rows = slice(c * ch, (c + 1) * ch)
        last = (c + 1) * ch - 1
        gs = slice(gi * gw, (gi + 1) * gw)
        u_parts, qs_parts = [], []
        for p in range(gh // 2):
            ps = slice(gi * gw + 2 * p * dk, gi * gw + (2 * p + 2) * dk)
            s_pair = s_ref[:, ps]
            s_bd = jnp.where(bd_s, jnp.concatenate([s_pair, s_pair], axis=0), 0.0).astype(BF16)
            r = jnp.dot(jnp.concatenate([win_s[rows, ps], qd_s[rd_slot, rows, ps]], axis=0), s_bd,
                        preferred_element_type=F32)
            u_parts.append(uin_s[rows, ps] - r[0:ch])
            qs_parts.append(r[ch:2 * ch])
        u_g = jnp.concatenate(u_parts, axis=1)
        u_bd = jnp.where(bd_u, jnp.concatenate([u_g] * gh, axis=0), 0.0).astype(BF16)
        o_s[rows, gs] = jnp.concatenate(qs_parts, axis=1) + jnp.dot(qk_s[c, gi], u_bd,
                                                                      preferred_element_type=F32)
        kd_stack = jnp.concatenate(
            [kd_s[rd_slot, rows, gi * gw + hl * dk:gi * gw + (hl + 1) * dk] for hl in range(gh)], axis=0)
        s_ref[:, gs] = (s_ref[:, gs] * jnp.exp(gcol_s[rd_slot, last:last + 1, gs])
                        + lax.dot_general(kd_stack, u_bd, (((0,), (0,)), ((), ())),
                                          preferred_element_type=F32))

    def b_out():
        outs = [_rms(o_s[:, h * dk:(h + 1) * dk], onw_ref[...]) for h in range(nh)]
        o = jnp.concatenate(outs, axis=1) * z_s[rd_slot]
        xo_ref[0] = xb_ref[0] + _dot(o, wout_ref[...])
        for h in range(nh):
            ssm_ref[0, h] = s_ref[:, h * dk:(h + 1) * dk]

    for c, gi in units:
        back.append((2, functools.partial(b_scores, c, gi)))
    for u in range(len(units)):
        back.append((1, functools.partial(b_square, u)))
    sq = 2
    while sq < ch:
        for u in range(len(units)):
            back.append((1, functools.partial(b_level, u, 2 * sq >= ch)))
        sq *= 2
    for u in range(len(units)):
        back.append((1.5, functools.partial(b_apply, u)))
    for c in range(nch):
        for gi in range(nh // gh):
            back.append((3, functools.partial(b_recur, c, gi)))
    back.append((8, b_out))

    _interleave([front, back])


def _ffn_sample_chunk(xn_ref, wg_ref, wv_ref, cwg_ref, cwv_ref, cbg_ref, cbv_ref, wdn_ref, stg_ref, stv_ref,
                      stog_ref, stov_ref):
    conv = []
    for w_ref, cw_ref, cb_ref, st_ref, sto_ref in ((wg_ref, cwg_ref, cbg_ref, stg_ref, stog_ref),
                                                   (wv_ref, cwv_ref, cbv_ref, stv_ref, stov_ref)):
        h = jnp.dot(xn_ref[...], w_ref[...], preferred_element_type=F32)
        conv.append(st_ref[0] * cw_ref[0:1, :] + st_ref[1] * cw_ref[1:2, :] + h * cw_ref[2:3, :] + cb_ref[...])
        sto_ref[0] = st_ref[1]
        sto_ref[1] = h
    a = (_silu(conv[0]) * conv[1]).astype(BF16)
    return jnp.dot(a, wdn_ref[...], preferred_element_type=F32)


def _sample_layer0_kernel(x_ref, pb_ref, n1_ref, pw_ref, ps_ref, n2_ref,
                          wg_ref, wv_ref, cwg_ref, cwv_ref, cbg_ref, cbv_ref, wdn_ref, stg_ref, stv_ref,
                          xo_ref, pbo_ref, stog_ref, stov_ref, xn_s):
    c = pl.program_id(0)
    nbuf = pb_ref.shape[0]
    gd = pw_ref.shape[1]

    @pl.when(c == 0)
    def _():
        x = x_ref[...]
        h = _rms(x, n1_ref[...])
        parts = []
        for g, w in enumerate(POOL_WINDOWS):
            cols = slice(g * gd, (g + 1) * gd)
            s = h[:, cols]
            for j in range(1, w):
                s = s + pb_ref[nbuf - j, :, cols]
            dg = s / float(w) - h[:, cols]
            parts.append(x[:, cols] + _dot(dg, pw_ref[g]) * ps_ref[:, cols])
        x1 = jnp.concatenate(parts, axis=1)
        for j in range(nbuf - 1):
            pbo_ref[j] = pb_ref[j + 1]
        pbo_ref[nbuf - 1] = h
        xn_s[...] = _rms(x1, n2_ref[...]).astype(BF16)
        xo_ref[...] = x1

    xo_ref[...] += _ffn_sample_chunk(xn_s, wg_ref, wv_ref, cwg_ref, cwv_ref, cbg_ref, cbv_ref, wdn_ref,
                                     stg_ref, stv_ref, stog_ref, stov_ref)


def _sample_dn_proj_kernel(x_ref, n1_ref, win_ref, wab_ref, cw_ref, alog_ref, dtb_ref,
                           selg_ref, selb_ref, cst_ref,
                           w_ref, qd_ref, k_ref, vb_ref, qk_ref, eg_ref, z_ref, csto_ref):
    dk = DN_HEAD_DIM
    dn = win_ref.shape[1] // 4
    nh = dn // dk
    xb = _rms(x_ref[...], n1_ref[...]).astype(BF16)
    pre = jnp.dot(xb, win_ref[:, 0:3 * dn], preferred_element_type=F32)
    y = (cst_ref[0] * cw_ref[0:1, :] + cst_ref[1] * cw_ref[1:2, :] + cst_ref[2] * cw_ref[2:3, :]
         + pre * cw_ref[3:4, :])
    qkv = _silu(y)
    csto_ref[0] = cst_ref[1]
    csto_ref[1] = cst_ref[2]
    csto_ref[2] = pre
    z_ref[...] = jnp.dot(xb, win_ref[:, 3 * dn:4 * dn], preferred_element_type=F32)
    ab = jnp.dot(xb, wab_ref[...], preferred_element_type=F32)
    g = -jnp.exp(alog_ref[...]) * _softplus(ab + dtb_ref[...])
    eg = jnp.exp(_dot_exact_rhs(g, selg_ref[...]))
    betab = _dot_exact_rhs(1.0 / (1.0 + jnp.exp(-ab)), selb_ref[...])
    eg_ref[...] = eg
    for h in range(nh):
        hs = slice(h * dk, (h + 1) * dk)
        q = qkv[:, hs]
        k = qkv[:, dn + h * dk:dn + (h + 1) * dk]
        v = qkv[:, 2 * dn + h * dk:2 * dn + (h + 1) * dk]
        q = q * lax.rsqrt(jnp.sum(q * q, axis=-1, keepdims=True) + NORM_EPS) * (dk ** -0.5)
        k = k * lax.rsqrt(jnp.sum(k * k, axis=-1, keepdims=True) + NORM_EPS)
        kb = k * betab[:, hs]
        qk = jnp.sum(q.astype(BF16).astype(F32) * k.astype(BF16).astype(F32), axis=-1, keepdims=True)
        w_ref[:, hs] = kb * eg[:, hs]
        qd_ref[:, hs] = q * eg[:, hs]
        k_ref[:, hs] = k
        vb_ref[:, hs] = v * betab[:, hs]
        qk_ref[:, hs] = jnp.broadcast_to(qk, (q.shape[0], dk))


def _sample_dn_state_kernel(w_ref, qd_ref, k_ref, vb_ref, qk_ref, eg_ref, s_ref, o_ref, so_ref):
    dk = DN_HEAD_DIM
    nb, nh = s_ref.shape[0], s_ref.shape[1]
    eye = (lax.broadcasted_iota(jnp.int32, (dk, dk), 0) == lax.broadcasted_iota(jnp.int32, (dk, dk), 1))
    for b in range(nb):
        for h in range(nh):
            hs = slice(h * dk, (h + 1) * dk)
            s = s_ref[b, h]
            lhs = jnp.concatenate([jnp.broadcast_to(w_ref[b:b + 1, hs], (8, dk)),
                                   jnp.broadcast_to(qd_ref[b:b + 1, hs], (8, dk))], axis=0)
            r = _dot(lhs, s)
            u = vb_ref[b:b + 1, hs] - r[0:1]
            ub = u.astype(BF16).astype(F32)
            o_ref[b:b + 1, hs] = r[8:9] + qk_ref[b:b + 1, hs] * ub
            kdiag = jnp.where(eye, jnp.broadcast_to(k_ref[b:b + 1, hs], (dk, dk)), 0.0)
            so_ref[b, h] = s * eg_ref[b:b + 1, hs] + _dot(kdiag, jnp.broadcast_to(u, (dk, dk)))


def _sample_tail_kernel(x_ref, o_ref, z_ref, onw_ref, wout_ref, n2_ref,
                        wg_ref, wv_ref, cwg_ref, cwv_ref, cbg_ref, cbv_ref, wdn_ref, stg_ref, stv_ref, fn_ref,
                        yo_ref, stog_ref, stov_ref, xn_s, acc_s):
    c = pl.program_id(0)
    dk = DN_HEAD_DIM

    @pl.when(c == 0)
    def _():
        outs = []
        for h in range(o_ref.shape[1] // dk):
            outs.append(_rms(o_ref[:, h * dk:(h + 1) * dk], onw_ref[...]))
        o = jnp.concatenate(outs, axis=1) * _silu(z_ref[...])
        x = x_ref[...] + _dot(o, wout_ref[...])
        acc_s[...] = x
        xn_s[...] = _rms(x, n2_ref[...]).astype(BF16)

    acc_s[...] += _ffn_sample_chunk(xn_s, wg_ref, wv_ref, cwg_ref, cwv_ref, cbg_ref, cbv_ref, wdn_ref,
                                    stg_ref, stv_ref, stog_ref, stov_ref)

    @pl.when(c == pl.num_programs(0) - 1)
    def _():
        yo_ref[...] = _rms(acc_s[...], fn_ref[...])


def _params(*sem):
    return pltpu.CompilerParams(dimension_semantics=sem, vmem_limit_bytes=VMEM_LIMIT)


def kernel(x_prompt, x_sample, state_pool_buf, state_dn_conv, state_dn_ssm, state_ffn_conv, norm1_w, norm2_w,
           final_norm_w, pool_w, pool_scale, dn_w_in, dn_conv_w, dn_a_log, dn_dt_bias, dn_o_norm_w, dn_w_out,
           ffn_w_up, ffn_conv_w, ffn_conv_b, ffn_w_down):
    bp, seq, d = x_prompt.shape
    bs = x_sample.shape[0]
    nbuf = state_pool_buf.shape[2]
    dff = ffn_w_down.shape[1]
    nh = dn_a_log.shape[1]
    dk = DN_HEAD_DIM
    dn = nh * dk
    gd = pool_w.shape[2]
    tm = PROMPT_TILE
    nt = seq // tm
    nfc = dff // FFN_CHUNK
    assert seq % tm == 0 and dff % FFN_CHUNK == 0 and (3 * dn) % QKV_CHUNK == 0 and dk == 128
    assert nbuf + 1 == max(POOL_WINDOWS) and 2 * nh <= 128 and bs % SAMPLE_BLOCK == 0

    row = lambda v: v.reshape(1, -1)
    wup = ffn_w_up.astype(BF16)
    wdn = ffn_w_down.astype(BF16)
    pw = pool_w.astype(BF16)
    w_in = dn_w_in[0]
    wmain = w_in[:, :4 * dn].astype(BF16)
    wab = jnp.pad(w_in[:, 4 * dn:], ((0, 0), (0, 128 - 2 * nh))).astype(BF16)
    wout = dn_w_out[0].astype(BF16)
    alog = jnp.pad(dn_a_log[0], (0, 128 - nh)).reshape(1, 128)
    dtb = jnp.pad(dn_dt_bias[0], (0, 128 - nh)).reshape(1, 128)
    onw = row(dn_o_norm_w[0])
    lane_head = jnp.arange(dn) // dk
    selg = (jnp.arange(128)[:, None] == lane_head[None, :]).astype(BF16)
    selb = (jnp.arange(128)[:, None] == (lane_head[None, :] + nh)).astype(BF16)
    ti = jnp.arange(tm)
    ltri = ((ti[:, None] // DN_CHUNK == ti[None, :] // DN_CHUNK) & (ti[:, None] >= ti[None, :])).astype(BF16)

    x_spec = pl.BlockSpec((1, tm, d), lambda b, t: (b, t, 0))
    layer_resident = lambda shape, layer: pl.BlockSpec((None,) + shape, lambda *_: (layer,) + (0,) * len(shape),
                                                       pipeline_mode=pl.Buffered(1))
    ffn_w_specs = lambda layer: [layer_resident((d, 2 * dff), layer), _resident((3, 2 * dff)),
                                 _resident((1, 2 * dff)), layer_resident((dff, d), layer)]
    fst_spec = pl.BlockSpec((1, CONV_HALO, 2 * dff), lambda b, t: (b, 0, 0))
    fst_shape = jax.ShapeDtypeStruct((bp, CONV_HALO, 2 * dff), F32)

    x2, pool_tail, fst0 = pl.pallas_call(
        _prompt_layer0_kernel,
        grid=(bp, nt),
        in_specs=[x_spec, _resident((1, d)), _resident(pw.shape[1:]), _resident((1, d)), _resident((1, d))]
        + ffn_w_specs(0),
        out_specs=[x_spec, pl.BlockSpec((1, POOL_HALO, d), lambda b, t: (b, 0, 0)), fst_spec],
        out_shape=[jax.ShapeDtypeStruct((bp, seq, d), F32), jax.ShapeDtypeStruct((bp, POOL_HALO, d), F32), fst_shape],
        scratch_shapes=[pltpu.VMEM((POOL_HALO + tm, d), F32), pltpu.VMEM((CONV_HALO, 2 * dff), F32)],
        compiler_params=_params("arbitrary", "arbitrary"),
        name="prompt_layer0",
    )(x_prompt, row(norm1_w[0]), pw[0], row(pool_scale[0]), row(norm2_w[0]),
      wup, ffn_conv_w[0], row(ffn_conv_b[0]), wdn)

    ntiles = bp * nt
    front_tile = lambda i: jnp.minimum(i, ntiles - 1)
    back_tile = lambda i: jnp.maximum(i - 1, 0)
    x_front = pl.BlockSpec((1, tm, d), lambda i: (front_tile(i) // nt, front_tile(i) % nt, 0))
    x_back = pl.BlockSpec((1, tm, d), lambda i: (back_tile(i) // nt, back_tile(i) % nt, 0))
    x3, cst, ssm = pl.pallas_call(
        functools.partial(_prompt_deltanet_kernel, tiles_per_seq=nt),
        grid=(ntiles + 1,),
        in_specs=[x_front, x_back,
                  _resident((1, d)), _resident((d, 4 * dn)), _resident((d, 128)),
                  _resident((4, 3 * dn)), _resident((1, 128)), _resident((1, 128)), _resident((1, dk)),
                  _resident((dn, d)), _resident((tm, tm))],
        out_specs=[x_back,
                   pl.BlockSpec((1, CONV_HALO, 3 * dn), lambda i: (front_tile(i) // nt, 0, 0)),
                   pl.BlockSpec((1, nh, dk, dk), lambda i: (back_tile(i) // nt, 0, 0, 0))],
        out_shape=[jax.ShapeDtypeStruct((bp, seq, d), F32), jax.ShapeDtypeStruct((bp, CONV_HALO, 3 * dn), F32),
                   jax.ShapeDtypeStruct((bp, nh, dk, dk), F32)],
        scratch_shapes=[pltpu.VMEM((CONV_HALO, 3 * dn), F32), pltpu.VMEM((dk, dn), F32),
                        pltpu.VMEM((tm, 3 * dn), F32)]
        + [pltpu.VMEM((2, tm, dn), BF16)] * 5 + [pltpu.VMEM((2, tm, dn), F32)] * 4
        + [pltpu.VMEM((tm, dn), F32), pltpu.VMEM((tm, dn), BF16),
           pltpu.VMEM((tm // DN_CHUNK, nh // HEAD_GROUP, DN_CHUNK, HEAD_GROUP * DN_CHUNK), BF16),
           pltpu.VMEM((tm, dn), F32)],
        compiler_params=_params("arbitrary"),
        name="prompt_deltanet",
    )(x2, x2, row(norm1_w[1]), wmain, wab, dn_conv_w[0], alog, dtb, onw, wout, ltri)

    y_prompt, fst1 = pl.pallas_call(
        _prompt_ffn_final_kernel,
        grid=(bp, nt),
        in_specs=[x_spec, _resident((1, d))] + ffn_w_specs(1) + [_resident((1, d))],
        out_specs=[x_spec, fst_spec],
        out_shape=[jax.ShapeDtypeStruct((bp, seq, d), F32), fst_shape],
        scratch_shapes=[pltpu.VMEM((CONV_HALO, 2 * dff), F32)],
        compiler_params=_params("arbitrary", "arbitrary"),
        name="prompt_ffn_final",
    )(x3, row(norm2_w[1]), wup, ffn_conv_w[1], row(ffn_conv_b[1]), wdn, row(final_norm_w))

    xs = x_sample[:, 0, :]
    pb = jnp.transpose(state_pool_buf[0], (1, 0, 2))
    cs = jnp.transpose(state_dn_conv[0], (1, 0, 2))
    fs = jnp.transpose(state_ffn_conv, (0, 2, 1, 3))

    def ffn_chunk_specs(layer):
        specs = []
        for shape in ((None, d, FFN_CHUNK), (3, FFN_CHUNK), (1, FFN_CHUNK)):
            for half in range(2):
                if shape[0] is None:
                    specs.append(pl.BlockSpec(shape, lambda c, half=half: (layer, 0, half * nfc + c)))
                else:
                    specs.append(pl.BlockSpec(shape, lambda c, half=half: (0, half * nfc + c)))
        specs.append(pl.BlockSpec((None, FFN_CHUNK, d), lambda c: (layer, c, 0)))
        for half in range(2):
            specs.append(pl.BlockSpec((None, 2, bs, FFN_CHUNK), lambda c, half=half: (layer, 0, 0, half * nfc + c)))
        return specs

    sto_specs = [pl.BlockSpec((2, bs, FFN_CHUNK), lambda c: (0, 0, c))] * 2
    sto_shapes = [jax.ShapeDtypeStruct((2, bs, dff), F32)] * 2

    def ffn_chunk_args(layer):
        cb = row(ffn_conv_b[layer])
        return (wup, wup, ffn_conv_w[layer], ffn_conv_w[layer], cb, cb, wdn, fs, fs)

    xs2, pbo, stg0, stv0 = pl.pallas_call(
        _sample_layer0_kernel,
        grid=(nfc,),
        in_specs=[_full((bs, d)), _full((nbuf, bs, d)), _full((1, d)), _full(pw.shape[1:]), _full((1, d)),
                  _full((1, d))] + ffn_chunk_specs(0),
        out_specs=[_full((bs, d)), _full((nbuf, bs, d))] + sto_specs,
        out_shape=[jax.ShapeDtypeStruct((bs, d), F32), jax.ShapeDtypeStruct((nbuf, bs, d), F32)] + sto_shapes,
        scratch_shapes=[pltpu.VMEM((bs, d), BF16)],
        compiler_params=_params("arbitrary"),
        name="sample_layer0",
    )(xs, pb, row(norm1_w[0]), pw[0], row(pool_scale[0]), row(norm2_w[0]), *ffn_chunk_args(0))

    vec = jax.ShapeDtypeStruct((bs, dn), F32)
    w_s, qd_s, k_s, vb_s, qk_s, eg_s, z_s, cso = pl.pallas_call(
        _sample_dn_proj_kernel,
        out_shape=[vec] * 7 + [jax.ShapeDtypeStruct((3, bs, 3 * dn), F32)],
        compiler_params=pltpu.CompilerParams(vmem_limit_bytes=VMEM_LIMIT),
        name="sample_dn_proj",
    )(xs2, row(norm1_w[1]), wmain, wab, dn_conv_w[0], alog, dtb, selg, selb, cs)

    nb = SAMPLE_BLOCK
    vspec = pl.BlockSpec((nb, dn), lambda i: (i, 0))
    sspec = pl.BlockSpec((nb, nh, dk, dk), lambda i: (i, 0, 0, 0))
    o_s, ssm_s = pl.pallas_call(
        _sample_dn_state_kernel,
        grid=(bs // nb,),
        in_specs=[vspec] * 6 + [sspec],
        out_specs=[vspec, sspec],
        out_shape=[vec, jax.ShapeDtypeStruct((bs, nh, dk, dk), F32)],
        compiler_params=_params("arbitrary"),
        name="sample_dn_state",
    )(w_s, qd_s, k_s, vb_s, qk_s, eg_s, state_dn_ssm[0])

    ys, stg1, stv1 = pl.pallas_call(
        _sample_tail_kernel,
        grid=(nfc,),
        in_specs=[_full((bs, d)), _full((bs, dn)), _full((bs, dn)), _full((1, dk)), _full((dn, d)), _full((1, d))]
        + ffn_chunk_specs(1) + [_full((1, d))],
        out_specs=[_full((bs, d))] + sto_specs,
        out_shape=[jax.ShapeDtypeStruct((bs, d), F32)] + sto_shapes,
        scratch_shapes=[pltpu.VMEM((bs, d), BF16), pltpu.VMEM((bs, d), F32)],
        compiler_params=_params("arbitrary"),
        name="sample_tail",
    )(xs2, o_s, z_s, onw, wout, row(norm2_w[1]), *ffn_chunk_args(1), row(final_norm_w))

    ffn_state = lambda sg, sv: jnp.transpose(jnp.concatenate([sg, sv], axis=-1), (1, 0, 2))
    return (
        y_prompt,
        ys[:, None, :],
        pool_tail[None, :, POOL_HALO - nbuf:, :],
        jnp.transpose(pbo, (1, 0, 2))[None],
        cst[None, :, CONV_HALO - 3:, :],
        jnp.transpose(cso, (1, 0, 2))[None],
        ssm[None],
        ssm_s[None],
        jnp.stack([fst0[:, CONV_HALO - 2:, :], fst1[:, CONV_HALO - 2:, :]]),
        jnp.stack([ffn_state(stg0, stv0), ffn_state(stg1, stv1)]),
    )
```

```python
import functools

import jax
import jax.numpy as jnp
from jax import lax
from jax.experimental import pallas as pl
from jax.experimental.pallas import tpu as pltpu

F32 = jnp.float32
BF16 = jnp.bfloat16

NORM_EPS = 1e-6
POOL_WINDOWS = (2, 4, 8, 16)
POOL_HALO = 16
CONV_HALO = 8
DN_HEAD_DIM = 128
DN_CHUNK = 64
FFN_CHUNK = 256
FFN_DOWN_GROUP = 2
FFN_LOOKAHEAD = 3
QKV_CHUNK = 512
PROMPT_TILE = 256
FFN_TILE = 512
HEAD_GROUP = 4
SAMPLE_FFN_CHUNK = 1408
SAMPLE_BLOCK = 8
VMEM_LIMIT = 56 * 1024 * 1024


def _dot(a, b):
    return jnp.dot(a.astype(BF16), b.astype(BF16), preferred_element_type=F32)


def _dot_nt(a, b):
    return lax.dot_general(a.astype(BF16), b.astype(BF16), (((1,), (1,)), ((), ())),
                           preferred_element_type=F32)


def _split3(x):
    hi = x.astype(BF16)
    r = x - hi.astype(F32)
    mid = r.astype(BF16)
    lo = (r - mid.astype(F32)).astype(BF16)
    return hi, mid, lo


def _dot_exact_rhs(x, e):
    hi, mid, lo = _split3(x)
    return (jnp.dot(lo, e, preferred_element_type=F32) + jnp.dot(mid, e, preferred_element_type=F32)
            + jnp.dot(hi, e, preferred_element_type=F32))


def _dot_exact_lhs(e, x):
    hi, mid, lo = _split3(x)
    return (jnp.dot(e, lo, preferred_element_type=F32) + jnp.dot(e, mid, preferred_element_type=F32)
            + jnp.dot(e, hi, preferred_element_type=F32))


def _dot3(a, b):
    a_hi = a.astype(BF16)
    a_lo = (a - a_hi.astype(F32)).astype(BF16)
    b_hi = b.astype(BF16)
    b_lo = (b - b_hi.astype(F32)).astype(BF16)
    return (jnp.dot(a_lo, b_hi, preferred_element_type=F32) + jnp.dot(a_hi, b_lo, preferred_element_type=F32)
            + jnp.dot(a_hi, b_hi, preferred_element_type=F32))


def _rms(x, w):
    return x * lax.rsqrt(jnp.mean(x * x, axis=-1, keepdims=True) + NORM_EPS) * w


def _silu(x):
    return x / (1.0 + jnp.exp(-x))


def _softplus(x):
    return jnp.maximum(x, 0.0) + jnp.log1p(jnp.exp(-jnp.abs(x)))


def _full(shape):
    return pl.BlockSpec(shape, lambda *_: (0,) * len(shape))


def _resident(shape):
    return pl.BlockSpec(shape, lambda *_: (0,) * len(shape), pipeline_mode=pl.Buffered(1))


def _ffn_tile(xn_b, wup_ref, cw_ref, cb_ref, wdn_ref, carry_ref):
    tm = xn_b.shape[0]
    dff = wdn_ref.shape[0]
    nchunk = dff // FFN_CHUNK

    def up(c):
        return [jnp.dot(xn_b, wup_ref[:, half * dff + c * FFN_CHUNK:half * dff + (c + 1) * FFN_CHUNK],
                        preferred_element_type=F32) for half in range(2)]

    def conv_act(c, hs):
        conv = []
        for half, h in enumerate(hs):
            cols = slice(half * dff + c * FFN_CHUNK, half * dff + (c + 1) * FFN_CHUNK)
            ext = jnp.concatenate([carry_ref[:, cols], h], axis=0)
            y = (ext * cw_ref[2:3, cols] + pltpu.roll(ext, 1, 0) * cw_ref[1:2, cols]
                 + pltpu.roll(ext, 2, 0) * cw_ref[0:1, cols])
            conv.append(y[CONV_HALO:] + cb_ref[:, cols])
            carry_ref[:, cols] = h[tm - CONV_HALO:]
        return (_silu(conv[0]) * conv[1]).astype(BF16)

    acc = None
    acts = []
    ahead = [up(c) for c in range(min(FFN_LOOKAHEAD, nchunk))]
    for c in range(nchunk):
        cur = ahead.pop(0)
        if c + FFN_LOOKAHEAD < nchunk:
            ahead.append(up(c + FFN_LOOKAHEAD))
        acts.append(conv_act(c, cur))
        if len(acts) == FFN_DOWN_GROUP or c + 1 == nchunk:
            c0 = c + 1 - len(acts)
            a = acts[0] if len(acts) == 1 else jnp.concatenate(acts, axis=1)
            d = jnp.dot(a, wdn_ref[c0 * FFN_CHUNK:(c + 1) * FFN_CHUNK, :], preferred_element_type=F32)
            acc = d if acc is None else acc + d
            acts = []
    return acc


def _prompt_layer0_kernel(x_ref, n1_ref, pw_ref, ps_ref, n2_ref, wup_ref, cw_ref, cb_ref, wdn_ref,
                          xo_ref, pool_ref, fst_ref, hbuf, carry):
    t = pl.program_id(1)
    tm = x_ref.shape[1]
    gd = pw_ref.shape[1]

    @pl.when(t == 0)
    def _():
        hbuf[0:POOL_HALO, :] = jnp.zeros((POOL_HALO, hbuf.shape[1]), F32)
        carry[...] = jnp.zeros(carry.shape, F32)

    x = x_ref[0]
    h = _rms(x, n1_ref[...])
    hbuf[POOL_HALO:POOL_HALO + tm, :] = h
    pos1 = t * tm + lax.broadcasted_iota(jnp.int32, (tm, 1), 0) + 1
    parts = []
    for g, w in enumerate(POOL_WINDOWS):
        cols = slice(g * gd, (g + 1) * gd)
        s = hbuf[:, cols]
        sh = 1
        while sh < w:
            s = s + pltpu.roll(s, sh, 0)
            sh *= 2
        cnt = jnp.minimum(w, pos1).astype(F32)
        dg = s[POOL_HALO:] / cnt - h[:, cols]
        parts.append(x[:, cols] + _dot(dg, pw_ref[g]) * ps_ref[:, cols])
    x1 = jnp.concatenate(parts, axis=1)
    tail = hbuf[tm:tm + POOL_HALO, :]
    pool_ref[0] = tail
    hbuf[0:POOL_HALO, :] = tail

    xn = _rms(x1, n2_ref[...]).astype(BF16)
    xo_ref[0] = x1 + _ffn_tile(xn, wup_ref, cw_ref, cb_ref, wdn_ref, carry)
    fst_ref[0] = carry[...]


def _prompt_ffn_final_kernel(x_ref, n2_ref, wup_ref, cw_ref, cb_ref, wdn_ref, fn_ref,
                             yo_ref, fst_ref, carry):
    @pl.when(pl.program_id(1) == 0)
    def _():
        carry[...] = jnp.zeros(carry.shape, F32)

    x = x_ref[0]
    xn = _rms(x, n2_ref[...]).astype(BF16)
    x = x + _ffn_tile(xn, wup_ref, cw_ref, cb_ref, wdn_ref, carry)
    yo_ref[0] = _rms(x, fn_ref[...])
    fst_ref[0] = carry[...]


def _interleave(streams):
    merged = []
    for si, steps in enumerate(streams):
        total = float(sum(w for w, _ in steps))
        done = 0.0
        for w, thunk in steps:
            merged.append(((done + 0.5 * w) / total, si, len(merged), thunk))
            done += w
    for _, _, _, thunk in sorted(merged, key=lambda m: m[:3]):
        thunk()


def _prompt_deltanet_kernel(xa_ref, xb_ref, n1_ref, win_ref, wab_ref, cw_ref, alog_ref, dtb_ref, onw_ref,
                            wout_ref, ltri_ref,
                            xo_ref, cst_ref, ssm_ref,
                            carry, s_ref, qkv_s, kb_s, q_s, k_s, qd_s, kd_s, vb_s, wr_s, gcol_s, z_s, uin_s, win_s,
                            qk_s, o_s, *, tiles_per_seq):
    i = pl.program_id(0)
    tm = xa_ref.shape[1]
    dk = DN_HEAD_DIM
    dn = s_ref.shape[1]
    nh = dn // dk
    ch = DN_CHUNK
    nch = tm // ch
    gh = HEAD_GROUP
    gw = gh * dk
    pw = gh * ch
    wr_slot = i % 2
    rd_slot = 1 - wr_slot

    @pl.when(i == 0)
    def _():
        for ref in (kb_s, q_s, k_s, qd_s, kd_s, vb_s, wr_s, gcol_s, z_s):
            ref[1] = jnp.zeros(ref.shape[1:], ref.dtype)
        s_ref[...] = jnp.zeros(s_ref.shape, F32)

    @pl.when(i % tiles_per_seq == 0)
    def _():
        carry[...] = jnp.zeros(carry.shape, F32)

    @pl.when((i + tiles_per_seq - 1) % tiles_per_seq == 0)
    def _():
        s_ref[...] = jnp.zeros(s_ref.shape, F32)

    front, back = [], []
    env = {}

    def f_norm():
        env["xb"] = _rms(xa_ref[0], n1_ref[...]).astype(BF16)

    def f_qkv(c):
        cols = slice(c * QKV_CHUNK, (c + 1) * QKV_CHUNK)
        pre = jnp.dot(env["xb"], win_ref[:, cols], preferred_element_type=F32)
        ext = jnp.concatenate([carry[:, cols], pre], axis=0)
        prev = pltpu.roll(ext, 1, 0)
        y = (ext * cw_ref[3:4, cols] + prev * cw_ref[2:3, cols]
             + pltpu.roll(ext * cw_ref[1:2, cols] + prev * cw_ref[0:1, cols], 2, 0))
        qkv_s[:, cols] = _silu(y[CONV_HALO:])
        carry[:, cols] = pre[tm - CONV_HALO:]
        cst_ref[0, :, cols] = pre[tm - CONV_HALO:]

    def f_gates():
        ab = jnp.dot(env["xb"], wab_ref[...], preferred_element_type=F32)
        g = -jnp.exp(alog_ref[...]) * _softplus(ab + dtb_ref[...])
        env["gc"] = _dot_exact_lhs(ltri_ref[...], g)
        env["sig"] = 1.0 / (1.0 + jnp.exp(-ab))

    def f_z():
        z_s[wr_slot] = _silu(jnp.dot(env["xb"], win_ref[:, 3 * dn:4 * dn], preferred_element_type=F32))

    def f_head(h):
        hs = slice(h * dk, (h + 1) * dk)
        gcol = jnp.broadcast_to(env["gc"][:, h:h + 1], (tm, dk))
        beta = jnp.broadcast_to(env["sig"][:, nh + h:nh + h + 1], (tm, dk))
        glast = jnp.concatenate(
            [jnp.broadcast_to(gcol[(c + 1) * ch - 1:(c + 1) * ch, :], (ch, dk)) for c in range(nch)], axis=0)
        egc = jnp.exp(gcol)
        qh = qkv_s[:, hs]
        kh = qkv_s[:, dn + h * dk:dn + (h + 1) * dk]
        vh = qkv_s[:, 2 * dn + h * dk:2 * dn + (h + 1) * dk]
        qh = qh * (lax.rsqrt(jnp.sum(qh * qh, axis=-1, keepdims=True) + NORM_EPS) * (dk ** -0.5))
        kh = kh * lax.rsqrt(jnp.sum(kh * kh, axis=-1, keepdims=True) + NORM_EPS)
        kb = kh * beta
        gcol_s[wr_slot, :, hs] = gcol
        q_s[wr_slot, :, hs] = qh.astype(BF16)
        k_s[wr_slot, :, hs] = kh.astype(BF16)
        kb_s[wr_slot, :, hs] = kb.astype(BF16)
        qd_s[wr_slot, :, hs] = (qh * egc).astype(BF16)
        kd_s[wr_slot, :, hs] = (kh * jnp.exp(glast - gcol)).astype(BF16)
        vb_s[wr_slot, :, hs] = vh * beta
        wr_s[wr_slot, :, hs] = kb * egc

    front.append((2, f_norm))
    per_part = dn // QKV_CHUNK
    for c in [part * per_part + j for j in range(per_part) for part in range(3)]:
        front.append((10, functools.partial(f_qkv, c)))
    front.append((3, f_gates))
    for h in range(nh):
        front.append((3, functools.partial(f_head, h)))
    front.append((5, f_z))

    def iota2(shape, d):
        return lax.broadcasted_iota(jnp.int32, shape, d)

    pi, pj = iota2((ch, pw), 0), iota2((ch, pw), 1) % ch
    eye_p = pi == pj
    incl_p = pi >= pj
    strict_p = pi > pj
    eye_f = eye_p.astype(F32)
    head_of_lane = iota2((ch, pw), 1) // ch
    bd_p = (iota2((pw, pw), 0) // ch) == (iota2((pw, pw), 1) // ch)
    bd_k = (iota2((2 * ch, 2 * dk), 0) // ch) == (iota2((2 * ch, 2 * dk), 1) // dk)
    bd_u = (iota2((pw, gw), 0) // ch) == (iota2((pw, gw), 1) // dk)
    bd_s = (iota2((2 * dk, 2 * dk), 0) // dk) == (iota2((2 * dk, 2 * dk), 1) // dk)

    def expand(mc):
        return jnp.where(bd_p, jnp.concatenate([mc] * gh, axis=0), 0.0).astype(BF16)

    units = [(c, gi) for c in range(nch) for gi in range(nh // gh)]
    n_cs, m_cs, p_cs = [], [None] * len(units), [None] * len(units)

    def b_scores(c, gi):
        rows = slice(c * ch, (c + 1) * ch)
        h0 = gi * gh
        sc = []
        for p in range(gh // 2):
            ps = slice((h0 + 2 * p) * dk, (h0 + 2 * p + 2) * dk)
            kc = k_s[rd_slot, rows, ps]
            kbd = jnp.where(bd_k, jnp.concatenate([kc, kc], axis=0), jnp.zeros((), BF16))
            sc.append(lax.dot_general(jnp.concatenate([kb_s[rd_slot, rows, ps], q_s[rd_slot, rows, ps]], axis=0),
                                      kbd, (((1,), (1,)), ((), ())), preferred_element_type=F32))
        sc = jnp.concatenate(sc, axis=1)
        gcol_p = jnp.broadcast_to(gcol_s[rd_slot, rows, h0 * dk:h0 * dk + 1], (ch, pw))
        for hl in range(1, gh):
            gcol_p = jnp.where(
                head_of_lane == hl,
                jnp.broadcast_to(gcol_s[rd_slot, rows, (h0 + hl) * dk:(h0 + hl) * dk + 1], (ch, pw)), gcol_p)
        grow_p = jnp.sum(jnp.where(eye_p, gcol_p, 0.0), axis=0, keepdims=True)
        decay = jnp.exp(jnp.where(incl_p, gcol_p - grow_p, -jnp.inf))
        n_cs.append(jnp.where(strict_p, sc[0:ch] * decay, 0.0))
        qk_s[c, gi] = (sc[ch:2 * ch] * decay).astype(BF16)

    def b_square(u):
        m_cs[u] = jnp.dot(n_cs[u].astype(BF16), expand(n_cs[u]), preferred_element_type=F32)
        p_cs[u] = eye_f - n_cs[u]

    def b_level(u, last):
        m_bd = expand(m_cs[u])
        if last:
            p_cs[u] = p_cs[u] + jnp.dot(p_cs[u].astype(BF16), m_bd, preferred_element_type=F32)
        else:
            both = jnp.dot(jnp.concatenate([m_cs[u], p_cs[u]], axis=0).astype(BF16), m_bd,
                           preferred_element_type=F32)
            m_cs[u] = both[0:ch]
            p_cs[u] = p_cs[u] + both[ch:2 * ch]

    def b_apply(u):
        c, gi = units[u]
        rows = slice(c * ch, (c + 1) * ch)
        h0 = gi * gh
        rhs = jnp.concatenate(
            [jnp.concatenate([vb_s[rd_slot, rows, (h0 + hl) * dk:(h0 + hl + 1) * dk],
                              wr_s[rd_slot, rows, (h0 + hl) * dk:(h0 + hl + 1) * dk]], axis=1)
             for hl in range(gh)], axis=0)
        uw = rhs + jnp.dot(expand(p_cs[u] - eye_f), rhs.astype(BF16), preferred_element_type=F32)
        for hl in range(gh):
            hs = slice((h0 + hl) * dk, (h0 + hl + 1) * dk)
            uin_s[rows, hs] = uw[hl * ch:(hl + 1) * ch, 0:dk]
            win_s[rows, hs] = uw[hl * ch:(hl + 1) * ch, dk:2 * dk].astype(BF16)

    def b_recur(c, gi):
        rows = slice(c * ch, (c + 1) * ch)
        last = (c + 1) * ch - 1
        gs = slice(gi * gw, (gi + 1) * gw)
        u_parts, qs_parts = [], []
        for p in range(gh // 2):
            ps = slice(gi * gw + 2 * p * dk, gi * gw + (2 * p + 2) * dk)
            s_pair = s_ref[:, ps]
            s_bd = jnp.where(bd_s, jnp.concatenate([s_pair, s_pair], axis=0), 0.0).astype(BF16)
            r = jnp.dot(jnp.concatenate([win_s[rows, ps], qd_s[rd_slot, rows, ps]], axis=0), s_bd,
                        preferred_element_type=F32)
            u_parts.append(uin_s[rows, ps] - r[0:ch])
            qs_parts.append(r[ch:2 * ch])
        u_g = jnp.concatenate(u_parts, axis=1)
        u_bd = jnp.where(bd_u, jnp.concatenate([u_g] * gh, axis=0), 0.0).astype(BF16)
        o_s[rows, gs] = jnp.concatenate(qs_parts, axis=1) + jnp.dot(qk_s[c, gi], u_bd,
                                                                      preferred_element_type=F32)
        kd_stack = jnp.concatenate(
            [kd_s[rd_slot, rows, gi * gw + hl * dk:gi * gw + (hl + 1) * dk] for hl in range(gh)], axis=0)
        s_ref[:, gs] = (s_ref[:, gs] * jnp.exp(gcol_s[rd_slot, last:last + 1, gs])
                        + lax.dot_general(kd_stack, u_bd, (((0,), (0,)), ((), ())),
                                          preferred_element_type=F32))

    def b_out():
        outs = [_rms(o_s[:, h * dk:(h + 1) * dk], onw_ref[...]) for h in range(nh)]
        o = jnp.concatenate(outs, axis=1) * z_s[rd_slot]
        xo_ref[0] = xb_ref[0] + _dot(o, wout_ref[...])
        for h in range(nh):
            ssm_ref[0, h] = s_ref[:, h * dk:(h + 1) * dk]

    for c, gi in units:
        back.append((2, functools.partial(b_scores, c, gi)))
    for u in range(len(units)):
        back.append((1, functools.partial(b_square, u)))
    sq = 2
    while sq < ch:
        for u in range(len(units)):
            back.append((1, functools.partial(b_level, u, 2 * sq >= ch)))
        sq *= 2
    for u in range(len(units)):
        back.append((1.5, functools.partial(b_apply, u)))
    for c in range(nch):
        for gi in range(nh // gh):
            back.append((3, functools.partial(b_recur, c, gi)))
    back.append((8, b_out))

    _interleave([front, back])


def _ffn_sample_chunk(xn_ref, wg_ref, wv_ref, cwg_ref, cwv_ref, cbg_ref, cbv_ref, wdn_ref, stg_ref, stv_ref,
                      hg_ref, hv_ref):
    conv = []
    for w_ref, cw_ref, cb_ref, st_ref, ho_ref in ((wg_ref, cwg_ref, cbg_ref, stg_ref, hg_ref),
                                                  (wv_ref, cwv_ref, cbv_ref, stv_ref, hv_ref)):
        h = jnp.dot(xn_ref[...], w_ref[...], preferred_element_type=F32)
        conv.append(st_ref[0] * cw_ref[0:1, :] + st_ref[1] * cw_ref[1:2, :] + h * cw_ref[2:3, :] + cb_ref[...])
        ho_ref[...] = h
    a = (_silu(conv[0]) * conv[1]).astype(BF16)
    return jnp.dot(a, wdn_ref[...], preferred_element_type=F32)


def _sample_layer0_kernel(x_ref, pb_ref, n1_ref, pw_ref, ps_ref, n2_ref,
                          wg_ref, wv_ref, cwg_ref, cwv_ref, cbg_ref, cbv_ref, wdn_ref, stg_ref, stv_ref,
                          xo_ref, ho_ref, stog_ref, stov_ref, xn_s):
    c = pl.program_id(0)
    nbuf = pb_ref.shape[0]
    gd = pw_ref.shape[1]

    @pl.when(c == 0)
    def _():
        x = x_ref[...]
        h = _rms(x, n1_ref[...])
        parts = []
        for g, w in enumerate(POOL_WINDOWS):
            cols = slice(g * gd, (g + 1) * gd)
            s = h[:, cols]
            for j in range(1, w):
                s = s + pb_ref[nbuf - j, :, cols]
            dg = s / float(w) - h[:, cols]
            parts.append(x[:, cols] + _dot(dg, pw_ref[g]) * ps_ref[:, cols])
        x1 = jnp.concatenate(parts, axis=1)
        ho_ref[...] = h
        xn_s[...] = _rms(x1, n2_ref[...]).astype(BF16)
        xo_ref[...] = x1

    xo_ref[...] += _ffn_sample_chunk(xn_s, wg_ref, wv_ref, cwg_ref, cwv_ref, cbg_ref, cbv_ref, wdn_ref,
                                     stg_ref, stv_ref, stog_ref, stov_ref)


def _sample_dn_proj_kernel(x_ref, n1_ref, win_ref, wab_ref, cw_ref, alog_ref, dtb_ref,
                           selg_ref, selb_ref, cst_ref,
                           w_ref, qd_ref, k_ref, vb_ref, qk_ref, eg_ref, z_ref, pre_ref):
    dk = DN_HEAD_DIM
    dn = win_ref.shape[1] // 4
    nh = dn // dk
    xb = _rms(x_ref[...], n1_ref[...]).astype(BF16)
    pre = jnp.dot(xb, win_ref[:, 0:3 * dn], preferred_element_type=F32)
    y = (cst_ref[0] * cw_ref[0:1, :] + cst_ref[1] * cw_ref[1:2, :] + cst_ref[2] * cw_ref[2:3, :]
         + pre * cw_ref[3:4, :])
    qkv = _silu(y)
    pre_ref[...] = pre
    z_ref[...] = jnp.dot(xb, win_ref[:, 3 * dn:4 * dn], preferred_element_type=F32)
    ab = jnp.dot(xb, wab_ref[...], preferred_element_type=F32)
    g = -jnp.exp(alog_ref[...]) * _softplus(ab + dtb_ref[...])
    eg = jnp.exp(_dot_exact_rhs(g, selg_ref[...]))
    betab = _dot_exact_rhs(1.0 / (1.0 + jnp.exp(-ab)), selb_ref[...])
    eg_ref[...] = eg
    for h in range(nh):
        hs = slice(h * dk, (h + 1) * dk)
        q = qkv[:, hs]
        k = qkv[:, dn + h * dk:dn + (h + 1) * dk]
        v = qkv[:, 2 * dn + h * dk:2 * dn + (h + 1) * dk]
        q = q * lax.rsqrt(jnp.sum(q * q, axis=-1, keepdims=True) + NORM_EPS) * (dk ** -0.5)
        k = k * lax.rsqrt(jnp.sum(k * k, axis=-1, keepdims=True) + NORM_EPS)
        kb = k * betab[:, hs]
        qk = jnp.sum(q.astype(BF16).astype(F32) * k.astype(BF16).astype(F32), axis=-1, keepdims=True)
        w_ref[:, hs] = kb * eg[:, hs]
        qd_ref[:, hs] = q * eg[:, hs]
        k_ref[:, hs] = k
        vb_ref[:, hs] = v * betab[:, hs]
        qk_ref[:, hs] = jnp.broadcast_to(qk, (q.shape[0], dk))


def _sample_dn_state_kernel(w_ref, qd_ref, k_ref, vb_ref, qk_ref, eg_ref, s_ref, o_ref, so_ref):
    dk = DN_HEAD_DIM
    nb, nh = s_ref.shape[0], s_ref.shape[1]
    eye = (lax.broadcasted_iota(jnp.int32, (dk, dk), 0) == lax.broadcasted_iota(jnp.int32, (dk, dk), 1))
    for b in range(nb):
        for h in range(nh):
            hs = slice(h * dk, (h + 1) * dk)
            s = s_ref[b, h]
            lhs = jnp.concatenate([jnp.broadcast_to(w_ref[b:b + 1, hs], (8, dk)),
                                   jnp.broadcast_to(qd_ref[b:b + 1, hs], (8, dk))], axis=0)
            r = _dot(lhs, s)
            u = vb_ref[b:b + 1, hs] - r[0:1]
            ub = u.astype(BF16).astype(F32)
            o_ref[b:b + 1, hs] = r[8:9] + qk_ref[b:b + 1, hs] * ub
            kdiag = jnp.where(eye, jnp.broadcast_to(k_ref[b:b + 1, hs], (dk, dk)), 0.0)
            so_ref[b, h] = s * eg_ref[b:b + 1, hs] + _dot(kdiag, jnp.broadcast_to(u, (dk, dk)))


def _sample_tail_kernel(x_ref, o_ref, z_ref, onw_ref, wout_ref, n2_ref,
                        wg_ref, wv_ref, cwg_ref, cwv_ref, cbg_ref, cbv_ref, wdn_ref, stg_ref, stv_ref, fn_ref,
                        yo_ref, stog_ref, stov_ref, xn_s, acc_s):
    c = pl.program_id(0)
    dk = DN_HEAD_DIM

    @pl.when(c == 0)
    def _():
        outs = []
        for h in range(o_ref.shape[1] // dk):
            outs.append(_rms(o_ref[:, h * dk:(h + 1) * dk], onw_ref[...]))
        o = jnp.concatenate(outs, axis=1) * _silu(z_ref[...])
        x = x_ref[...] + _dot(o, wout_ref[...])
        acc_s[...] = x
        xn_s[...] = _rms(x, n2_ref[...]).astype(BF16)

    acc_s[...] += _ffn_sample_chunk(xn_s, wg_ref, wv_ref, cwg_ref, cwv_ref, cbg_ref, cbv_ref, wdn_ref,
                                    stg_ref, stv_ref, stog_ref, stov_ref)

    @pl.when(c == pl.num_programs(0) - 1)
    def _():
        yo_ref[...] = _rms(acc_s[...], fn_ref[...])


def _params(*sem):
    return pltpu.CompilerParams(dimension_semantics=sem, vmem_limit_bytes=VMEM_LIMIT)


def kernel(x_prompt, x_sample, state_pool_buf, state_dn_conv, state_dn_ssm, state_ffn_conv, norm1_w, norm2_w,
           final_norm_w, pool_w, pool_scale, dn_w_in, dn_conv_w, dn_a_log, dn_dt_bias, dn_o_norm_w, dn_w_out,
           ffn_w_up, ffn_conv_w, ffn_conv_b, ffn_w_down):
    bp, seq, d = x_prompt.shape
    bs = x_sample.shape[0]
    nbuf = state_pool_buf.shape[2]
    dff = ffn_w_down.shape[1]
    nh = dn_a_log.shape[1]
    dk = DN_HEAD_DIM
    dn = nh * dk
    gd = pool_w.shape[2]
    tm = PROMPT_TILE
    nt = seq // tm
    nfc = dff // FFN_CHUNK
    assert seq % tm == 0 and dff % FFN_CHUNK == 0 and (3 * dn) % QKV_CHUNK == 0 and dk == 128
    assert nbuf + 1 == max(POOL_WINDOWS) and 2 * nh <= 128 and bs % SAMPLE_BLOCK == 0

    row = lambda v: v.reshape(1, -1)
    wup = ffn_w_up.astype(BF16)
    wdn = ffn_w_down.astype(BF16)
    pw = pool_w.astype(BF16)
    w_in = dn_w_in[0]
    wmain = w_in[:, :4 * dn].astype(BF16)
    wab = jnp.pad(w_in[:, 4 * dn:], ((0, 0), (0, 128 - 2 * nh))).astype(BF16)
    wout = dn_w_out[0].astype(BF16)
    alog = jnp.pad(dn_a_log[0], (0, 128 - nh)).reshape(1, 128)
    dtb = jnp.pad(dn_dt_bias[0], (0, 128 - nh)).reshape(1, 128)
    onw = row(dn_o_norm_w[0])
    lane_head = jnp.arange(dn) // dk
    selg = (jnp.arange(128)[:, None] == lane_head[None, :]).astype(BF16)
    selb = (jnp.arange(128)[:, None] == (lane_head[None, :] + nh)).astype(BF16)
    ti = jnp.arange(tm)
    ltri = ((ti[:, None] // DN_CHUNK == ti[None, :] // DN_CHUNK) & (ti[:, None] >= ti[None, :])).astype(BF16)

    tf = FFN_TILE
    assert seq % tf == 0
    x_spec = pl.BlockSpec((1, tf, d), lambda b, t: (b, t, 0))
    layer_resident = lambda shape, layer: pl.BlockSpec((None,) + shape, lambda *_: (layer,) + (0,) * len(shape),
                                                       pipeline_mode=pl.Buffered(1))
    ffn_w_specs = lambda layer: [layer_resident((d, 2 * dff), layer), _resident((3, 2 * dff)),
                                 _resident((1, 2 * dff)), layer_resident((dff, d), layer)]
    fst_spec = pl.BlockSpec((1, CONV_HALO, 2 * dff), lambda b, t: (b, 0, 0))
    fst_shape = jax.ShapeDtypeStruct((bp, CONV_HALO, 2 * dff), F32)

    x2, pool_tail, fst0 = pl.pallas_call(
        _prompt_layer0_kernel,
        grid=(bp, seq // tf),
        in_specs=[x_spec, _resident((1, d)), _resident(pw.shape[1:]), _resident((1, d)), _resident((1, d))]
        + ffn_w_specs(0),
        out_specs=[x_spec, pl.BlockSpec((1, POOL_HALO, d), lambda b, t: (b, 0, 0)), fst_spec],
        out_shape=[jax.ShapeDtypeStruct((bp, seq, d), F32), jax.ShapeDtypeStruct((bp, POOL_HALO, d), F32), fst_shape],
        scratch_shapes=[pltpu.VMEM((POOL_HALO + tf, d), F32), pltpu.VMEM((CONV_HALO, 2 * dff), F32)],
        compiler_params=_params("arbitrary", "arbitrary"),
        name="prompt_layer0",
    )(x_prompt, row(norm1_w[0]), pw[0], row(pool_scale[0]), row(norm2_w[0]),
      wup, ffn_conv_w[0], row(ffn_conv_b[0]), wdn)

    ntiles = bp * nt
    front_tile = lambda i: jnp.minimum(i, ntiles - 1)
    back_tile = lambda i: jnp.maximum(i - 1, 0)
    x_front = pl.BlockSpec((1, tm, d), lambda i: (front_tile(i) // nt, front_tile(i) % nt, 0))
    x_back = pl.BlockSpec((1, tm, d), lambda i: (back_tile(i) // nt, back_tile(i) % nt, 0))
    x3, cst, ssm = pl.pallas_call(
        functools.partial(_prompt_deltanet_kernel, tiles_per_seq=nt),
        grid=(ntiles + 1,),
        in_specs=[x_front, x_back,
                  _resident((1, d)), _resident((d, 4 * dn)), _resident((d, 128)),
                  _resident((4, 3 * dn)), _resident((1, 128)), _resident((1, 128)), _resident((1, dk)),
                  _resident((dn, d)), _resident((tm, tm))],
        out_specs=[x_back,
                   pl.BlockSpec((1, CONV_HALO, 3 * dn), lambda i: (front_tile(i) // nt, 0, 0)),
                   pl.BlockSpec((1, nh, dk, dk), lambda i: (back_tile(i) // nt, 0, 0, 0))],
        out_shape=[jax.ShapeDtypeStruct((bp, seq, d), F32), jax.ShapeDtypeStruct((bp, CONV_HALO, 3 * dn), F32),
                   jax.ShapeDtypeStruct((bp, nh, dk, dk), F32)],
        scratch_shapes=[pltpu.VMEM((CONV_HALO, 3 * dn), F32), pltpu.VMEM((dk, dn), F32),
                        pltpu.VMEM((tm, 3 * dn), F32)]
        + [pltpu.VMEM((2, tm, dn), BF16)] * 5 + [pltpu.VMEM((2, tm, dn), F32)] * 4
        + [pltpu.VMEM((tm, dn), F32), pltpu.VMEM((tm, dn), BF16),
           pltpu.VMEM((tm // DN_CHUNK, nh // HEAD_GROUP, DN_CHUNK, HEAD_GROUP * DN_CHUNK), BF16),
           pltpu.VMEM((tm, dn), F32)],
        compiler_params=_params("arbitrary"),
        name="prompt_deltanet",
    )(x2, x2, row(norm1_w[1]), wmain, wab, dn_conv_w[0], alog, dtb, onw, wout, ltri)

    y_prompt, fst1 = pl.pallas_call(
        _prompt_ffn_final_kernel,
        grid=(bp, seq // tf),
        in_specs=[x_spec, _resident((1, d))] + ffn_w_specs(1) + [_resident((1, d))],
        out_specs=[x_spec, fst_spec],
        out_shape=[jax.ShapeDtypeStruct((bp, seq, d), F32), fst_shape],
        scratch_shapes=[pltpu.VMEM((CONV_HALO, 2 * dff), F32)],
        compiler_params=_params("arbitrary", "arbitrary"),
        name="prompt_ffn_final",
    )(x3, row(norm2_w[1]), wup, ffn_conv_w[1], row(ffn_conv_b[1]), wdn, row(final_norm_w))

    xs = x_sample[:, 0, :]
    pb = jnp.transpose(state_pool_buf[0], (1, 0, 2))
    cs = jnp.transpose(state_dn_conv[0], (1, 0, 2))
    fs = jnp.transpose(state_ffn_conv, (0, 2, 1, 3))
    sc = SAMPLE_FFN_CHUNK
    nsc = dff // sc
    assert dff % sc == 0

    def ffn_chunk_specs(layer):
        specs = []
        for shape in ((None, d, sc), (3, sc), (1, sc)):
            for half in range(2):
                if shape[0] is None:
                    specs.append(pl.BlockSpec(shape, lambda c, half=half: (layer, 0, half * nsc + c)))
                else:
                    specs.append(pl.BlockSpec(shape, lambda c, half=half: (0, half * nsc + c)))
        specs.append(pl.BlockSpec((None, sc, d), lambda c: (layer, c, 0)))
        for half in range(2):
            specs.append(pl.BlockSpec((None, 2, bs, sc), lambda c, half=half: (layer, 0, 0, half * nsc + c)))
        return specs

    h_specs = [pl.BlockSpec((bs, sc), lambda c: (0, c))] * 2
    h_shapes = [jax.ShapeDtypeStruct((bs, dff), F32)] * 2

    def ffn_chunk_args(layer):
        cb = row(ffn_conv_b[layer])
        return (wup, wup, ffn_conv_w[layer], ffn_conv_w[layer], cb, cb, wdn, fs, fs)

    xs2, pool_row, hg0, hv0 = pl.pallas_call(
        _sample_layer0_kernel,
        grid=(nsc,),
        in_specs=[_resident((bs, d)), _resident((nbuf, bs, d)), _resident((1, d)), _resident(pw.shape[1:]),
                  _resident((1, d)), _resident((1, d))] + ffn_chunk_specs(0),
        out_specs=[_full((bs, d)), _full((bs, d))] + h_specs,
        out_shape=[jax.ShapeDtypeStruct((bs, d), F32), jax.ShapeDtypeStruct((bs, d), F32)] + h_shapes,
        scratch_shapes=[pltpu.VMEM((bs, d), BF16)],
        compiler_params=_params("arbitrary"),
        name="sample_layer0",
    )(xs, pb, row(norm1_w[0]), pw[0], row(pool_scale[0]), row(norm2_w[0]), *ffn_chunk_args(0))

    vec = jax.ShapeDtypeStruct((bs, dn), F32)
    w_s, qd_s, k_s, vb_s, qk_s, eg_s, z_s, conv_row = pl.pallas_call(
        _sample_dn_proj_kernel,
        out_shape=[vec] * 7 + [jax.ShapeDtypeStruct((bs, 3 * dn), F32)],
        compiler_params=pltpu.CompilerParams(vmem_limit_bytes=VMEM_LIMIT),
        name="sample_dn_proj",
    )(xs2, row(norm1_w[1]), wmain, wab, dn_conv_w[0], alog, dtb, selg, selb, cs)

    nb = SAMPLE_BLOCK
    vspec = pl.BlockSpec((nb, dn), lambda i: (i, 0))
    sspec = pl.BlockSpec((nb, nh, dk, dk), lambda i: (i, 0, 0, 0))
    o_s, ssm_s = pl.pallas_call(
        _sample_dn_state_kernel,
        grid=(bs // nb,),
        in_specs=[vspec] * 6 + [sspec],
        out_specs=[vspec, sspec],
        out_shape=[vec, jax.ShapeDtypeStruct((bs, nh, dk, dk), F32)],
        compiler_params=_params("arbitrary"),
        name="sample_dn_state",
    )(w_s, qd_s, k_s, vb_s, qk_s, eg_s, state_dn_ssm[0])

    ys, hg1, hv1 = pl.pallas_call(
        _sample_tail_kernel,
        grid=(nsc,),
        in_specs=[_resident((bs, d)), _resident((bs, dn)), _resident((bs, dn)), _resident((1, dk)),
                  _resident((dn, d)), _resident((1, d))] + ffn_chunk_specs(1) + [_resident((1, d))],
        out_specs=[_full((bs, d))] + h_specs,
        out_shape=[jax.ShapeDtypeStruct((bs, d), F32)] + h_shapes,
        scratch_shapes=[pltpu.VMEM((bs, d), BF16), pltpu.VMEM((bs, d), F32)],
        compiler_params=_params("arbitrary"),
        name="sample_tail",
    )(xs2, o_s, z_s, onw, wout, row(norm2_w[1]), *ffn_chunk_args(1), row(final_norm_w))

    shifted = lambda state, new_row: jnp.concatenate([state[:, 1:], new_row[:, None]], axis=1)
    ffn_rows = [jnp.concatenate([hg0, hv0], axis=-1), jnp.concatenate([hg1, hv1], axis=-1)]
    return (
        y_prompt,
        ys[:, None, :],
        pool_tail[None, :, POOL_HALO - nbuf:, :],
        shifted(state_pool_buf[0], pool_row)[None],
        cst[None, :, CONV_HALO - 3:, :],
        shifted(state_dn_conv[0], conv_row)[None],
        ssm[None],
        ssm_s[None],
        jnp.stack([fst0[:, CONV_HALO - 2:, :], fst1[:, CONV_HALO - 2:, :]]),
        jnp.stack([shifted(state_ffn_conv[layer], ffn_rows[layer]) for layer in range(2)]),
    )
```

```python
import functools

import jax
import jax.numpy as jnp
from jax import lax
from jax.experimental import pallas as pl
from jax.experimental.pallas import tpu as pltpu

F32 = jnp.float32
BF16 = jnp.bfloat16

NORM_EPS = 1e-6
POOL_WINDOWS = (2, 4, 8, 16)
POOL_HALO = 16
CONV_HALO = 8
DN_HEAD_DIM = 128
DN_CHUNK = 64
FFN_CHUNK = 256
FFN_DOWN_GROUP = 4
FFN_LOOKAHEAD = 3
QKV_CHUNK = 256
PROMPT_TILE = 256
FFN_TILE = 512
HEAD_GROUP = 4
SAMPLE_FFN_CHUNK = 1408
SAMPLE_BLOCK = 8
VMEM_LIMIT = 56 * 1024 * 1024


def _dot(a, b):
    return jnp.dot(a.astype(BF16), b.astype(BF16), preferred_element_type=F32)


def _dot_nt(a, b):
    return lax.dot_general(a.astype(BF16), b.astype(BF16), (((1,), (1,)), ((), ())),
                           preferred_element_type=F32)


def _split3(x):
    hi = x.astype(BF16)
    r = x - hi.astype(F32)
    mid = r.astype(BF16)
    lo = (r - mid.astype(F32)).astype(BF16)
    return hi, mid, lo


def _dot_exact_rhs(x, e):
    hi, mid, lo = _split3(x)
    return (jnp.dot(lo, e, preferred_element_type=F32) + jnp.dot(mid, e, preferred_element_type=F32)
            + jnp.dot(hi, e, preferred_element_type=F32))


def _dot_exact_lhs(e, x):
    hi, mid, lo = _split3(x)
    return (jnp.dot(e, lo, preferred_element_type=F32) + jnp.dot(e, mid, preferred_element_type=F32)
            + jnp.dot(e, hi, preferred_element_type=F32))


def _dot3(a, b):
    a_hi = a.astype(BF16)
    a_lo = (a - a_hi.astype(F32)).astype(BF16)
    b_hi = b.astype(BF16)
    b_lo = (b - b_hi.astype(F32)).astype(BF16)
    return (jnp.dot(a_lo, b_hi, preferred_element_type=F32) + jnp.dot(a_hi, b_lo, preferred_element_type=F32)
            + jnp.dot(a_hi, b_hi, preferred_element_type=F32))


def _rms(x, w):
    return x * lax.rsqrt(jnp.mean(x * x, axis=-1, keepdims=True) + NORM_EPS) * w


def _silu(x):
    return x / (1.0 + jnp.exp(-x))


def _softplus(x):
    return jnp.maximum(x, 0.0) + jnp.log1p(jnp.exp(-jnp.abs(x)))


def _full(shape):
    return pl.BlockSpec(shape, lambda *_: (0,) * len(shape))


def _resident(shape):
    return pl.BlockSpec(shape, lambda *_: (0,) * len(shape), pipeline_mode=pl.Buffered(1))


def _ffn_tile(xn_b, wup_ref, cw_ref, cb_ref, wdn_ref, carry_ref):
    tm = xn_b.shape[0]
    dff = wdn_ref.shape[0]
    nchunk = dff // FFN_CHUNK

    def up(c):
        return [jnp.dot(xn_b, wup_ref[:, half * dff + c * FFN_CHUNK:half * dff + (c + 1) * FFN_CHUNK],
                        preferred_element_type=F32) for half in range(2)]

    def conv_act(c, hs):
        conv = []
        for half, h in enumerate(hs):
            cols = slice(half * dff + c * FFN_CHUNK, half * dff + (c + 1) * FFN_CHUNK)
            ext = jnp.concatenate([carry_ref[:, cols], h], axis=0)
            y = (ext * cw_ref[2:3, cols] + pltpu.roll(ext, 1, 0) * cw_ref[1:2, cols]
                 + pltpu.roll(ext, 2, 0) * cw_ref[0:1, cols])
            conv.append(y[CONV_HALO:] + cb_ref[:, cols])
            carry_ref[:, cols] = h[tm - CONV_HALO:]
        return (_silu(conv[0]) * conv[1]).astype(BF16)

    acc = None
    acts = []
    ahead = [up(c) for c in range(min(FFN_LOOKAHEAD, nchunk))]
    for c in range(nchunk):
        cur = ahead.pop(0)
        if c + FFN_LOOKAHEAD < nchunk:
            ahead.append(up(c + FFN_LOOKAHEAD))
        acts.append(conv_act(c, cur))
        if len(acts) == FFN_DOWN_GROUP or c + 1 == nchunk:
            c0 = c + 1 - len(acts)
            a = acts[0] if len(acts) == 1 else jnp.concatenate(acts, axis=1)
            d = jnp.dot(a, wdn_ref[c0 * FFN_CHUNK:(c + 1) * FFN_CHUNK, :], preferred_element_type=F32)
            acc = d if acc is None else acc + d
            acts = []
    return acc


def _prompt_layer0_kernel(x_ref, n1_ref, pw_ref, ps_ref, n2_ref, wup_ref, cw_ref, cb_ref, wdn_ref,
                          xo_ref, pool_ref, fst_ref, hbuf, carry):
    t = pl.program_id(1)
    tm = x_ref.shape[1]
    gd = pw_ref.shape[1]

    @pl.when(t == 0)
    def _():
        hbuf[0:POOL_HALO, :] = jnp.zeros((POOL_HALO, hbuf.shape[1]), F32)
        carry[...] = jnp.zeros(carry.shape, F32)

    x = x_ref[0]
    h = _rms(x, n1_ref[...])
    hbuf[POOL_HALO:POOL_HALO + tm, :] = h
    pos1 = t * tm + lax.broadcasted_iota(jnp.int32, (tm, 1), 0) + 1
    parts = []
    for g, w in enumerate(POOL_WINDOWS):
        cols = slice(g * gd, (g + 1) * gd)
        s = hbuf[:, cols]
        sh = 1
        while sh < w:
            s = s + pltpu.roll(s, sh, 0)
            sh *= 2
        cnt = jnp.minimum(w, pos1).astype(F32)
        dg = s[POOL_HALO:] / cnt - h[:, cols]
        parts.append(x[:, cols] + _dot(dg, pw_ref[g]) * ps_ref[:, cols])
    x1 = jnp.concatenate(parts, axis=1)
    tail = hbuf[tm:tm + POOL_HALO, :]
    pool_ref[0] = tail
    hbuf[0:POOL_HALO, :] = tail

    xn = _rms(x1, n2_ref[...]).astype(BF16)
    xo_ref[0] = x1 + _ffn_tile(xn, wup_ref, cw_ref, cb_ref, wdn_ref, carry)
    fst_ref[0] = carry[...]


def _prompt_ffn_final_kernel(x_ref, n2_ref, wup_ref, cw_ref, cb_ref, wdn_ref, fn_ref,
                             yo_ref, fst_ref, carry):
    @pl.when(pl.program_id(1) == 0)
    def _():
        carry[...] = jnp.zeros(carry.shape, F32)

    x = x_ref[0]
    xn = _rms(x, n2_ref[...]).astype(BF16)
    x = x + _ffn_tile(xn, wup_ref, cw_ref, cb_ref, wdn_ref, carry)
    yo_ref[0] = _rms(x, fn_ref[...])
    fst_ref[0] = carry[...]


def _interleave(streams):
    merged = []
    for si, steps in enumerate(streams):
        total = float(sum(w for w, _ in steps))
        done = 0.0
        for w, thunk in steps:
            merged.append(((done + 0.5 * w) / total, si, len(merged), thunk))
            done += w
    for _, _, _, thunk in sorted(merged, key=lambda m: m[:3]):
        thunk()


def _prompt_deltanet_kernel(xa_ref, xb_ref, n1_ref, win_ref, wab_ref, cw_ref, alog_ref, dtb_ref, onw_ref,
                            wout_ref, ltri_ref,
                            xo_ref, cst_ref, ssm_ref,
                            carry, s_ref, qkv_s, kb_s, q_s, k_s, qd_s, kd_s, vb_s, wr_s, gcol_s, z_s, uin_s, win_s,
                            qk_s, o_s, *, tiles_per_seq):
    i = pl.program_id(0)
    tm = xa_ref.shape[1]
    dk = DN_HEAD_DIM
    dn = s_ref.shape[1]
    nh = dn // dk
    ch = DN_CHUNK
    nch = tm // ch
    gh = HEAD_GROUP
    gw = gh * dk
    pw = gh * ch
    wr_slot = i % 2
    rd_slot = 1 - wr_slot

    @pl.when(i == 0)
    def _():
        for ref in (kb_s, q_s, k_s, qd_s, kd_s, vb_s, wr_s, gcol_s, z_s):
            ref[1] = jnp.zeros(ref.shape[1:], ref.dtype)
        s_ref[...] = jnp.zeros(s_ref.shape, F32)

    @pl.when(i % tiles_per_seq == 0)
    def _():
        carry[...] = jnp.zeros(carry.shape, F32)

    @pl.when((i + tiles_per_seq - 1) % tiles_per_seq == 0)
    def _():
        s_ref[...] = jnp.zeros(s_ref.shape, F32)

    front, back = [], []
    env = {}

    def f_norm():
        env["xb"] = _rms(xa_ref[0], n1_ref[...]).astype(BF16)

    def f_qkv(c):
        cols = slice(c * QKV_CHUNK, (c + 1) * QKV_CHUNK)
        pre = jnp.dot(env["xb"], win_ref[:, cols], preferred_element_type=F32)
        ext = jnp.concatenate([carry[:, cols], pre], axis=0)
        prev = pltpu.roll(ext, 1, 0)
        y = (ext * cw_ref[3:4, cols] + prev * cw_ref[2:3, cols]
             + pltpu.roll(ext * cw_ref[1:2, cols] + prev * cw_ref[0:1, cols], 2, 0))
        qkv_s[:, cols] = _silu(y[CONV_HALO:])
        carry[:, cols] = pre[tm - CONV_HALO:]
        cst_ref[0, :, cols] = pre[tm - CONV_HALO:]

    def f_gates():
        ab = jnp.dot(env["xb"], wab_ref[...], preferred_element_type=F32)
        g = -jnp.exp(alog_ref[...]) * _softplus(ab + dtb_ref[...])
        env["gc"] = _dot_exact_lhs(ltri_ref[...], g)
        env["sig"] = 1.0 / (1.0 + jnp.exp(-ab))

    def f_z():
        z_s[wr_slot] = _silu(jnp.dot(env["xb"], win_ref[:, 3 * dn:4 * dn], preferred_element_type=F32))

    def f_head(h):
        hs = slice(h * dk, (h + 1) * dk)
        gcol = jnp.broadcast_to(env["gc"][:, h:h + 1], (tm, dk))
        beta = jnp.broadcast_to(env["sig"][:, nh + h:nh + h + 1], (tm, dk))
        glast = jnp.concatenate(
            [jnp.broadcast_to(gcol[(c + 1) * ch - 1:(c + 1) * ch, :], (ch, dk)) for c in range(nch)], axis=0)
        egc = jnp.exp(gcol)
        qh = qkv_s[:, hs]
        kh = qkv_s[:, dn + h * dk:dn + (h + 1) * dk]
        vh = qkv_s[:, 2 * dn + h * dk:2 * dn + (h + 1) * dk]
        qh = qh * (lax.rsqrt(jnp.sum(qh * qh, axis=-1, keepdims=True) + NORM_EPS) * (dk ** -0.5))
        kh = kh * lax.rsqrt(jnp.sum(kh * kh, axis=-1, keepdims=True) + NORM_EPS)
        kb = kh * beta
        gcol_s[wr_slot, :, hs] = gcol
        q_s[wr_slot, :, hs] = qh.astype(BF16)
        k_s[wr_slot, :, hs] = kh.astype(BF16)
        kb_s[wr_slot, :, hs] = kb.astype(BF16)
        qd_s[wr_slot, :, hs] = (qh * egc).astype(BF16)
        kd_s[wr_slot, :, hs] = (kh * jnp.exp(glast - gcol)).astype(BF16)
        vb_s[wr_slot, :, hs] = vh * beta
        wr_s[wr_slot, :, hs] = kb * egc

    front.append((2, f_norm))
    per_part = dn // QKV_CHUNK
    for c in [part * per_part + j for j in range(per_part) for part in range(3)]:
        front.append((20.0 * QKV_CHUNK / 1024, functools.partial(f_qkv, c)))
    front.append((3, f_gates))
    for h in range(nh):
        front.append((3, functools.partial(f_head, h)))
    front.append((5, f_z))

    def iota2(shape, d):
        return lax.broadcasted_iota(jnp.int32, shape, d)

    pi, pj = iota2((ch, pw), 0), iota2((ch, pw), 1) % ch
    eye_p = pi == pj
    incl_p = pi >= pj
    strict_p = pi > pj
    eye_f = eye_p.astype(F32)
    head_of_lane = iota2((ch, pw), 1) // ch
    bd_p = (iota2((pw, pw), 0) // ch) == (iota2((pw, pw), 1) // ch)
    bd_k = (iota2((2 * ch, 2 * dk), 0) // ch) == (iota2((2 * ch, 2 * dk), 1) // dk)
    bd_u = (iota2((pw, gw), 0) // ch) == (iota2((pw, gw), 1) // dk)
    bd_s = (iota2((2 * dk, 2 * dk), 0) // dk) == (iota2((2 * dk, 2 * dk), 1) // dk)

    def expand(mc):
        return jnp.where(bd_p, jnp.concatenate([mc] * gh, axis=0), 0.0).astype(BF16)

    units = [(c, gi) for c in range(nch) for gi in range(nh // gh)]
    n_cs, m_cs, p_cs = [], [None] * len(units), [None] * len(units)

    def b_scores(c, gi):
        rows = slice(c * ch, (c + 1) * ch)
        h0 = gi * gh
        sc = []
        for p in range(gh // 2):
            ps = slice((h0 + 2 * p) * dk, (h0 + 2 * p + 2) * dk)
            kc = k_s[rd_slot, rows, ps]
            kbd = jnp.where(bd_k, jnp.concatenate([kc, kc], axis=0), jnp.zeros((), BF16))
            sc.append(lax.dot_general(jnp.concatenate([kb_s[rd_slot, rows, ps], q_s[rd_slot, rows, ps]], axis=0),
                                      kbd, (((1,), (1,)), ((), ())), preferred_element_type=F32))
        sc = jnp.concatenate(sc, axis=1)
        gcol_p = jnp.broadcast_to(gcol_s[rd_slot, rows, h0 * dk:h0 * dk + 1], (ch, pw))
        for hl in range(1, gh):
            gcol_p = jnp.where(
                head_of_lane == hl,
                jnp.broadcast_to(gcol_s[rd_slot, rows, (h0 + hl) * dk:(h0 + hl) * dk + 1], (ch, pw)), gcol_p)
        grow_p = jnp.sum(jnp.where(eye_p, gcol_p, 0.0), axis=0, keepdims=True)
        decay = jnp.exp(jnp.where(incl_p, gcol_p - grow_p, -jnp.inf))
        n_cs.append(jnp.where(strict_p, sc[0:ch] * decay, 0.0))
        qk_s[c, gi] = (sc[ch:2 * ch] * decay).astype(BF16)

    def b_square(u):
        m_cs[u] = jnp.dot(n_cs[u].astype(BF16), expand(n_cs[u]), preferred_element_type=F32)
        p_cs[u] = eye_f - n_cs[u]

    def b_level(u, last):
        m_bd = expand(m_cs[u])
        if last:
            p_cs[u] = p_cs[u] + jnp.dot(p_cs[u].astype(BF16), m_bd, preferred_element_type=F32)
        else:
            both = jnp.dot(jnp.concatenate([m_cs[u], p_cs[u]], axis=0).astype(BF16), m_bd,
                           preferred_element_type=F32)
            m_cs[u] = both[0:ch]
            p_cs[u] = p_cs[u] + both[ch:2 * ch]

    def b_apply(u):
        c, gi = units[u]
        rows = slice(c * ch, (c + 1) * ch)
        h0 = gi * gh
        rhs = jnp.concatenate(
            [jnp.concatenate([vb_s[rd_slot, rows, (h0 + hl) * dk:(h0 + hl + 1) * dk],
                              wr_s[rd_slot, rows, (h0 + hl) * dk:(h0 + hl + 1) * dk]], axis=1)
             for hl in range(gh)], axis=0)
        uw = rhs + jnp.dot(expand(p_cs[u] - eye_f), rhs.astype(BF16), preferred_element_type=F32)
        for hl in range(gh):
            hs = slice((h0 + hl) * dk, (h0 + hl + 1) * dk)
            uin_s[rows, hs] = uw[hl * ch:(hl + 1) * ch, 0:dk]
            win_s[rows, hs] = uw[hl * ch:(hl + 1) * ch, dk:2 * dk].astype(BF16)

    def b_recur(c, gi):
        rows = slice(c * ch, (c + 1) * ch)
        last = (c + 1) * ch - 1
        gs = slice(gi * gw, (gi + 1) * gw)
        u_parts, qs_parts = [], []
        for p in range(gh // 2):
            ps = slice(gi * gw + 2 * p * dk, gi * gw + (2 * p + 2) * dk)
            s_pair = s_ref[:, ps]
            s_bd = jnp.where(bd_s, jnp.concatenate([s_pair, s_pair], axis=0), 0.0).astype(BF16)
            r = jnp.dot(jnp.concatenate([win_s[rows, ps], qd_s[rd_slot, rows, ps]], axis=0), s_bd,
                        preferred_element_type=F32)
            u_parts.append(uin_s[rows, ps] - r[0:ch])
            qs_parts.append(r[ch:2 * ch])
        u_g = jnp.concatenate(u_parts, axis=1)
        u_bd = jnp.where(bd_u, jnp.concatenate([u_g] * gh, axis=0), 0.0).astype(BF16)
        o_s[rows, gs] = jnp.concatenate(qs_parts, axis=1) + jnp.dot(qk_s[c, gi], u_bd,
                                                                      preferred_element_type=F32)
        kd_stack = jnp.concatenate(
            [kd_s[rd_slot, rows, gi * gw + hl * dk:gi * gw + (hl + 1) * dk] for hl in range(gh)], axis=0)
        s_ref[:, gs] = (s_ref[:, gs] * jnp.exp(gcol_s[rd_slot, last:last + 1, gs])
                        + lax.dot_general(kd_stack, u_bd, (((0,), (0,)), ((), ())),
                                          preferred_element_type=F32))

    def b_out():
        outs = [_rms(o_s[:, h * dk:(h + 1) * dk], onw_ref[...]) for h in range(nh)]
        o = jnp.concatenate(outs, axis=1) * z_s[rd_slot]
        xo_ref[0] = xb_ref[0] + _dot(o, wout_ref[...])
        for h in range(nh):
            ssm_ref[0, h] = s_ref[:, h * dk:(h + 1) * dk]

    for c, gi in units:
        back.append((2, functools.partial(b_scores, c, gi)))
    for u in range(len(units)):
        back.append((1, functools.partial(b_square, u)))
    sq = 2
    while sq < ch:
        for u in range(len(units)):
            back.append((1, functools.partial(b_level, u, 2 * sq >= ch)))
        sq *= 2
    for u in range(len(units)):
        back.append((1.5, functools.partial(b_apply, u)))
    for c in range(nch):
        for gi in range(nh // gh):
            back.append((3, functools.partial(b_recur, c, gi)))
    back.append((8, b_out))

    _interleave([front, back])


def _ffn_sample_chunk(xn_ref, wg_ref, wv_ref, cwg_ref, cwv_ref, cbg_ref, cbv_ref, wdn_ref, stg_ref, stv_ref,
                      hg_ref, hv_ref):
    conv = []
    for w_ref, cw_ref, cb_ref, st_ref, ho_ref in ((wg_ref, cwg_ref, cbg_ref, stg_ref, hg_ref),
                                                  (wv_ref, cwv_ref, cbv_ref, stv_ref, hv_ref)):
        h = jnp.dot(xn_ref[...], w_ref[...], preferred_element_type=F32)
        conv.append(st_ref[0] * cw_ref[0:1, :] + st_ref[1] * cw_ref[1:2, :] + h * cw_ref[2:3, :] + cb_ref[...])
        ho_ref[...] = h
    a = (_silu(conv[0]) * conv[1]).astype(BF16)
    return jnp.dot(a, wdn_ref[...], preferred_element_type=F32)


def _sample_layer0_kernel(x_ref, pb_ref, n1_ref, pw_ref, ps_ref, n2_ref,
                          wg_ref, wv_ref, cwg_ref, cwv_ref, cbg_ref, cbv_ref, wdn_ref, stg_ref, stv_ref,
                          xo_ref, ho_ref, stog_ref, stov_ref, xn_s):
    c = pl.program_id(0)
    nbuf = pb_ref.shape[0]
    gd = pw_ref.shape[1]

    @pl.when(c == 0)
    def _():
        x = x_ref[...]
        h = _rms(x, n1_ref[...])
        parts = []
        for g, w in enumerate(POOL_WINDOWS):
            cols = slice(g * gd, (g + 1) * gd)
            s = h[:, cols]
            for j in range(1, w):
                s = s + pb_ref[nbuf - j, :, cols]
            dg = s / float(w) - h[:, cols]
            parts.append(x[:, cols] + _dot(dg, pw_ref[g]) * ps_ref[:, cols])
        x1 = jnp.concatenate(parts, axis=1)
        ho_ref[...] = h
        xn_s[...] = _rms(x1, n2_ref[...]).astype(BF16)
        xo_ref[...] = x1

    xo_ref[...] += _ffn_sample_chunk(xn_s, wg_ref, wv_ref, cwg_ref, cwv_ref, cbg_ref, cbv_ref, wdn_ref,
                                     stg_ref, stv_ref, stog_ref, stov_ref)


def _sample_dn_proj_kernel(x_ref, n1_ref, win_ref, wab_ref, cw_ref, alog_ref, dtb_ref,
                           selg_ref, selb_ref, cst_ref,
                           w_ref, qd_ref, k_ref, vb_ref, qk_ref, eg_ref, z_ref, pre_ref):
    dk = DN_HEAD_DIM
    dn = cw_ref.shape[1] // 3
    nh = dn // dk
    xb = _rms(x_ref[...], n1_ref[...]).astype(BF16)
    pre = jnp.dot(xb, win_ref[:, 0:3 * dn], preferred_element_type=F32)
    y = (cst_ref[0] * cw_ref[0:1, :] + cst_ref[1] * cw_ref[1:2, :] + cst_ref[2] * cw_ref[2:3, :]
         + pre * cw_ref[3:4, :])
    qkv = _silu(y)
    pre_ref[...] = pre
    z_ref[...] = jnp.dot(xb, win_ref[:, 3 * dn:4 * dn], preferred_element_type=F32)
    ab = jnp.dot(xb, wab_ref[...], preferred_element_type=F32)
    g = -jnp.exp(alog_ref[...]) * _softplus(ab + dtb_ref[...])
    eg = jnp.exp(_dot_exact_rhs(g, selg_ref[...]))
    betab = _dot_exact_rhs(1.0 / (1.0 + jnp.exp(-ab)), selb_ref[...])
    eg_ref[...] = eg
    for h in range(nh):
        hs = slice(h * dk, (h + 1) * dk)
        q = qkv[:, hs]
        k = qkv[:, dn + h * dk:dn + (h + 1) * dk]
        v = qkv[:, 2 * dn + h * dk:2 * dn + (h + 1) * dk]
        q = q * lax.rsqrt(jnp.sum(q * q, axis=-1, keepdims=True) + NORM_EPS) * (dk ** -0.5)
        k = k * lax.rsqrt(jnp.sum(k * k, axis=-1, keepdims=True) + NORM_EPS)
        kb = k * betab[:, hs]
        qk = jnp.sum(q.astype(BF16).astype(F32) * k.astype(BF16).astype(F32), axis=-1, keepdims=True)
        w_ref[:, hs] = kb * eg[:, hs]
        qd_ref[:, hs] = q * eg[:, hs]
        k_ref[:, hs] = k
        vb_ref[:, hs] = v * betab[:, hs]
        qk_ref[:, hs] = jnp.broadcast_to(qk, (q.shape[0], dk))


def _sample_dn_state_kernel(w_ref, qd_ref, k_ref, vb_ref, qk_ref, eg_ref, s_ref, o_ref, so_ref):
    dk = DN_HEAD_DIM
    nb, nh = s_ref.shape[0], s_ref.shape[1]
    eye = (lax.broadcasted_iota(jnp.int32, (dk, dk), 0) == lax.broadcasted_iota(jnp.int32, (dk, dk), 1))
    for b in range(nb):
        for h in range(nh):
            hs = slice(h * dk, (h + 1) * dk)
            s = s_ref[b, h]
            lhs = jnp.concatenate([jnp.broadcast_to(w_ref[b:b + 1, hs], (8, dk)),
                                   jnp.broadcast_to(qd_ref[b:b + 1, hs], (8, dk))], axis=0)
            r = _dot(lhs, s)
            u = vb_ref[b:b + 1, hs] - r[0:1]
            ub = u.astype(BF16).astype(F32)
            o_ref[b:b + 1, hs] = r[8:9] + qk_ref[b:b + 1, hs] * ub
            kdiag = jnp.where(eye, jnp.broadcast_to(k_ref[b:b + 1, hs], (dk, dk)), 0.0)
            so_ref[b, h] = s * eg_ref[b:b + 1, hs] + _dot(kdiag, jnp.broadcast_to(u, (dk, dk)))


def _sample_tail_kernel(x_ref, o_ref, z_ref, onw_ref, wout_ref, n2_ref,
                        wg_ref, wv_ref, cwg_ref, cwv_ref, cbg_ref, cbv_ref, wdn_ref, stg_ref, stv_ref, fn_ref,
                        yo_ref, stog_ref, stov_ref, xn_s, acc_s):
    c = pl.program_id(0)
    dk = DN_HEAD_DIM

    @pl.when(c == 0)
    def _():
        outs = []
        for h in range(o_ref.shape[1] // dk):
            outs.append(_rms(o_ref[:, h * dk:(h + 1) * dk], onw_ref[...]))
        o = jnp.concatenate(outs, axis=1) * _silu(z_ref[...])
        x = x_ref[...] + _dot(o, wout_ref[...])
        acc_s[...] = x
        xn_s[...] = _rms(x, n2_ref[...]).astype(BF16)

    acc_s[...] += _ffn_sample_chunk(xn_s, wg_ref, wv_ref, cwg_ref, cwv_ref, cbg_ref, cbv_ref, wdn_ref,
                                    stg_ref, stv_ref, stog_ref, stov_ref)

    @pl.when(c == pl.num_programs(0) - 1)
    def _():
        yo_ref[...] = _rms(acc_s[...], fn_ref[...])


def _params(*sem):
    return pltpu.CompilerParams(dimension_semantics=sem, vmem_limit_bytes=VMEM_LIMIT)


def kernel(x_prompt, x_sample, state_pool_buf, state_dn_conv, state_dn_ssm, state_ffn_conv, norm1_w, norm2_w,
           final_norm_w, pool_w, pool_scale, dn_w_in, dn_conv_w, dn_a_log, dn_dt_bias, dn_o_norm_w, dn_w_out,
           ffn_w_up, ffn_conv_w, ffn_conv_b, ffn_w_down):
    bp, seq, d = x_prompt.shape
    bs = x_sample.shape[0]
    nbuf = state_pool_buf.shape[2]
    dff = ffn_w_down.shape[1]
    nh = dn_a_log.shape[1]
    dk = DN_HEAD_DIM
    dn = nh * dk
    gd = pool_w.shape[2]
    tm = PROMPT_TILE
    nt = seq // tm
    nfc = dff // FFN_CHUNK
    assert seq % tm == 0 and dff % FFN_CHUNK == 0 and (3 * dn) % QKV_CHUNK == 0 and dk == 128
    assert nbuf + 1 == max(POOL_WINDOWS) and 2 * nh <= 128 and bs % SAMPLE_BLOCK == 0

    row = lambda v: v.reshape(1, -1)
    wup = ffn_w_up.astype(BF16)
    wdn = ffn_w_down.astype(BF16)
    pw = pool_w.astype(BF16)
    w_in = dn_w_in[0]
    win_all = dn_w_in.astype(BF16)
    wab = jnp.pad(w_in[:, 4 * dn:], ((0, 0), (0, 128 - 2 * nh))).astype(BF16)
    wout = dn_w_out[0].astype(BF16)
    alog = jnp.pad(dn_a_log[0], (0, 128 - nh)).reshape(1, 128)
    dtb = jnp.pad(dn_dt_bias[0], (0, 128 - nh)).reshape(1, 128)
    onw = row(dn_o_norm_w[0])
    lane_head = jnp.arange(dn) // dk
    selg = (jnp.arange(128)[:, None] == lane_head[None, :]).astype(BF16)
    selb = (jnp.arange(128)[:, None] == (lane_head[None, :] + nh)).astype(BF16)
    ti = jnp.arange(tm)
    ltri = ((ti[:, None] // DN_CHUNK == ti[None, :] // DN_CHUNK) & (ti[:, None] >= ti[None, :])).astype(BF16)

    tf = FFN_TILE
    assert seq % tf == 0
    x_spec = pl.BlockSpec((1, tf, d), lambda b, t: (b, t, 0))
    layer_resident = lambda shape, layer: pl.BlockSpec((None,) + shape, lambda *_: (layer,) + (0,) * len(shape),
                                                       pipeline_mode=pl.Buffered(1))
    ffn_w_specs = lambda layer: [layer_resident((d, 2 * dff), layer), _resident((3, 2 * dff)),
                                 _resident((1, 2 * dff)), layer_resident((dff, d), layer)]
    fst_spec = pl.BlockSpec((1, CONV_HALO, 2 * dff), lambda b, t: (b, 0, 0))
    fst_shape = jax.ShapeDtypeStruct((bp, CONV_HALO, 2 * dff), F32)

    x2, pool_tail, fst0 = pl.pallas_call(
        _prompt_layer0_kernel,
        grid=(bp, seq // tf),
        in_specs=[x_spec, _resident((1, d)), _resident(pw.shape[1:]), _resident((1, d)), _resident((1, d))]
        + ffn_w_specs(0),
        out_specs=[x_spec, pl.BlockSpec((1, POOL_HALO, d), lambda b, t: (b, 0, 0)), fst_spec],
        out_shape=[jax.ShapeDtypeStruct((bp, seq, d), F32), jax.ShapeDtypeStruct((bp, POOL_HALO, d), F32), fst_shape],
        scratch_shapes=[pltpu.VMEM((POOL_HALO + tf, d), F32), pltpu.VMEM((CONV_HALO, 2 * dff), F32)],
        compiler_params=_params("arbitrary", "arbitrary"),
        name="prompt_layer0",
    )(x_prompt, row(norm1_w[0]), pw[0], row(pool_scale[0]), row(norm2_w[0]),
      wup, ffn_conv_w[0], row(ffn_conv_b[0]), wdn)

    ntiles = bp * nt
    front_tile = lambda i: jnp.minimum(i, ntiles - 1)
    back_tile = lambda i: jnp.maximum(i - 1, 0)
    x_front = pl.BlockSpec((1, tm, d), lambda i: (front_tile(i) // nt, front_tile(i) % nt, 0))
    x_back = pl.BlockSpec((1, tm, d), lambda i: (back_tile(i) // nt, back_tile(i) % nt, 0))
    x3, cst, ssm = pl.pallas_call(
        functools.partial(_prompt_deltanet_kernel, tiles_per_seq=nt),
        grid=(ntiles + 1,),
        in_specs=[x_front, x_back,
                  _resident((1, d)), layer_resident((d, win_all.shape[2]), 0), _resident((d, 128)),
                  _resident((4, 3 * dn)), _resident((1, 128)), _resident((1, 128)), _resident((1, dk)),
                  _resident((dn, d)), _resident((tm, tm))],
        out_specs=[x_back,
                   pl.BlockSpec((1, CONV_HALO, 3 * dn), lambda i: (front_tile(i) // nt, 0, 0)),
                   pl.BlockSpec((1, nh, dk, dk), lambda i: (back_tile(i) // nt, 0, 0, 0))],
        out_shape=[jax.ShapeDtypeStruct((bp, seq, d), F32),
                   jax.ShapeDtypeStruct((bp, CONV_HALO, 3 * dn), F32),
                   jax.ShapeDtypeStruct((bp, nh, dk, dk), F32)],
        scratch_shapes=[pltpu.VMEM((CONV_HALO, 3 * dn), F32), pltpu.VMEM((dk, dn), F32),
                        pltpu.VMEM((tm, 3 * dn), F32)]
        + [pltpu.VMEM((2, tm, dn), BF16)] * 5 + [pltpu.VMEM((2, tm, dn), F32)] * 4
        + [pltpu.VMEM((tm, dn), F32), pltpu.VMEM((tm, dn), BF16),
           pltpu.VMEM((tm // DN_CHUNK, nh // HEAD_GROUP, DN_CHUNK, HEAD_GROUP * DN_CHUNK), BF16),
           pltpu.VMEM((tm, dn), F32)],
        compiler_params=_params("arbitrary"),
        name="prompt_deltanet",
    )(x2, x2, row(norm1_w[1]), win_all, wab, dn_conv_w[0], alog, dtb, onw, wout, ltri)

    y_prompt, fst1 = pl.pallas_call(
        _prompt_ffn_final_kernel,
        grid=(bp, seq // tf),
        in_specs=[x_spec, _resident((1, d))] + ffn_w_specs(1) + [_resident((1, d))],
        out_specs=[x_spec, fst_spec],
        out_shape=[jax.ShapeDtypeStruct((bp, seq, d), F32), fst_shape],
        scratch_shapes=[pltpu.VMEM((CONV_HALO, 2 * dff), F32)],
        compiler_params=_params("arbitrary", "arbitrary"),
        name="prompt_ffn_final",
    )(x3, row(norm2_w[1]), wup, ffn_conv_w[1], row(ffn_conv_b[1]), wdn, row(final_norm_w))

    xs = x_sample[:, 0, :]
    pb = jnp.transpose(state_pool_buf[0], (1, 0, 2))
    cs = jnp.transpose(state_dn_conv[0], (1, 0, 2))
    fs = jnp.transpose(state_ffn_conv, (0, 2, 1, 3))
    sc = SAMPLE_FFN_CHUNK
    nsc = dff // sc
    assert dff % sc == 0

    def ffn_chunk_specs(layer):
        specs = []
        for shape in ((None, d, sc), (3, sc), (1, sc)):
            for half in range(2):
                if shape[0] is None:
                    specs.append(pl.BlockSpec(shape, lambda c, half=half: (layer, 0, half * nsc + c)))
                else:
                    specs.append(pl.BlockSpec(shape, lambda c, half=half: (0, half * nsc + c)))
        specs.append(pl.BlockSpec((None, sc, d), lambda c: (layer, c, 0)))
        for half in range(2):
            specs.append(pl.BlockSpec((None, 2, bs, sc), lambda c, half=half: (layer, 0, 0, half * nsc + c)))
        return specs

    h_specs = [pl.BlockSpec((bs, sc), lambda c: (0, c))] * 2
    h_shapes = [jax.ShapeDtypeStruct((bs, dff), F32)] * 2

    def ffn_chunk_args(layer):
        cb = row(ffn_conv_b[layer])
        return (wup, wup, ffn_conv_w[layer], ffn_conv_w[layer], cb, cb, wdn, fs, fs)

    xs2, pool_row, hg0, hv0 = pl.pallas_call(
        _sample_layer0_kernel,
        grid=(nsc,),
        in_specs=[_resident((bs, d)), _resident((nbuf, bs, d)), _resident((1, d)), _resident(pw.shape[1:]),
                  _resident((1, d)), _resident((1, d))] + ffn_chunk_specs(0),
        out_specs=[_full((bs, d)), _full((bs, d))] + h_specs,
        out_shape=[jax.ShapeDtypeStruct((bs, d), F32), jax.ShapeDtypeStruct((bs, d), F32)] + h_shapes,
        scratch_shapes=[pltpu.VMEM((bs, d), BF16)],
        compiler_params=_params("arbitrary"),
        name="sample_layer0",
    )(xs, pb, row(norm1_w[0]), pw[0], row(pool_scale[0]), row(norm2_w[0]), *ffn_chunk_args(0))

    vec = jax.ShapeDtypeStruct((bs, dn), F32)
    w_s, qd_s, k_s, vb_s, qk_s, eg_s, z_s, conv_row = pl.pallas_call(
        _sample_dn_proj_kernel,
        out_shape=[vec] * 7 + [jax.ShapeDtypeStruct((bs, 3 * dn), F32)],
        compiler_params=pltpu.CompilerParams(vmem_limit_bytes=VMEM_LIMIT),
        name="sample_dn_proj",
    )(xs2, row(norm1_w[1]), win_all[0], wab, dn_conv_w[0], alog, dtb, selg, selb, cs)

    nb = SAMPLE_BLOCK
    vspec = pl.BlockSpec((nb, dn), lambda i: (i, 0))
    sspec = pl.BlockSpec((nb, nh, dk, dk), lambda i: (i, 0, 0, 0))
    o_s, ssm_s = pl.pallas_call(
        _sample_dn_state_kernel,
        grid=(bs // nb,),
        in_specs=[vspec] * 6 + [sspec],
        out_specs=[vspec, sspec],
        out_shape=[vec, jax.ShapeDtypeStruct((bs, nh, dk, dk), F32)],
        compiler_params=_params("arbitrary"),
        name="sample_dn_state",
    )(w_s, qd_s, k_s, vb_s, qk_s, eg_s, state_dn_ssm[0])

    ys, hg1, hv1 = pl.pallas_call(
        _sample_tail_kernel,
        grid=(nsc,),
        in_specs=[_resident((bs, d)), _resident((bs, dn)), _resident((bs, dn)), _resident((1, dk)),
                  _resident((dn, d)), _resident((1, d))] + ffn_chunk_specs(1) + [_resident((1, d))],
        out_specs=[_full((bs, d))] + h_specs,
        out_shape=[jax.ShapeDtypeStruct((bs, d), F32)] + h_shapes,
        scratch_shapes=[pltpu.VMEM((bs, d), BF16), pltpu.VMEM((bs, d), F32)],
        compiler_params=_params("arbitrary"),
        name="sample_tail",
    )(xs2, o_s, z_s, onw, wout, row(norm2_w[1]), *ffn_chunk_args(1), row(final_norm_w))

    shifted = lambda state, new_row: jnp.concatenate([state[:, 1:], new_row[:, None]], axis=1)
    ffn_rows = jnp.stack([jnp.concatenate([hg0, hv0], axis=-1), jnp.concatenate([hg1, hv1], axis=-1)])
    return (
        y_prompt,
        ys[:, None, :],
        pool_tail[None, :, POOL_HALO - nbuf:, :],
        shifted(state_pool_buf[0], pool_row)[None],
        cst[None, :, CONV_HALO - 3:, :],
        shifted(state_dn_conv[0], conv_row)[None],
        ssm[None],
        ssm_s[None],
        jnp.stack([fst0[:, CONV_HALO - 2:, :], fst1[:, CONV_HALO - 2:, :]]),
        jnp.concatenate([state_ffn_conv[:, :, 1:], ffn_rows[:, :, None]], axis=2),
    )
```

```python
import functools

import jax
import jax.numpy as jnp
from jax import lax
from jax.experimental import pallas as pl
from jax.experimental.pallas import tpu as pltpu

F32 = jnp.float32
BF16 = jnp.bfloat16

NORM_EPS = 1e-6
POOL_WINDOWS = (2, 4, 8, 16)
POOL_HALO = 16
CONV_HALO = 8
DN_HEAD_DIM = 128
DN_CHUNK = 64
FFN_CHUNK = 256
FFN_DOWN_GROUP = 4
FFN_LOOKAHEAD = 3
QKV_CHUNK = 256
OUT_CHUNK = 256
PROMPT_TILE = 256
FFN_TILE = 512
HEAD_GROUP = 4
SAMPLE_FFN_CHUNK = 1408
SAMPLE_BLOCK = 8
VMEM_LIMIT = 56 * 1024 * 1024


def _dot(a, b):
    return jnp.dot(a.astype(BF16), b.astype(BF16), preferred_element_type=F32)


def _dot_nt(a, b):
    return lax.dot_general(a.astype(BF16), b.astype(BF16), (((1,), (1,)), ((), ())),
                           preferred_element_type=F32)


def _split3(x):
    hi = x.astype(BF16)
    r = x - hi.astype(F32)
    mid = r.astype(BF16)
    lo = (r - mid.astype(F32)).astype(BF16)
    return hi, mid, lo


def _dot_exact_rhs(x, e):
    hi, mid, lo = _split3(x)
    return (jnp.dot(lo, e, preferred_element_type=F32) + jnp.dot(mid, e, preferred_element_type=F32)
            + jnp.dot(hi, e, preferred_element_type=F32))


def _dot_exact_lhs(e, x):
    hi, mid, lo = _split3(x)
    return (jnp.dot(e, lo, preferred_element_type=F32) + jnp.dot(e, mid, preferred_element_type=F32)
            + jnp.dot(e, hi, preferred_element_type=F32))


def _dot3(a, b):
    a_hi = a.astype(BF16)
    a_lo = (a - a_hi.astype(F32)).astype(BF16)
    b_hi = b.astype(BF16)
    b_lo = (b - b_hi.astype(F32)).astype(BF16)
    return (jnp.dot(a_lo, b_hi, preferred_element_type=F32) + jnp.dot(a_hi, b_lo, preferred_element_type=F32)
            + jnp.dot(a_hi, b_hi, preferred_element_type=F32))


def _rms(x, w):
    return x * lax.rsqrt(jnp.mean(x * x, axis=-1, keepdims=True) + NORM_EPS) * w


def _silu(x):
    return x / (1.0 + jnp.exp(-x))


def _softplus(x):
    return jnp.maximum(x, 0.0) + jnp.log1p(jnp.exp(-jnp.abs(x)))


def _full(shape):
    return pl.BlockSpec(shape, lambda *_: (0,) * len(shape))


def _resident(shape):
    return pl.BlockSpec(shape, lambda *_: (0,) * len(shape), pipeline_mode=pl.Buffered(1))


def _ffn_tile(xn_b, wup_ref, cw_ref, cb_ref, wdn_ref, carry_ref):
    tm = xn_b.shape[0]
    dff = wdn_ref.shape[0]
    nchunk = dff // FFN_CHUNK

    def up(c):
        return [jnp.dot(xn_b, wup_ref[:, half * dff + c * FFN_CHUNK:half * dff + (c + 1) * FFN_CHUNK],
                        preferred_element_type=F32) for half in range(2)]

    def conv_act(c, hs):
        conv = []
        for half, h in enumerate(hs):
            cols = slice(half * dff + c * FFN_CHUNK, half * dff + (c + 1) * FFN_CHUNK)
            ext = jnp.concatenate([carry_ref[:, cols], h], axis=0)
            y = (ext * cw_ref[2:3, cols] + pltpu.roll(ext, 1, 0) * cw_ref[1:2, cols]
                 + pltpu.roll(ext, 2, 0) * cw_ref[0:1, cols])
            conv.append(y[CONV_HALO:] + cb_ref[:, cols])
            carry_ref[:, cols] = h[tm - CONV_HALO:]
        return (_silu(conv[0]) * conv[1]).astype(BF16)

    acc = None
    acts = []
    ahead = [up(c) for c in range(min(FFN_LOOKAHEAD, nchunk))]
    for c in range(nchunk):
        cur = ahead.pop(0)
        if c + FFN_LOOKAHEAD < nchunk:
            ahead.append(up(c + FFN_LOOKAHEAD))
        acts.append(conv_act(c, cur))
        if len(acts) == FFN_DOWN_GROUP or c + 1 == nchunk:
            c0 = c + 1 - len(acts)
            a = acts[0] if len(acts) == 1 else jnp.concatenate(acts, axis=1)
            d = jnp.dot(a, wdn_ref[c0 * FFN_CHUNK:(c + 1) * FFN_CHUNK, :], preferred_element_type=F32)
            acc = d if acc is None else acc + d
            acts = []
    return acc


def _prompt_layer0_kernel(x_ref, n1_ref, pw_ref, ps_ref, n2_ref, wup_ref, cw_ref, cb_ref, wdn_ref,
                          xo_ref, pool_ref, fst_ref, hbuf, carry):
    t = pl.program_id(1)
    tm = x_ref.shape[1]
    gd = pw_ref.shape[1]

    @pl.when(t == 0)
    def _():
        hbuf[0:POOL_HALO, :] = jnp.zeros((POOL_HALO, hbuf.shape[1]), F32)
        carry[...] = jnp.zeros(carry.shape, F32)

    x = x_ref[0]
    h = _rms(x, n1_ref[...])
    hbuf[POOL_HALO:POOL_HALO + tm, :] = h
    pos1 = t * tm + lax.broadcasted_iota(jnp.int32, (tm, 1), 0) + 1
    parts = []
    for g, w in enumerate(POOL_WINDOWS):
        cols = slice(g * gd, (g + 1) * gd)
        s = hbuf[:, cols]
        sh = 1
        while sh < w:
            s = s + pltpu.roll(s, sh, 0)
            sh *= 2
        cnt = jnp.minimum(w, pos1).astype(F32)
        dg = s[POOL_HALO:] / cnt - h[:, cols]
        parts.append(x[:, cols] + _dot(dg, pw_ref[g]) * ps_ref[:, cols])
    x1 = jnp.concatenate(parts, axis=1)
    tail = hbuf[tm:tm + POOL_HALO, :]
    pool_ref[0] = tail
    hbuf[0:POOL_HALO, :] = tail

    xn = _rms(x1, n2_ref[...]).astype(BF16)
    xo_ref[0] = x1 + _ffn_tile(xn, wup_ref, cw_ref, cb_ref, wdn_ref, carry)
    fst_ref[0] = carry[...]


def _prompt_ffn_final_kernel(x_ref, n2_ref, wup_ref, cw_ref, cb_ref, wdn_ref, fn_ref,
                             yo_ref, fst_ref, carry):
    @pl.when(pl.program_id(1) == 0)
    def _():
        carry[...] = jnp.zeros(carry.shape, F32)

    x = x_ref[0]
    xn = _rms(x, n2_ref[...]).astype(BF16)
    x = x + _ffn_tile(xn, wup_ref, cw_ref, cb_ref, wdn_ref, carry)
    yo_ref[0] = _rms(x, fn_ref[...])
    fst_ref[0] = carry[...]


def _interleave(streams):
    merged = []
    for si, steps in enumerate(streams):
        total = float(sum(w for w, _ in steps))
        done = 0.0
        for w, thunk in steps:
            merged.append(((done + 0.5 * w) / total, si, len(merged), thunk))
            done += w
    for _, _, _, thunk in sorted(merged, key=lambda m: m[:3]):
        thunk()


def _prompt_deltanet_kernel(xa_ref, xb_ref, n1_ref, win_ref, wab_ref, cw_ref, alog_ref, dtb_ref, onw_ref,
                            wout_ref, ltri_ref,
                            xo_ref, cst_ref, ssm_ref,
                            carry, s_ref, qkv_s, kb_s, q_s, k_s, qd_s, kd_s, vb_s, wr_s, gcol_s, z_s, uin_s, win_s,
                            qk_s, o_s, og_s, *, tiles_per_seq):
    i = pl.program_id(0)
    tm = xa_ref.shape[1]
    dk = DN_HEAD_DIM
    dn = s_ref.shape[1]
    nh = dn // dk
    ch = DN_CHUNK
    nch = tm // ch
    gh = HEAD_GROUP
    gw = gh * dk
    pw = gh * ch
    d_model = xa_ref.shape[2]
    wr_slot = i % 2
    rd_slot = 1 - wr_slot

    @pl.when(i == 0)
    def _():
        for ref in (kb_s, q_s, k_s, qd_s, kd_s, vb_s, wr_s, gcol_s, z_s):
            ref[1] = jnp.zeros(ref.shape[1:], ref.dtype)
        s_ref[...] = jnp.zeros(s_ref.shape, F32)

    @pl.when(i % tiles_per_seq == 0)
    def _():
        carry[...] = jnp.zeros(carry.shape, F32)

    @pl.when((i + tiles_per_seq - 1) % tiles_per_seq == 0)
    def _():
        s_ref[...] = jnp.zeros(s_ref.shape, F32)

    front, back = [], []
    env = {}

    def f_norm():
        env["xb"] = _rms(xa_ref[0], n1_ref[...]).astype(BF16)

    def f_qkv(c):
        cols = slice(c * QKV_CHUNK, (c + 1) * QKV_CHUNK)
        pre = jnp.dot(env["xb"], win_ref[:, cols], preferred_element_type=F32)
        ext = jnp.concatenate([carry[:, cols], pre], axis=0)
        prev = pltpu.roll(ext, 1, 0)
        y = (ext * cw_ref[3:4, cols] + prev * cw_ref[2:3, cols]
             + pltpu.roll(ext * cw_ref[1:2, cols] + prev * cw_ref[0:1, cols], 2, 0))
        qkv_s[:, cols] = _silu(y[CONV_HALO:])
        carry[:, cols] = pre[tm - CONV_HALO:]
        cst_ref[0, :, cols] = pre[tm - CONV_HALO:]

    def f_gates():
        ab = jnp.dot(env["xb"], wab_ref[...], preferred_element_type=F32)
        g = -jnp.exp(alog_ref[...]) * _softplus(ab + dtb_ref[...])
        env["gc"] = _dot_exact_lhs(ltri_ref[...], g)
        env["sig"] = 1.0 / (1.0 + jnp.exp(-ab))

    def f_z(j):
        cols = slice(j * OUT_CHUNK, (j + 1) * OUT_CHUNK)
        z = jnp.dot(env["xb"], win_ref[:, 3 * dn + cols.start:3 * dn + cols.stop], preferred_element_type=F32)
        z_s[wr_slot, :, cols] = _silu(z)

    def f_head(h):
        hs = slice(h * dk, (h + 1) * dk)
        gcol = jnp.broadcast_to(env["gc"][:, h:h + 1], (tm, dk))
        beta = jnp.broadcast_to(env["sig"][:, nh + h:nh + h + 1], (tm, dk))
        glast = jnp.concatenate(
            [jnp.broadcast_to(gcol[(c + 1) * ch - 1:(c + 1) * ch, :], (ch, dk)) for c in range(nch)], axis=0)
        egc = jnp.exp(gcol)
        qh = qkv_s[:, hs]
        kh = qkv_s[:, dn + h * dk:dn + (h + 1) * dk]
        vh = qkv_s[:, 2 * dn + h * dk:2 * dn + (h + 1) * dk]
        qh = qh * (lax.rsqrt(jnp.sum(qh * qh, axis=-1, keepdims=True) + NORM_EPS) * (dk ** -0.5))
        kh = kh * lax.rsqrt(jnp.sum(kh * kh, axis=-1, keepdims=True) + NORM_EPS)
        kb = kh * beta
        gcol_s[wr_slot, :, hs] = gcol
        q_s[wr_slot, :, hs] = qh.astype(BF16)
        k_s[wr_slot, :, hs] = kh.astype(BF16)
        kb_s[wr_slot, :, hs] = kb.astype(BF16)
        qd_s[wr_slot, :, hs] = (qh * egc).astype(BF16)
        kd_s[wr_slot, :, hs] = (kh * jnp.exp(glast - gcol)).astype(BF16)
        vb_s[wr_slot, :, hs] = vh * beta
        wr_s[wr_slot, :, hs] = kb * egc

    front.append((2, f_norm))
    per_part = dn // QKV_CHUNK
    for c in [part * per_part + j for j in range(per_part) for part in range(3)]:
        front.append((20.0 * QKV_CHUNK / 1024, functools.partial(f_qkv, c)))
    front.append((3, f_gates))
    for h in range(nh):
        front.append((3, functools.partial(f_head, h)))
    for j in range(dn // OUT_CHUNK):
        front.append((5.0 * OUT_CHUNK / dn, functools.partial(f_z, j)))

    def iota2(shape, d):
        return lax.broadcasted_iota(jnp.int32, shape, d)

    pi, pj = iota2((ch, pw), 0), iota2((ch, pw), 1) % ch
    eye_p = pi == pj
    incl_p = pi >= pj
    strict_p = pi > pj
    eye_f = eye_p.astype(F32)
    head_of_lane = iota2((ch, pw), 1) // ch
    bd_p = (iota2((pw, pw), 0) // ch) == (iota2((pw, pw), 1) // ch)
    bd_k = (iota2((2 * ch, 2 * dk), 0) // ch) == (iota2((2 * ch, 2 * dk), 1) // dk)
    bd_u = (iota2((pw, gw), 0) // ch) == (iota2((pw, gw), 1) // dk)
    bd_s = (iota2((2 * dk, 2 * dk), 0) // dk) == (iota2((2 * dk, 2 * dk), 1) // dk)

    def expand(mc):
        return jnp.where(bd_p, jnp.concatenate([mc] * gh, axis=0), 0.0).astype(BF16)

    units = [(c, gi) for c in range(nch) for gi in range(nh // gh)]
    n_cs, m_cs, p_cs = [], [None] * len(units), [None] * len(units)

    def b_scores(c, gi):
        rows = slice(c * ch, (c + 1) * ch)
        h0 = gi * gh
        sc = []
        for p in range(gh // 2):
            ps = slice((h0 + 2 * p) * dk, (h0 + 2 * p + 2) * dk)
            kc = k_s[rd_slot, rows, ps]
            kbd = jnp.where(bd_k, jnp.concatenate([kc, kc], axis=0), jnp.zeros((), BF16))
            sc.append(lax.dot_general(jnp.concatenate([kb_s[rd_slot, rows, ps], q_s[rd_slot, rows, ps]], axis=0),
                                      kbd, (((1,), (1,)), ((), ())), preferred_element_type=F32))
        sc = jnp.concatenate(sc, axis=1)
        gcol_p = jnp.broadcast_to(gcol_s[rd_slot, rows, h0 * dk:h0 * dk + 1], (ch, pw))
        for hl in range(1, gh):
            gcol_p = jnp.where(
                head_of_lane == hl,
                jnp.broadcast_to(gcol_s[rd_slot, rows, (h0 + hl) * dk:(h0 + hl) * dk + 1], (ch, pw)), gcol_p)
        grow_p = jnp.sum(jnp.where(eye_p, gcol_p, 0.0), axis=0, keepdims=True)
        decay = jnp.exp(jnp.where(incl_p, gcol_p - grow_p, -jnp.inf))
        n_cs.append(jnp.where(strict_p, sc[0:ch] * decay, 0.0))
        qk_s[c, gi] = (sc[ch:2 * ch] * decay).astype(BF16)

    def b_square(u):
        m_cs[u] = jnp.dot(n_cs[u].astype(BF16), expand(n_cs[u]), preferred_element_type=F32)
        p_cs[u] = eye_f - n_cs[u]

    def b_level(u, last):
        m_bd = expand(m_cs[u])
        if last:
            p_cs[u] = p_cs[u] + jnp.dot(p_cs[u].astype(BF16), m_bd, preferred_element_type=F32)
        else:
            both = jnp.dot(jnp.concatenate([m_cs[u], p_cs[u]], axis=0).astype(BF16), m_bd,
                           preferred_element_type=F32)
            m_cs[u] = both[0:ch]
            p_cs[u] = p_cs[u] + both[ch:2 * ch]

    def b_apply(u):
        c, gi = units[u]
        rows = slice(c * ch, (c + 1) * ch)
        h0 = gi * gh
        rhs = jnp.concatenate(
            [jnp.concatenate([vb_s[rd_slot, rows, (h0 + hl) * dk:(h0 + hl + 1) * dk],
                              wr_s[rd_slot, rows, (h0 + hl) * dk:(h0 + hl + 1) * dk]], axis=1)
             for hl in range(gh)], axis=0)
        uw = rhs + jnp.dot(expand(p_cs[u] - eye_f), rhs.astype(BF16), preferred_element_type=F32)
        for hl in range(gh):
            hs = slice((h0 + hl) * dk, (h0 + hl + 1) * dk)
            uin_s[rows, hs] = uw[hl * ch:(hl + 1) * ch, 0:dk]
            win_s[rows, hs] = uw[hl * ch:(hl + 1) * ch, dk:2 * dk].astype(BF16)

    def b_recur(c, gi):
        rows = slice(c * ch, (c + 1) * ch)
        last = (c + 1) * ch - 1
        gs = slice(gi * gw, (gi + 1) * gw)
        u_parts, qs_parts = [], []
        for p in range(gh // 2):
            ps = slice(gi * gw + 2 * p * dk, gi * gw + (2 * p + 2) * dk)
            s_pair = s_ref[:, ps]
            s_bd = jnp.where(bd_s, jnp.concatenate([s_pair, s_pair], axis=0), 0.0).astype(BF16)
            r = jnp.dot(jnp.concatenate([win_s[rows, ps], qd_s[rd_slot, rows, ps]], axis=0), s_bd,
                        preferred_element_type=F32)
            u_parts.append(uin_s[rows, ps] - r[0:ch])
            qs_parts.append(r[ch:2 * ch])
        u_g = jnp.concatenate(u_parts, axis=1)
        u_bd = jnp.where(bd_u, jnp.concatenate([u_g] * gh, axis=0), 0.0).astype(BF16)
        o_s[rows, gs] = jnp.concatenate(qs_parts, axis=1) + jnp.dot(qk_s[c, gi], u_bd,
                                                                      preferred_element_type=F32)
        kd_stack = jnp.concatenate(
            [kd_s[rd_slot, rows, gi * gw + hl * dk:gi * gw + (hl + 1) * dk] for hl in range(gh)], axis=0)
        s_ref[:, gs] = (s_ref[:, gs] * jnp.exp(gcol_s[rd_slot, last:last + 1, gs])
                        + lax.dot_general(kd_stack, u_bd, (((0,), (0,)), ((), ())),
                                          preferred_element_type=F32))

    def b_gate(h):
        hs = slice(h * dk, (h + 1) * dk)
        og_s[:, hs] = (_rms(o_s[:, hs], onw_ref[...]) * z_s[rd_slot, :, hs]).astype(BF16)

    def b_proj(j):
        cols = slice(j * OUT_CHUNK, (j + 1) * OUT_CHUNK)
        xo_ref[0, :, cols] = xb_ref[0, :, cols] + jnp.dot(og_s[...], wout_ref[:, cols],
                                                         preferred_element_type=F32)

    def b_state():
        for h in range(nh):
            ssm_ref[0, h] = s_ref[:, h * dk:(h + 1) * dk]

    for c, gi in units:
        back.append((2, functools.partial(b_scores, c, gi)))
    for u in range(len(units)):
        back.append((1, functools.partial(b_square, u)))
    sq = 2
    while sq < ch:
        for u in range(len(units)):
            back.append((1, functools.partial(b_level, u, 2 * sq >= ch)))
        sq *= 2
    for u in range(len(units)):
        back.append((1.5, functools.partial(b_apply, u)))
    for c in range(nch):
        for gi in range(nh // gh):
            back.append((3, functools.partial(b_recur, c, gi)))
    for h in range(nh):
        back.append((0.5, functools.partial(b_gate, h)))
    for j in range(d_model // OUT_CHUNK):
        back.append((1, functools.partial(b_proj, j)))
    back.append((0.5, b_state))

    _interleave([front, back])


def _ffn_sample_chunk(xn_ref, wg_ref, wv_ref, cwg_ref, cwv_ref, cbg_ref, cbv_ref, wdn_ref, stg_ref, stv_ref,
                      sto_ref):
    sc = wg_ref.shape[1]
    dff = sto_ref.shape[2] // 2
    conv = []
    for half, (w_ref, cw_ref, cb_ref, st_ref) in enumerate(((wg_ref, cwg_ref, cbg_ref, stg_ref),
                                                            (wv_ref, cwv_ref, cbv_ref, stv_ref))):
        h = jnp.dot(xn_ref[...], w_ref[...], preferred_element_type=F32)
        prev = st_ref[:, 1, :]
        conv.append(st_ref[:, 0, :] * cw_ref[0:1, :] + prev * cw_ref[1:2, :] + h * cw_ref[2:3, :] + cb_ref[...])
        for cc in range(dff // sc):
            @pl.when(pl.program_id(0) == cc)
            def _(half=half, cc=cc, prev=prev, h=h):
                cols = slice(half * dff + cc * sc, half * dff + (cc + 1) * sc)
                sto_ref[:, 0, cols] = prev
                sto_ref[:, 1, cols] = h
    a = (_silu(conv[0]) * conv[1]).astype(BF16)
    return jnp.dot(a, wdn_ref[...], preferred_element_type=F32)


def _sample_layer0_kernel(x_ref, pb_ref, n1_ref, pw_ref, ps_ref, n2_ref,
                          wg_ref, wv_ref, cwg_ref, cwv_ref, cbg_ref, cbv_ref, wdn_ref, stg_ref, stv_ref,
                          xo_ref, pbo_ref, sto_ref, xn_s):
    c = pl.program_id(0)
    nbuf = pb_ref.shape[1]
    gd = pw_ref.shape[1]

    @pl.when(c == 0)
    def _():
        x = x_ref[...]
        h = _rms(x, n1_ref[...])
        parts = []
        for g, w in enumerate(POOL_WINDOWS):
            cols = slice(g * gd, (g + 1) * gd)
            s = h[:, cols]
            for j in range(1, w):
                s = s + pb_ref[:, nbuf - j, cols]
            dg = s / float(w) - h[:, cols]
            parts.append(x[:, cols] + _dot(dg, pw_ref[g]) * ps_ref[:, cols])
        x1 = jnp.concatenate(parts, axis=1)
        for j in range(nbuf - 1):
            pbo_ref[:, j, :] = pb_ref[:, j + 1, :]
        pbo_ref[:, nbuf - 1, :] = h
        xn_s[...] = _rms(x1, n2_ref[...]).astype(BF16)
        xo_ref[...] = x1

    xo_ref[...] += _ffn_sample_chunk(xn_s, wg_ref, wv_ref, cwg_ref, cwv_ref, cbg_ref, cbv_ref, wdn_ref,
                                     stg_ref, stv_ref, sto_ref)


def _sample_dn_proj_kernel(x_ref, n1_ref, win_ref, wab_ref, cw_ref, alog_ref, dtb_ref,
                           selg_ref, selb_ref, cst_ref,
                           w_ref, qd_ref, k_ref, vb_ref, qk_ref, eg_ref, z_ref, csto_ref):
    dk = DN_HEAD_DIM
    dn = cw_ref.shape[1] // 3
    nh = dn // dk
    xb = _rms(x_ref[...], n1_ref[...]).astype(BF16)
    pre = jnp.dot(xb, win_ref[:, 0:3 * dn], preferred_element_type=F32)
    y = (cst_ref[:, 0, :] * cw_ref[0:1, :] + cst_ref[:, 1, :] * cw_ref[1:2, :] + cst_ref[:, 2, :] * cw_ref[2:3, :]
         + pre * cw_ref[3:4, :])
    qkv = _silu(y)
    csto_ref[:, 0, :] = cst_ref[:, 1, :]
    csto_ref[:, 1, :] = cst_ref[:, 2, :]
    csto_ref[:, 2, :] = pre
    z_ref[...] = jnp.dot(xb, win_ref[:, 3 * dn:4 * dn], preferred_element_type=F32)
    ab = jnp.dot(xb, wab_ref[...], preferred_element_type=F32)
    g = -jnp.exp(alog_ref[...]) * _softplus(ab + dtb_ref[...])
    eg = jnp.exp(_dot_exact_rhs(g, selg_ref[...]))
    betab = _dot_exact_rhs(1.0 / (1.0 + jnp.exp(-ab)), selb_ref[...])
    eg_ref[...] = eg
    for h in range(nh):
        hs = slice(h * dk, (h + 1) * dk)
        q = qkv[:, hs]
        k = qkv[:, dn + h * dk:dn + (h + 1) * dk]
        v = qkv[:, 2 * dn + h * dk:2 * dn + (h + 1) * dk]
        q = q * lax.rsqrt(jnp.sum(q * q, axis=-1, keepdims=True) + NORM_EPS) * (dk ** -0.5)
        k = k * lax.rsqrt(jnp.sum(k * k, axis=-1, keepdims=True) + NORM_EPS)
        kb = k * betab[:, hs]
        qk = jnp.sum(q.astype(BF16).astype(F32) * k.astype(BF16).astype(F32), axis=-1, keepdims=True)
        w_ref[:, hs] = kb * eg[:, hs]
        qd_ref[:, hs] = q * eg[:, hs]
        k_ref[:, hs] = k
        vb_ref[:, hs] = v * betab[:, hs]
        qk_ref[:, hs] = jnp.broadcast_to(qk, (q.shape[0], dk))


def _sample_dn_state_kernel(w_ref, qd_ref, k_ref, vb_ref, qk_ref, eg_ref, s_ref, o_ref, so_ref):
    dk = DN_HEAD_DIM
    nb, nh = s_ref.shape[0], s_ref.shape[1]
    eye = (lax.broadcasted_iota(jnp.int32, (dk, dk), 0) == lax.broadcasted_iota(jnp.int32, (dk, dk), 1))
    for b in range(nb):
        for h in range(nh):
            hs = slice(h * dk, (h + 1) * dk)
            s = s_ref[b, h]
            lhs = jnp.concatenate([jnp.broadcast_to(w_ref[b:b + 1, hs], (8, dk)),
                                   jnp.broadcast_to(qd_ref[b:b + 1, hs], (8, dk))], axis=0)
            r = _dot(lhs, s)
            u = vb_ref[b:b + 1, hs] - r[0:1]
            ub = u.astype(BF16).astype(F32)
            o_ref[b:b + 1, hs] = r[8:9] + qk_ref[b:b + 1, hs] * ub
            kdiag = jnp.where(eye, jnp.broadcast_to(k_ref[b:b + 1, hs], (dk, dk)), 0.0)
            so_ref[b, h] = s * eg_ref[b:b + 1, hs] + _dot(kdiag, jnp.broadcast_to(u, (dk, dk)))


def _sample_tail_kernel(x_ref, o_ref, z_ref, onw_ref, wout_ref, n2_ref,
                        wg_ref, wv_ref, cwg_ref, cwv_ref, cbg_ref, cbv_ref, wdn_ref, stg_ref, stv_ref, fn_ref,
                        yo_ref, sto_ref, xn_s, acc_s):
    c = pl.program_id(0)
    dk = DN_HEAD_DIM

    @pl.when(c == 0)
    def _():
        outs = []
        for h in range(o_ref.shape[1] // dk):
            outs.append(_rms(o_ref[:, h * dk:(h + 1) * dk], onw_ref[...]))
        o = jnp.concatenate(outs, axis=1) * _silu(z_ref[...])
        x = x_ref[...] + _dot(o, wout_ref[...])
        acc_s[...] = x
        xn_s[...] = _rms(x, n2_ref[...]).astype(BF16)

    acc_s[...] += _ffn_sample_chunk(xn_s, wg_ref, wv_ref, cwg_ref, cwv_ref, cbg_ref, cbv_ref, wdn_ref,
                                    stg_ref, stv_ref, sto_ref)

    @pl.when(c == pl.num_programs(0) - 1)
    def _():
        yo_ref[...] = _rms(acc_s[...], fn_ref[...])


def _params(*sem):
    return pltpu.CompilerParams(dimension_semantics=sem, vmem_limit_bytes=VMEM_LIMIT)


def kernel(x_prompt, x_sample, state_pool_buf, state_dn_conv, state_dn_ssm, state_ffn_conv, norm1_w, norm2_w,
           final_norm_w, pool_w, pool_scale, dn_w_in, dn_conv_w, dn_a_log, dn_dt_bias, dn_o_norm_w, dn_w_out,
           ffn_w_up, ffn_conv_w, ffn_conv_b, ffn_w_down):
    bp, seq, d = x_prompt.shape
    bs = x_sample.shape[0]
    nbuf = state_pool_buf.shape[2]
    dff = ffn_w_down.shape[1]
    nh = dn_a_log.shape[1]
    dk = DN_HEAD_DIM
    dn = nh * dk
    gd = pool_w.shape[2]
    tm = PROMPT_TILE
    nt = seq // tm
    nfc = dff // FFN_CHUNK
    assert seq % tm == 0 and dff % FFN_CHUNK == 0 and (3 * dn) % QKV_CHUNK == 0 and dk == 128
    assert nbuf + 1 == max(POOL_WINDOWS) and 2 * nh <= 128 and bs % SAMPLE_BLOCK == 0

    row = lambda v: v.reshape(1, -1)
    wup = ffn_w_up.astype(BF16)
    wdn = ffn_w_down.astype(BF16)
    pw = pool_w.astype(BF16)
    w_in = dn_w_in[0]
    win_all = dn_w_in.astype(BF16)
    wab = jnp.pad(w_in[:, 4 * dn:], ((0, 0), (0, 128 - 2 * nh))).astype(BF16)
    wout = dn_w_out[0].astype(BF16)
    alog = jnp.pad(dn_a_log[0], (0, 128 - nh)).reshape(1, 128)
    dtb = jnp.pad(dn_dt_bias[0], (0, 128 - nh)).reshape(1, 128)
    onw = row(dn_o_norm_w[0])
    lane_head = jnp.arange(dn) // dk
    selg = (jnp.arange(128)[:, None] == lane_head[None, :]).astype(BF16)
    selb = (jnp.arange(128)[:, None] == (lane_head[None, :] + nh)).astype(BF16)
    ti = jnp.arange(tm)
    ltri = ((ti[:, None] // DN_CHUNK == ti[None, :] // DN_CHUNK) & (ti[:, None] >= ti[None, :])).astype(BF16)

    tf = FFN_TILE
    assert seq % tf == 0
    x_spec = pl.BlockSpec((1, tf, d), lambda b, t: (b, t, 0))
    layer_resident = lambda shape, layer: pl.BlockSpec((None,) + shape, lambda *_: (layer,) + (0,) * len(shape),
                                                       pipeline_mode=pl.Buffered(1))
    ffn_w_specs = lambda layer: [layer_resident((d, 2 * dff), layer), _resident((3, 2 * dff)),
                                 _resident((1, 2 * dff)), layer_resident((dff, d), layer)]
    fst_spec = pl.BlockSpec((1, CONV_HALO, 2 * dff), lambda b, t: (b, 0, 0))
    fst_shape = jax.ShapeDtypeStruct((bp, CONV_HALO, 2 * dff), F32)

    x2, pool_tail, fst0 = pl.pallas_call(
        _prompt_layer0_kernel,
        grid=(bp, seq // tf),
        in_specs=[x_spec, _resident((1, d)), _resident(pw.shape[1:]), _resident((1, d)), _resident((1, d))]
        + ffn_w_specs(0),
        out_specs=[x_spec, pl.BlockSpec((1, POOL_HALO, d), lambda b, t: (b, 0, 0)), fst_spec],
        out_shape=[jax.ShapeDtypeStruct((bp, seq, d), F32), jax.ShapeDtypeStruct((bp, POOL_HALO, d), F32), fst_shape],
        scratch_shapes=[pltpu.VMEM((POOL_HALO + tf, d), F32), pltpu.VMEM((CONV_HALO, 2 * dff), F32)],
        compiler_params=_params("arbitrary", "arbitrary"),
        name="prompt_layer0",
    )(x_prompt, row(norm1_w[0]), pw[0], row(pool_scale[0]), row(norm2_w[0]),
      wup, ffn_conv_w[0], row(ffn_conv_b[0]), wdn)

    ntiles = bp * nt
    front_tile = lambda i: jnp.minimum(i, ntiles - 1)
    back_tile = lambda i: jnp.maximum(i - 1, 0)
    x_front = pl.BlockSpec((1, tm, d), lambda i: (front_tile(i) // nt, front_tile(i) % nt, 0))
    x_back = pl.BlockSpec((1, tm, d), lambda i: (back_tile(i) // nt, back_tile(i) % nt, 0))
    x3, cst, ssm = pl.pallas_call(
        functools.partial(_prompt_deltanet_kernel, tiles_per_seq=nt),
        grid=(ntiles + 1,),
        in_specs=[x_front, x_back,
                  _resident((1, d)), layer_resident((d, win_all.shape[2]), 0), _resident((d, 128)),
                  _resident((4, 3 * dn)), _resident((1, 128)), _resident((1, 128)), _resident((1, dk)),
                  _resident((dn, d)), _resident((tm, tm))],
        out_specs=[x_back,
                   pl.BlockSpec((1, CONV_HALO, 3 * dn), lambda i: (front_tile(i) // nt, 0, 0)),
                   pl.BlockSpec((1, nh, dk, dk), lambda i: (back_tile(i) // nt, 0, 0, 0))],
        out_shape=[jax.ShapeDtypeStruct((bp, seq, d), F32),
                   jax.ShapeDtypeStruct((bp, CONV_HALO, 3 * dn), F32),
                   jax.ShapeDtypeStruct((bp, nh, dk, dk), F32)],
        scratch_shapes=[pltpu.VMEM((CONV_HALO, 3 * dn), F32), pltpu.VMEM((dk, dn), F32),
                        pltpu.VMEM((tm, 3 * dn), F32)]
        + [pltpu.VMEM((2, tm, dn), BF16)] * 5 + [pltpu.VMEM((2, tm, dn), F32)] * 4
        + [pltpu.VMEM((tm, dn), F32), pltpu.VMEM((tm, dn), BF16),
           pltpu.VMEM((tm // DN_CHUNK, nh // HEAD_GROUP, DN_CHUNK, HEAD_GROUP * DN_CHUNK), BF16),
           pltpu.VMEM((tm, dn), F32), pltpu.VMEM((tm, dn), BF16)],
        compiler_params=_params("arbitrary"),
        name="prompt_deltanet",
    )(x2, x2, row(norm1_w[1]), win_all, wab, dn_conv_w[0], alog, dtb, onw, wout, ltri)

    y_prompt, fst1 = pl.pallas_call(
        _prompt_ffn_final_kernel,
        grid=(bp, seq // tf),
        in_specs=[x_spec, _resident((1, d))] + ffn_w_specs(1) + [_resident((1, d))],
        out_specs=[x_spec, fst_spec],
        out_shape=[jax.ShapeDtypeStruct((bp, seq, d), F32), fst_shape],
        scratch_shapes=[pltpu.VMEM((CONV_HALO, 2 * dff), F32)],
        compiler_params=_params("arbitrary", "arbitrary"),
        name="prompt_ffn_final",
    )(x3, row(norm2_w[1]), wup, ffn_conv_w[1], row(ffn_conv_b[1]), wdn, row(final_norm_w))

    xs = x_sample[:, 0, :]
    sc = SAMPLE_FFN_CHUNK
    nsc = dff // sc
    assert dff % sc == 0

    def ffn_chunk_specs(layer):
        specs = []
        for shape in ((None, d, sc), (3, sc), (1, sc)):
            for half in range(2):
                if shape[0] is None:
                    specs.append(pl.BlockSpec(shape, lambda c, half=half: (layer, 0, half * nsc + c)))
                else:
                    specs.append(pl.BlockSpec(shape, lambda c, half=half: (0, half * nsc + c)))
        specs.append(pl.BlockSpec((None, sc, d), lambda c: (layer, c, 0)))
        for half in range(2):
            specs.append(pl.BlockSpec((None, bs, 2, sc), lambda c, half=half: (layer, 0, 0, half * nsc + c)))
        return specs

    sto_spec = _full((bs, 2, 2 * dff))
    sto_shape = jax.ShapeDtypeStruct((bs, 2, 2 * dff), F32)

    def ffn_chunk_args(layer):
        cb = row(ffn_conv_b[layer])
        return (wup, wup, ffn_conv_w[layer], ffn_conv_w[layer], cb, cb, wdn, state_ffn_conv, state_ffn_conv)

    xs2, pool_s, fso0 = pl.pallas_call(
        _sample_layer0_kernel,
        grid=(nsc,),
        in_specs=[_resident((bs, d)), layer_resident((bs, nbuf, d), 0), _resident((1, d)), _resident(pw.shape[1:]),
                  _resident((1, d)), _resident((1, d))] + ffn_chunk_specs(0),
        out_specs=[_full((bs, d)), _full((bs, nbuf, d)), sto_spec],
        out_shape=[jax.ShapeDtypeStruct((bs, d), F32), jax.ShapeDtypeStruct((bs, nbuf, d), F32), sto_shape],
        scratch_shapes=[pltpu.VMEM((bs, d), BF16)],
        compiler_params=_params("arbitrary"),
        name="sample_layer0",
    )(xs, state_pool_buf, row(norm1_w[0]), pw[0], row(pool_scale[0]), row(norm2_w[0]), *ffn_chunk_args(0))

    vec = jax.ShapeDtypeStruct((bs, dn), F32)
    w_s, qd_s, k_s, vb_s, qk_s, eg_s, z_s, conv_s = pl.pallas_call(
        _sample_dn_proj_kernel,
        out_shape=[vec] * 7 + [jax.ShapeDtypeStruct(state_dn_conv.shape[1:], F32)],
        compiler_params=pltpu.CompilerParams(vmem_limit_bytes=VMEM_LIMIT),
        name="sample_dn_proj",
    )(xs2, row(norm1_w[1]), win_all[0], wab, dn_conv_w[0], alog, dtb, selg, selb, state_dn_conv[0])

    nb = SAMPLE_BLOCK
    vspec = pl.BlockSpec((nb, dn), lambda i: (i, 0))
    sspec = pl.BlockSpec((nb, nh, dk, dk), lambda i: (i, 0, 0, 0))
    o_s, ssm_s = pl.pallas_call(
        _sample_dn_state_kernel,
        grid=(bs // nb,),
        in_specs=[vspec] * 6 + [sspec],
        out_specs=[vspec, sspec],
        out_shape=[vec, jax.ShapeDtypeStruct((bs, nh, dk, dk), F32)],
        compiler_params=_params("arbitrary"),
        name="sample_dn_state",
    )(w_s, qd_s, k_s, vb_s, qk_s, eg_s, state_dn_ssm[0])

    ys, fso1 = pl.pallas_call(
        _sample_tail_kernel,
        grid=(nsc,),
        in_specs=[_resident((bs, d)), _resident((bs, dn)), _resident((bs, dn)), _resident((1, dk)),
                  _resident((dn, d)), _resident((1, d))] + ffn_chunk_specs(1) + [_resident((1, d))],
        out_specs=[_full((bs, d)), sto_spec],
        out_shape=[jax.ShapeDtypeStruct((bs, d), F32), sto_shape],
        scratch_shapes=[pltpu.VMEM((bs, d), BF16), pltpu.VMEM((bs, d), F32)],
        compiler_params=_params("arbitrary"),
        name="sample_tail",
    )(xs2, o_s, z_s, onw, wout, row(norm2_w[1]), *ffn_chunk_args(1), row(final_norm_w))

    return (
        y_prompt,
        ys[:, None, :],
        pool_tail[None, :, POOL_HALO - nbuf:, :],
        pool_s[None],
        cst[None, :, CONV_HALO - 3:, :],
        conv_s[None],
        ssm[None],
        ssm_s[None],
        jnp.stack([fst0[:, CONV_HALO - 2:, :], fst1[:, CONV_HALO - 2:, :]]),
        jnp.stack([fso0, fso1]),
    )
```

```python
import functools

import jax
import jax.numpy as jnp
from jax import lax
from jax.experimental import pallas as pl
from jax.experimental.pallas import tpu as pltpu

F32 = jnp.float32
BF16 = jnp.bfloat16

NORM_EPS = 1e-6
POOL_WINDOWS = (2, 4, 8, 16)
POOL_HALO = 16
CONV_HALO = 8
DN_HEAD_DIM = 128
DN_CHUNK = 64
FFN_CHUNK = 256
FFN_DOWN_GROUP = 4
FFN_LOOKAHEAD = 3
QKV_CHUNK = 256
OUT_CHUNK = 256
PROMPT_TILE = 256
FFN_TILE = 1024
HEAD_GROUP = 4
SAMPLE_FFN_CHUNK = 1408
SAMPLE_BLOCK = 8
VMEM_LIMIT = 56 * 1024 * 1024


def _dot(a, b):
    return jnp.dot(a.astype(BF16), b.astype(BF16), preferred_element_type=F32)


def _dot_nt(a, b):
    return lax.dot_general(a.astype(BF16), b.astype(BF16), (((1,), (1,)), ((), ())),
                           preferred_element_type=F32)


def _split3(x):
    hi = x.astype(BF16)
    r = x - hi.astype(F32)
    mid = r.astype(BF16)
    lo = (r - mid.astype(F32)).astype(BF16)
    return hi, mid, lo


def _dot_exact_rhs(x, e):
    hi, mid, lo = _split3(x)
    return (jnp.dot(lo, e, preferred_element_type=F32) + jnp.dot(mid, e, preferred_element_type=F32)
            + jnp.dot(hi, e, preferred_element_type=F32))


def _dot_exact_lhs(e, x):
    hi, mid, lo = _split3(x)
    return (jnp.dot(e, lo, preferred_element_type=F32) + jnp.dot(e, mid, preferred_element_type=F32)
            + jnp.dot(e, hi, preferred_element_type=F32))


def _dot3(a, b):
    a_hi = a.astype(BF16)
    a_lo = (a - a_hi.astype(F32)).astype(BF16)
    b_hi = b.astype(BF16)
    b_lo = (b - b_hi.astype(F32)).astype(BF16)
    return (jnp.dot(a_lo, b_hi, preferred_element_type=F32) + jnp.dot(a_hi, b_lo, preferred_element_type=F32)
            + jnp.dot(a_hi, b_hi, preferred_element_type=F32))


def _rms(x, w):
    return x * lax.rsqrt(jnp.mean(x * x, axis=-1, keepdims=True) + NORM_EPS) * w


def _silu(x):
    return x / (1.0 + jnp.exp(-x))


def _softplus(x):
    return jnp.maximum(x, 0.0) + jnp.log1p(jnp.exp(-jnp.abs(x)))


def _full(shape):
    return pl.BlockSpec(shape, lambda *_: (0,) * len(shape))


def _resident(shape):
    return pl.BlockSpec(shape, lambda *_: (0,) * len(shape), pipeline_mode=pl.Buffered(1))


def _ffn_tile(xn_b, wup_ref, cw_ref, cb_ref, wdn_ref, carry_ref):
    tm = xn_b.shape[0]
    dff = wdn_ref.shape[0]
    nchunk = dff // FFN_CHUNK

    def up(c):
        return [jnp.dot(xn_b, wup_ref[:, half * dff + c * FFN_CHUNK:half * dff + (c + 1) * FFN_CHUNK],
                        preferred_element_type=F32) for half in range(2)]

    def conv_act(c, hs):
        conv = []
        for half, h in enumerate(hs):
            cols = slice(half * dff + c * FFN_CHUNK, half * dff + (c + 1) * FFN_CHUNK)
            ext = jnp.concatenate([carry_ref[:, cols], h], axis=0)
            y = (ext * cw_ref[2:3, cols] + pltpu.roll(ext, 1, 0) * cw_ref[1:2, cols]
                 + pltpu.roll(ext, 2, 0) * cw_ref[0:1, cols])
            conv.append(y[CONV_HALO:] + cb_ref[:, cols])
            carry_ref[:, cols] = h[tm - CONV_HALO:]
        return (_silu(conv[0]) * conv[1]).astype(BF16)

    acc = None
    acts = []
    ahead = [up(c) for c in range(min(FFN_LOOKAHEAD, nchunk))]
    for c in range(nchunk):
        cur = ahead.pop(0)
        if c + FFN_LOOKAHEAD < nchunk:
            ahead.append(up(c + FFN_LOOKAHEAD))
        acts.append(conv_act(c, cur))
        if len(acts) == FFN_DOWN_GROUP or c + 1 == nchunk:
            c0 = c + 1 - len(acts)
            a = acts[0] if len(acts) == 1 else jnp.concatenate(acts, axis=1)
            d = jnp.dot(a, wdn_ref[c0 * FFN_CHUNK:(c + 1) * FFN_CHUNK, :], preferred_element_type=F32)
            acc = d if acc is None else acc + d
            acts = []
    return acc


def _prompt_layer0_kernel(x_ref, n1_ref, pw_ref, ps_ref, n2_ref, wup_ref, cw_ref, cb_ref, wdn_ref,
                          xo_ref, pool_ref, fst_ref, hbuf, carry):
    t = pl.program_id(1)
    tm = x_ref.shape[1]
    gd = pw_ref.shape[1]

    @pl.when(t == 0)
    def _():
        hbuf[0:POOL_HALO, :] = jnp.zeros((POOL_HALO, hbuf.shape[1]), F32)
        carry[...] = jnp.zeros(carry.shape, F32)

    x = x_ref[0]
    h = _rms(x, n1_ref[...])
    hbuf[POOL_HALO:POOL_HALO + tm, :] = h
    pos1 = t * tm + lax.broadcasted_iota(jnp.int32, (tm, 1), 0) + 1
    parts = []
    for g, w in enumerate(POOL_WINDOWS):
        cols = slice(g * gd, (g + 1) * gd)
        s = hbuf[:, cols]
        sh = 1
        while sh < w:
            s = s + pltpu.roll(s, sh, 0)
            sh *= 2
        cnt = jnp.minimum(w, pos1).astype(F32)
        dg = s[POOL_HALO:] / cnt - h[:, cols]
        parts.append(x[:, cols] + _dot(dg, pw_ref[g]) * ps_ref[:, cols])
    x1 = jnp.concatenate(parts, axis=1)
    tail = hbuf[tm:tm + POOL_HALO, :]
    pool_ref[0] = tail
    hbuf[0:POOL_HALO, :] = tail

    xn = _rms(x1, n2_ref[...]).astype(BF16)
    xo_ref[0] = x1 + _ffn_tile(xn, wup_ref, cw_ref, cb_ref, wdn_ref, carry)
    fst_ref[0] = carry[...]


def _prompt_ffn_final_kernel(x_ref, n2_ref, wup_ref, cw_ref, cb_ref, wdn_ref, fn_ref,
                             yo_ref, fst_ref, carry):
    @pl.when(pl.program_id(1) == 0)
    def _():
        carry[...] = jnp.zeros(carry.shape, F32)

    x = x_ref[0]
    xn = _rms(x, n2_ref[...]).astype(BF16)
    x = x + _ffn_tile(xn, wup_ref, cw_ref, cb_ref, wdn_ref, carry)
    yo_ref[0] = _rms(x, fn_ref[...])
    fst_ref[0] = carry[...]


def _interleave(streams):
    merged = []
    for si, steps in enumerate(streams):
        total = float(sum(w for w, _ in steps))
        done = 0.0
        for w, thunk in steps:
            merged.append(((done + 0.5 * w) / total, si, len(merged), thunk))
            done += w
    for _, _, _, thunk in sorted(merged, key=lambda m: m[:3]):
        thunk()


def _prompt_deltanet_kernel(xa_ref, xb_ref, n1_ref, win_ref, wab_ref, cw_ref, alog_ref, dtb_ref, onw_ref,
                            wout_ref, ltri_ref,
                            xo_ref, cst_ref, ssm_ref,
                            carry, s_ref, qkv_s, kb_s, q_s, k_s, qd_s, kd_s, vb_s, wr_s, gcol_s, z_s, uin_s, win_s,
                            qk_s, o_s, og_s, *, tiles_per_seq):
    i = pl.program_id(0)
    tm = xa_ref.shape[1]
    dk = DN_HEAD_DIM
    dn = s_ref.shape[1]
    nh = dn // dk
    ch = DN_CHUNK
    nch = tm // ch
    gh = HEAD_GROUP
    gw = gh * dk
    pw = gh * ch
    d_model = xa_ref.shape[2]
    wr_slot = i % 2
    rd_slot = 1 - wr_slot

    @pl.when(i == 0)
    def _():
        for ref in (kb_s, q_s, k_s, qd_s, kd_s, vb_s, wr_s, gcol_s, z_s):
            ref[1] = jnp.zeros(ref.shape[1:], ref.dtype)
        s_ref[...] = jnp.zeros(s_ref.shape, F32)

    @pl.when(i % tiles_per_seq == 0)
    def _():
        carry[...] = jnp.zeros(carry.shape, F32)

    @pl.when((i + tiles_per_seq - 1) % tiles_per_seq == 0)
    def _():
        s_ref[...] = jnp.zeros(s_ref.shape, F32)

    front, back = [], []
    env = {}

    def f_norm():
        env["xb"] = _rms(xa_ref[0], n1_ref[...]).astype(BF16)

    def f_qkv(c):
        cols = slice(c * QKV_CHUNK, (c + 1) * QKV_CHUNK)
        pre = jnp.dot(env["xb"], win_ref[:, cols], preferred_element_type=F32)
        ext = jnp.concatenate([carry[:, cols], pre], axis=0)
        prev = pltpu.roll(ext, 1, 0)
        y = (ext * cw_ref[3:4, cols] + prev * cw_ref[2:3, cols]
             + pltpu.roll(ext * cw_ref[1:2, cols] + prev * cw_ref[0:1, cols], 2, 0))
        qkv_s[:, cols] = _silu(y[CONV_HALO:])
        carry[:, cols] = pre[tm - CONV_HALO:]
        cst_ref[0, :, cols] = pre[tm - CONV_HALO:]

    def f_gates():
        ab = jnp.dot(env["xb"], wab_ref[...], preferred_element_type=F32)
        g = -jnp.exp(alog_ref[...]) * _softplus(ab + dtb_ref[...])
        env["gc"] = _dot_exact_lhs(ltri_ref[...], g)
        env["sig"] = 1.0 / (1.0 + jnp.exp(-ab))

    def f_z(j):
        cols = slice(j * OUT_CHUNK, (j + 1) * OUT_CHUNK)
        z = jnp.dot(env["xb"], win_ref[:, 3 * dn + cols.start:3 * dn + cols.stop], preferred_element_type=F32)
        z_s[wr_slot, :, cols] = _silu(z)

    def f_head(h):
        hs = slice(h * dk, (h + 1) * dk)
        gcol = jnp.broadcast_to(env["gc"][:, h:h + 1], (tm, dk))
        beta = jnp.broadcast_to(env["sig"][:, nh + h:nh + h + 1], (tm, dk))
        glast = jnp.concatenate(
            [jnp.broadcast_to(gcol[(c + 1) * ch - 1:(c + 1) * ch, :], (ch, dk)) for c in range(nch)], axis=0)
        egc = jnp.exp(gcol)
        qh = qkv_s[:, hs]
        kh = qkv_s[:, dn + h * dk:dn + (h + 1) * dk]
        vh = qkv_s[:, 2 * dn + h * dk:2 * dn + (h + 1) * dk]
        qh = qh * (lax.rsqrt(jnp.sum(qh * qh, axis=-1, keepdims=True) + NORM_EPS) * (dk ** -0.5))
        kh = kh * lax.rsqrt(jnp.sum(kh * kh, axis=-1, keepdims=True) + NORM_EPS)
        kb = kh * beta
        gcol_s[wr_slot, :, hs] = gcol
        q_s[wr_slot, :, hs] = qh.astype(BF16)
        k_s[wr_slot, :, hs] = kh.astype(BF16)
        kb_s[wr_slot, :, hs] = kb.astype(BF16)
        qd_s[wr_slot, :, hs] = (qh * egc).astype(BF16)
        kd_s[wr_slot, :, hs] = (kh * jnp.exp(glast - gcol)).astype(BF16)
        vb_s[wr_slot, :, hs] = vh * beta
        wr_s[wr_slot, :, hs] = kb * egc

    front.append((2, f_norm))
    per_part = dn // QKV_CHUNK
    for c in [part * per_part + j for j in range(per_part) for part in range(3)]:
        front.append((20.0 * QKV_CHUNK / 1024, functools.partial(f_qkv, c)))
    front.append((3, f_gates))
    for h in range(nh):
        front.append((3, functools.partial(f_head, h)))
    for j in range(dn // OUT_CHUNK):
        front.append((5.0 * OUT_CHUNK / dn, functools.partial(f_z, j)))

    def iota2(shape, d):
        return lax.broadcasted_iota(jnp.int32, shape, d)

    pi, pj = iota2((ch, pw), 0), iota2((ch, pw), 1) % ch
    eye_p = pi == pj
    incl_p = pi >= pj
    strict_p = pi > pj
    eye_f = eye_p.astype(F32)
    head_of_lane = iota2((ch, pw), 1) // ch
    bd_p = (iota2((pw, pw), 0) // ch) == (iota2((pw, pw), 1) // ch)
    bd_k = (iota2((2 * ch, 2 * dk), 0) // ch) == (iota2((2 * ch, 2 * dk), 1) // dk)
    bd_u = (iota2((pw, gw), 0) // ch) == (iota2((pw, gw), 1) // dk)
    bd_s = (iota2((2 * dk, 2 * dk), 0) // dk) == (iota2((2 * dk, 2 * dk), 1) // dk)

    def expand(mc):
        return jnp.where(bd_p, jnp.concatenate([mc] * gh, axis=0), 0.0).astype(BF16)

    units = [(c, gi) for c in range(nch) for gi in range(nh // gh)]
    n_cs, m_cs, p_cs = [], [None] * len(units), [None] * len(units)

    def b_scores(c, gi):
        rows = slice(c * ch, (c + 1) * ch)
        h0 = gi * gh
        sc = []
        for p in range(gh // 2):
            ps = slice((h0 + 2 * p) * dk, (h0 + 2 * p + 2) * dk)
            kc = k_s[rd_slot, rows, ps]
            kbd = jnp.where(bd_k, jnp.concatenate([kc, kc], axis=0), jnp.zeros((), BF16))
            sc.append(lax.dot_general(jnp.concatenate([kb_s[rd_slot, rows, ps], q_s[rd_slot, rows, ps]], axis=0),
                                      kbd, (((1,), (1,)), ((), ())), preferred_element_type=F32))
        sc = jnp.concatenate(sc, axis=1)
        gcol_p = jnp.broadcast_to(gcol_s[rd_slot, rows, h0 * dk:h0 * dk + 1], (ch, pw))
        for hl in range(1, gh):
            gcol_p = jnp.where(
                head_of_lane == hl,
                jnp.broadcast_to(gcol_s[rd_slot, rows, (h0 + hl) * dk:(h0 + hl) * dk + 1], (ch, pw)), gcol_p)
        grow_p = jnp.sum(jnp.where(eye_p, gcol_p, 0.0), axis=0, keepdims=True)
        decay = jnp.exp(jnp.where(incl_p, gcol_p - grow_p, -jnp.inf))
        n_cs.append(jnp.where(strict_p, sc[0:ch] * decay, 0.0))
        qk_s[c, gi] = (sc[ch:2 * ch] * decay).astype(BF16)

    def b_square(u):
        m_cs[u] = jnp.dot(n_cs[u].astype(BF16), expand(n_cs[u]), preferred_element_type=F32)
        p_cs[u] = eye_f - n_cs[u]

    def b_level(u, last):
        m_bd = expand(m_cs[u])
        if last:
            p_cs[u] = p_cs[u] + jnp.dot(p_cs[u].astype(BF16), m_bd, preferred_element_type=F32)
        else:
            both = jnp.dot(jnp.concatenate([m_cs[u], p_cs[u]], axis=0).astype(BF16), m_bd,
                           preferred_element_type=F32)
            m_cs[u] = both[0:ch]
            p_cs[u] = p_cs[u] + both[ch:2 * ch]

    def b_apply(u):
        c, gi = units[u]
        rows = slice(c * ch, (c + 1) * ch)
        h0 = gi * gh
        rhs = jnp.concatenate(
            [jnp.concatenate([vb_s[rd_slot, rows, (h0 + hl) * dk:(h0 + hl + 1) * dk],
                              wr_s[rd_slot, rows, (h0 + hl) * dk:(h0 + hl + 1) * dk]], axis=1)
             for hl in range(gh)], axis=0)
        uw = rhs + jnp.dot(expand(p_cs[u] - eye_f), rhs.astype(BF16), preferred_element_type=F32)
        for hl in range(gh):
            hs = slice((h0 + hl) * dk, (h0 + hl + 1) * dk)
            uin_s[rows, hs] = uw[hl * ch:(hl + 1) * ch, 0:dk]
            win_s[rows, hs] = uw[hl * ch:(hl + 1) * ch, dk:2 * dk].astype(BF16)

    def b_recur(c, gi):
        rows = slice(c * ch, (c + 1) * ch)
        last = (c + 1) * ch - 1
        gs = slice(gi * gw, (gi + 1) * gw)
        u_parts, qs_parts = [], []
        for p in range(gh // 2):
            ps = slice(gi * gw + 2 * p * dk, gi * gw + (2 * p + 2) * dk)
            s_pair = s_ref[:, ps]
            s_bd = jnp.where(bd_s, jnp.concatenate([s_pair, s_pair], axis=0), 0.0).astype(BF16)
            r = jnp.dot(jnp.concatenate([win_s[rows, ps], qd_s[rd_slot, rows, ps]], axis=0), s_bd,
                        preferred_element_type=F32)
            u_parts.append(uin_s[rows, ps] - r[0:ch])
            qs_parts.append(r[ch:2 * ch])
        u_g = jnp.concatenate(u_parts, axis=1)
        u_bd = jnp.where(bd_u, jnp.concatenate([u_g] * gh, axis=0), 0.0).astype(BF16)
        o_s[rows, gs] = jnp.concatenate(qs_parts, axis=1) + jnp.dot(qk_s[c, gi], u_bd,
                                                                      preferred_element_type=F32)
        kd_stack = jnp.concatenate(
            [kd_s[rd_slot, rows, gi * gw + hl * dk:gi * gw + (hl + 1) * dk] for hl in range(gh)], axis=0)
        s_ref[:, gs] = (s_ref[:, gs] * jnp.exp(gcol_s[rd_slot, last:last + 1, gs])
                        + lax.dot_general(kd_stack, u_bd, (((0,), (0,)), ((), ())),
                                          preferred_element_type=F32))

    def b_gate(h):
        hs = slice(h * dk, (h + 1) * dk)
        og_s[:, hs] = (_rms(o_s[:, hs], onw_ref[...]) * z_s[rd_slot, :, hs]).astype(BF16)

    def b_proj(j):
        cols = slice(j * OUT_CHUNK, (j + 1) * OUT_CHUNK)
        xo_ref[0, :, cols] = xb_ref[0, :, cols] + jnp.dot(og_s[...], wout_ref[:, cols],
                                                         preferred_element_type=F32)

    def b_state():
        for h in range(nh):
            ssm_ref[0, h] = s_ref[:, h * dk:(h + 1) * dk]

    for c, gi in units:
        back.append((2, functools.partial(b_scores, c, gi)))
    for u in range(len(units)):
        back.append((1, functools.partial(b_square, u)))
    sq = 2
    while sq < ch:
        for u in range(len(units)):
            back.append((1, functools.partial(b_level, u, 2 * sq >= ch)))
        sq *= 2
    for u in range(len(units)):
        back.append((1.5, functools.partial(b_apply, u)))
    for c in range(nch):
        for gi in range(nh // gh):
            back.append((3, functools.partial(b_recur, c, gi)))
    for h in range(nh):
        back.append((0.5, functools.partial(b_gate, h)))
    for j in range(d_model // OUT_CHUNK):
        back.append((1, functools.partial(b_proj, j)))
    back.append((0.5, b_state))

    _interleave([front, back])


def _ffn_sample_chunk(xn_ref, wg_ref, wv_ref, cwg_ref, cwv_ref, cbg_ref, cbv_ref, wdn_ref, stg_ref, stv_ref,
                      sto_ref):
    sc = wg_ref.shape[1]
    dff = sto_ref.shape[2] // 2
    conv = []
    for half, (w_ref, cw_ref, cb_ref, st_ref) in enumerate(((wg_ref, cwg_ref, cbg_ref, stg_ref),
                                                            (wv_ref, cwv_ref, cbv_ref, stv_ref))):
        h = jnp.dot(xn_ref[...], w_ref[...], preferred_element_type=F32)
        prev = st_ref[:, 1, :]
        conv.append(st_ref[:, 0, :] * cw_ref[0:1, :] + prev * cw_ref[1:2, :] + h * cw_ref[2:3, :] + cb_ref[...])
        for cc in range(dff // sc):
            @pl.when(pl.program_id(0) == cc)
            def _(half=half, cc=cc, prev=prev, h=h):
                cols = slice(half * dff + cc * sc, half * dff + (cc + 1) * sc)
                sto_ref[:, 0, cols] = prev
                sto_ref[:, 1, cols] = h
    a = (_silu(conv[0]) * conv[1]).astype(BF16)
    return jnp.dot(a, wdn_ref[...], preferred_element_type=F32)


def _sample_layer0_kernel(x_ref, pb_ref, n1_ref, pw_ref, ps_ref, n2_ref,
                          wg_ref, wv_ref, cwg_ref, cwv_ref, cbg_ref, cbv_ref, wdn_ref, stg_ref, stv_ref,
                          xo_ref, ho_ref, sto_ref, xn_s):
    c = pl.program_id(0)
    nbuf = pb_ref.shape[0]
    gd = pw_ref.shape[1]

    @pl.when(c == 0)
    def _():
        x = x_ref[...]
        h = _rms(x, n1_ref[...])
        parts = []
        for g, w in enumerate(POOL_WINDOWS):
            cols = slice(g * gd, (g + 1) * gd)
            s = h[:, cols]
            for j in range(1, w):
                s = s + pb_ref[nbuf - j, :, cols]
            dg = s / float(w) - h[:, cols]
            parts.append(x[:, cols] + _dot(dg, pw_ref[g]) * ps_ref[:, cols])
        x1 = jnp.concatenate(parts, axis=1)
        ho_ref[...] = h
        xn_s[...] = _rms(x1, n2_ref[...]).astype(BF16)
        xo_ref[...] = x1

    xo_ref[...] += _ffn_sample_chunk(xn_s, wg_ref, wv_ref, cwg_ref, cwv_ref, cbg_ref, cbv_ref, wdn_ref,
                                     stg_ref, stv_ref, sto_ref)


def _sample_dn_proj_kernel(x_ref, n1_ref, win_ref, wab_ref, cw_ref, alog_ref, dtb_ref,
                           selg_ref, selb_ref, cst_ref,
                           w_ref, qd_ref, k_ref, vb_ref, qk_ref, eg_ref, z_ref, pre_ref):
    dk = DN_HEAD_DIM
    dn = cw_ref.shape[1] // 3
    nh = dn // dk
    xb = _rms(x_ref[...], n1_ref[...]).astype(BF16)
    pre = jnp.dot(xb, win_ref[:, 0:3 * dn], preferred_element_type=F32)
    y = (cst_ref[0] * cw_ref[0:1, :] + cst_ref[1] * cw_ref[1:2, :] + cst_ref[2] * cw_ref[2:3, :]
         + pre * cw_ref[3:4, :])
    qkv = _silu(y)
    pre_ref[...] = pre
    z_ref[...] = jnp.dot(xb, win_ref[:, 3 * dn:4 * dn], preferred_element_type=F32)
    ab = jnp.dot(xb, wab_ref[...], preferred_element_type=F32)
    g = -jnp.exp(alog_ref[...]) * _softplus(ab + dtb_ref[...])
    eg = jnp.exp(_dot_exact_rhs(g, selg_ref[...]))
    betab = _dot_exact_rhs(1.0 / (1.0 + jnp.exp(-ab)), selb_ref[...])
    eg_ref[...] = eg
    for h in range(nh):
        hs = slice(h * dk, (h + 1) * dk)
        q = qkv[:, hs]
        k = qkv[:, dn + h * dk:dn + (h + 1) * dk]
        v = qkv[:, 2 * dn + h * dk:2 * dn + (h + 1) * dk]
        q = q * lax.rsqrt(jnp.sum(q * q, axis=-1, keepdims=True) + NORM_EPS) * (dk ** -0.5)
        k = k * lax.rsqrt(jnp.sum(k * k, axis=-1, keepdims=True) + NORM_EPS)
        kb = k * betab[:, hs]
        qk = jnp.sum(q.astype(BF16).astype(F32) * k.astype(BF16).astype(F32), axis=-1, keepdims=True)
        w_ref[:, hs] = kb * eg[:, hs]
        qd_ref[:, hs] = q * eg[:, hs]
        k_ref[:, hs] = k
        vb_ref[:, hs] = v * betab[:, hs]
        qk_ref[:, hs] = jnp.broadcast_to(qk, (q.shape[0], dk))


def _sample_dn_state_kernel(w_ref, qd_ref, k_ref, vb_ref, qk_ref, eg_ref, s_ref, o_ref, so_ref):
    dk = DN_HEAD_DIM
    nb, nh = s_ref.shape[0], s_ref.shape[1]
    eye = (lax.broadcasted_iota(jnp.int32, (dk, dk), 0) == lax.broadcasted_iota(jnp.int32, (dk, dk), 1))
    for b in range(nb):
        for h in range(nh):
            hs = slice(h * dk, (h + 1) * dk)
            s = s_ref[b, h]
            lhs = jnp.concatenate([jnp.broadcast_to(w_ref[b:b + 1, hs], (8, dk)),
                                   jnp.broadcast_to(qd_ref[b:b + 1, hs], (8, dk))], axis=0)
            r = _dot(lhs, s)
            u = vb_ref[b:b + 1, hs] - r[0:1]
            ub = u.astype(BF16).astype(F32)
            o_ref[b:b + 1, hs] = r[8:9] + qk_ref[b:b + 1, hs] * ub
            kdiag = jnp.where(eye, jnp.broadcast_to(k_ref[b:b + 1, hs], (dk, dk)), 0.0)
            so_ref[b, h] = s * eg_ref[b:b + 1, hs] + _dot(kdiag, jnp.broadcast_to(u, (dk, dk)))


def _sample_tail_kernel(x_ref, o_ref, z_ref, onw_ref, wout_ref, n2_ref,
                        wg_ref, wv_ref, cwg_ref, cwv_ref, cbg_ref, cbv_ref, wdn_ref, stg_ref, stv_ref, fn_ref,
                        yo_ref, sto_ref, xn_s, acc_s):
    c = pl.program_id(0)
    dk = DN_HEAD_DIM

    @pl.when(c == 0)
    def _():
        outs = []
        for h in range(o_ref.shape[1] // dk):
            outs.append(_rms(o_ref[:, h * dk:(h + 1) * dk], onw_ref[...]))
        o = jnp.concatenate(outs, axis=1) * _silu(z_ref[...])
        x = x_ref[...] + _dot(o, wout_ref[...])
        acc_s[...] = x
        xn_s[...] = _rms(x, n2_ref[...]).astype(BF16)

    acc_s[...] += _ffn_sample_chunk(xn_s, wg_ref, wv_ref, cwg_ref, cwv_ref, cbg_ref, cbv_ref, wdn_ref,
                                    stg_ref, stv_ref, sto_ref)

    @pl.when(c == pl.num_programs(0) - 1)
    def _():
        yo_ref[...] = _rms(acc_s[...], fn_ref[...])


def _params(*sem):
    return pltpu.CompilerParams(dimension_semantics=sem, vmem_limit_bytes=VMEM_LIMIT)


def kernel(x_prompt, x_sample, state_pool_buf, state_dn_conv, state_dn_ssm, state_ffn_conv, norm1_w, norm2_w,
           final_norm_w, pool_w, pool_scale, dn_w_in, dn_conv_w, dn_a_log, dn_dt_bias, dn_o_norm_w, dn_w_out,
           ffn_w_up, ffn_conv_w, ffn_conv_b, ffn_w_down):
    bp, seq, d = x_prompt.shape
    bs = x_sample.shape[0]
    nbuf = state_pool_buf.shape[2]
    dff = ffn_w_down.shape[1]
    nh = dn_a_log.shape[1]
    dk = DN_HEAD_DIM
    dn = nh * dk
    gd = pool_w.shape[2]
    tm = PROMPT_TILE
    nt = seq // tm
    nfc = dff // FFN_CHUNK
    assert seq % tm == 0 and dff % FFN_CHUNK == 0 and (3 * dn) % QKV_CHUNK == 0 and dk == 128
    assert nbuf + 1 == max(POOL_WINDOWS) and 2 * nh <= 128 and bs % SAMPLE_BLOCK == 0

    row = lambda v: v.reshape(1, -1)
    wup = ffn_w_up.astype(BF16)
    wdn = ffn_w_down.astype(BF16)
    pw = pool_w.astype(BF16)
    w_in = dn_w_in[0]
    win_all = dn_w_in.astype(BF16)
    wab = jnp.pad(w_in[:, 4 * dn:], ((0, 0), (0, 128 - 2 * nh))).astype(BF16)
    wout = dn_w_out[0].astype(BF16)
    alog = jnp.pad(dn_a_log[0], (0, 128 - nh)).reshape(1, 128)
    dtb = jnp.pad(dn_dt_bias[0], (0, 128 - nh)).reshape(1, 128)
    onw = row(dn_o_norm_w[0])
    lane_head = jnp.arange(dn) // dk
    selg = (jnp.arange(128)[:, None] == lane_head[None, :]).astype(BF16)
    selb = (jnp.arange(128)[:, None] == (lane_head[None, :] + nh)).astype(BF16)
    ti = jnp.arange(tm)
    ltri = ((ti[:, None] // DN_CHUNK == ti[None, :] // DN_CHUNK) & (ti[:, None] >= ti[None, :])).astype(BF16)

    tf = FFN_TILE
    assert seq % tf == 0
    x_spec = pl.BlockSpec((1, tf, d), lambda b, t: (b, t, 0))
    layer_resident = lambda shape, layer: pl.BlockSpec((None,) + shape, lambda *_: (layer,) + (0,) * len(shape),
                                                       pipeline_mode=pl.Buffered(1))
    ffn_w_specs = lambda layer: [layer_resident((d, 2 * dff), layer), _resident((3, 2 * dff)),
                                 _resident((1, 2 * dff)), layer_resident((dff, d), layer)]
    fst_spec = pl.BlockSpec((1, CONV_HALO, 2 * dff), lambda b, t: (b, 0, 0))
    fst_shape = jax.ShapeDtypeStruct((bp, CONV_HALO, 2 * dff), F32)

    x2, pool_tail, fst0 = pl.pallas_call(
        _prompt_layer0_kernel,
        grid=(bp, seq // tf),
        in_specs=[x_spec, _resident((1, d)), _resident(pw.shape[1:]), _resident((1, d)), _resident((1, d))]
        + ffn_w_specs(0),
        out_specs=[x_spec, pl.BlockSpec((1, POOL_HALO, d), lambda b, t: (b, 0, 0)), fst_spec],
        out_shape=[jax.ShapeDtypeStruct((bp, seq, d), F32), jax.ShapeDtypeStruct((bp, POOL_HALO, d), F32), fst_shape],
        scratch_shapes=[pltpu.VMEM((POOL_HALO + tf, d), F32), pltpu.VMEM((CONV_HALO, 2 * dff), F32)],
        compiler_params=_params("arbitrary", "arbitrary"),
        name="prompt_layer0",
    )(x_prompt, row(norm1_w[0]), pw[0], row(pool_scale[0]), row(norm2_w[0]),
      wup, ffn_conv_w[0], row(ffn_conv_b[0]), wdn)

    ntiles = bp * nt
    front_tile = lambda i: jnp.minimum(i, ntiles - 1)
    back_tile = lambda i: jnp.maximum(i - 1, 0)
    x_front = pl.BlockSpec((1, tm, d), lambda i: (front_tile(i) // nt, front_tile(i) % nt, 0))
    x_back = pl.BlockSpec((1, tm, d), lambda i: (back_tile(i) // nt, back_tile(i) % nt, 0))
    x3, cst, ssm = pl.pallas_call(
        functools.partial(_prompt_deltanet_kernel, tiles_per_seq=nt),
        grid=(ntiles + 1,),
        in_specs=[x_front, x_back,
                  _resident((1, d)), layer_resident((d, win_all.shape[2]), 0), _resident((d, 128)),
                  _resident((4, 3 * dn)), _resident((1, 128)), _resident((1, 128)), _resident((1, dk)),
                  _resident((dn, d)), _resident((tm, tm))],
        out_specs=[x_back,
                   pl.BlockSpec((1, CONV_HALO, 3 * dn), lambda i: (front_tile(i) // nt, 0, 0)),
                   pl.BlockSpec((1, nh, dk, dk), lambda i: (back_tile(i) // nt, 0, 0, 0))],
        out_shape=[jax.ShapeDtypeStruct((bp, seq, d), F32),
                   jax.ShapeDtypeStruct((bp, CONV_HALO, 3 * dn), F32),
                   jax.ShapeDtypeStruct((bp, nh, dk, dk), F32)],
        scratch_shapes=[pltpu.VMEM((CONV_HALO, 3 * dn), F32), pltpu.VMEM((dk, dn), F32),
                        pltpu.VMEM((tm, 3 * dn), F32)]
        + [pltpu.VMEM((2, tm, dn), BF16)] * 5 + [pltpu.VMEM((2, tm, dn), F32)] * 4
        + [pltpu.VMEM((tm, dn), F32), pltpu.VMEM((tm, dn), BF16),
           pltpu.VMEM((tm // DN_CHUNK, nh // HEAD_GROUP, DN_CHUNK, HEAD_GROUP * DN_CHUNK), BF16),
           pltpu.VMEM((tm, dn), F32), pltpu.VMEM((tm, dn), BF16)],
        compiler_params=_params("arbitrary"),
        name="prompt_deltanet",
    )(x2, x2, row(norm1_w[1]), win_all, wab, dn_conv_w[0], alog, dtb, onw, wout, ltri)

    y_prompt, fst1 = pl.pallas_call(
        _prompt_ffn_final_kernel,
        grid=(bp, seq // tf),
        in_specs=[x_spec, _resident((1, d))] + ffn_w_specs(1) + [_resident((1, d))],
        out_specs=[x_spec, fst_spec],
        out_shape=[jax.ShapeDtypeStruct((bp, seq, d), F32), fst_shape],
        scratch_shapes=[pltpu.VMEM((CONV_HALO, 2 * dff), F32)],
        compiler_params=_params("arbitrary", "arbitrary"),
        name="prompt_ffn_final",
    )(x3, row(norm2_w[1]), wup, ffn_conv_w[1], row(ffn_conv_b[1]), wdn, row(final_norm_w))

    xs = x_sample[:, 0, :]
    pb = jnp.transpose(state_pool_buf[0], (1, 0, 2))
    cs = jnp.transpose(state_dn_conv[0], (1, 0, 2))
    sc = SAMPLE_FFN_CHUNK
    nsc = dff // sc
    assert dff % sc == 0

    def ffn_chunk_specs(layer):
        specs = []
        for shape in ((None, d, sc), (3, sc), (1, sc)):
            for half in range(2):
                if shape[0] is None:
                    specs.append(pl.BlockSpec(shape, lambda c, half=half: (layer, 0, half * nsc + c)))
                else:
                    specs.append(pl.BlockSpec(shape, lambda c, half=half: (0, half * nsc + c)))
        specs.append(pl.BlockSpec((None, sc, d), lambda c: (layer, c, 0)))
        for half in range(2):
            specs.append(pl.BlockSpec((None, bs, 2, sc), lambda c, half=half: (layer, 0, 0, half * nsc + c)))
        return specs

    sto_spec = _full((bs, 2, 2 * dff))
    sto_shape = jax.ShapeDtypeStruct((bs, 2, 2 * dff), F32)

    def ffn_chunk_args(layer):
        cb = row(ffn_conv_b[layer])
        return (wup, wup, ffn_conv_w[layer], ffn_conv_w[layer], cb, cb, wdn, state_ffn_conv, state_ffn_conv)

    xs2, pool_row, fso0 = pl.pallas_call(
        _sample_layer0_kernel,
        grid=(nsc,),
        in_specs=[_resident((bs, d)), _resident((nbuf, bs, d)), _resident((1, d)), _resident(pw.shape[1:]),
                  _resident((1, d)), _resident((1, d))] + ffn_chunk_specs(0),
        out_specs=[_full((bs, d)), _full((bs, d)), sto_spec],
        out_shape=[jax.ShapeDtypeStruct((bs, d), F32), jax.ShapeDtypeStruct((bs, d), F32), sto_shape],
        scratch_shapes=[pltpu.VMEM((bs, d), BF16)],
        compiler_params=_params("arbitrary"),
        name="sample_layer0",
    )(xs, pb, row(norm1_w[0]), pw[0], row(pool_scale[0]), row(norm2_w[0]), *ffn_chunk_args(0))

    vec = jax.ShapeDtypeStruct((bs, dn), F32)
    w_s, qd_s, k_s, vb_s, qk_s, eg_s, z_s, conv_row = pl.pallas_call(
        _sample_dn_proj_kernel,
        out_shape=[vec] * 7 + [jax.ShapeDtypeStruct((bs, 3 * dn), F32)],
        compiler_params=pltpu.CompilerParams(vmem_limit_bytes=VMEM_LIMIT),
        name="sample_dn_proj",
    )(xs2, row(norm1_w[1]), win_all[0], wab, dn_conv_w[0], alog, dtb, selg, selb, cs)

    nb = SAMPLE_BLOCK
    vspec = pl.BlockSpec((nb, dn), lambda i: (i, 0))
    sspec = pl.BlockSpec((nb, nh, dk, dk), lambda i: (i, 0, 0, 0))
    o_s, ssm_s = pl.pallas_call(
        _sample_dn_state_kernel,
        grid=(bs // nb,),
        in_specs=[vspec] * 6 + [sspec],
        out_specs=[vspec, sspec],
        out_shape=[vec, jax.ShapeDtypeStruct((bs, nh, dk, dk), F32)],
        compiler_params=_params("arbitrary"),
        name="sample_dn_state",
    )(w_s, qd_s, k_s, vb_s, qk_s, eg_s, state_dn_ssm[0])

    ys, fso1 = pl.pallas_call(
        _sample_tail_kernel,
        grid=(nsc,),
        in_specs=[_resident((bs, d)), _resident((bs, dn)), _resident((bs, dn)), _resident((1, dk)),
                  _resident((dn, d)), _resident((1, d))] + ffn_chunk_specs(1) + [_resident((1, d))],
        out_specs=[_full((bs, d)), sto_spec],
        out_shape=[jax.ShapeDtypeStruct((bs, d), F32), sto_shape],
        scratch_shapes=[pltpu.VMEM((bs, d), BF16), pltpu.VMEM((bs, d), F32)],
        compiler_params=_params("arbitrary"),
        name="sample_tail",
    )(xs2, o_s, z_s, onw, wout, row(norm2_w[1]), *ffn_chunk_args(1), row(final_norm_w))

    shifted = lambda state, new_row: jnp.concatenate([state[:, 1:], new_row[:, None]], axis=1)
    return (
        y_prompt,
        ys[:, None, :],
        pool_tail[None, :, POOL_HALO - nbuf:, :],
        shifted(state_pool_buf[0], pool_row)[None],
        cst[None, :, CONV_HALO - 3:, :],
        shifted(state_dn_conv[0], conv_row)[None],
        ssm[None],
        ssm_s[None],
        jnp.stack([fst0[:, CONV_HALO - 2:, :], fst1[:, CONV_HALO - 2:, :]]),
        jnp.stack([fso0, fso1]),
    )
```

```python
import functools

import jax
import jax.numpy as jnp
from jax import lax
from jax.experimental import pallas as pl
from jax.experimental.pallas import tpu as pltpu

F32 = jnp.float32
BF16 = jnp.bfloat16

NORM_EPS = 1e-6
POOL_WINDOWS = (2, 4, 8, 16)
POOL_HALO = 16
CONV_HALO = 8
DN_HEAD_DIM = 128
DN_CHUNK = 64
V7X_LANES = 128
V7X_MXU_WIDTH = 256
V7X_VMEM_BYTES = 64 * 1024 * 1024
FFN_CHUNK = V7X_MXU_WIDTH
FFN_DOWN_GROUP = 4
FFN_LOOKAHEAD = 3
QKV_CHUNK = V7X_MXU_WIDTH
OUT_CHUNK = V7X_MXU_WIDTH
PROMPT_TILE = 256
FFN_TILE = 512
HEAD_GROUP = 4
SAMPLE_FFN_CHUNK = 1408
SAMPLE_BLOCK = 16
VMEM_LIMIT = V7X_VMEM_BYTES * 7 // 8


def _dot(a, b):
    return jnp.dot(a.astype(BF16), b.astype(BF16), preferred_element_type=F32)


def _split3(x):
    hi = x.astype(BF16)
    r = x - hi.astype(F32)
    mid = r.astype(BF16)
    lo = (r - mid.astype(F32)).astype(BF16)
    return hi, mid, lo


def _dot_exact_rhs(x, e):
    hi, mid, lo = _split3(x)
    return (jnp.dot(lo, e, preferred_element_type=F32) + jnp.dot(mid, e, preferred_element_type=F32)
            + jnp.dot(hi, e, preferred_element_type=F32))


def _dot_exact_lhs(e, x):
    hi, mid, lo = _split3(x)
    return (jnp.dot(e, lo, preferred_element_type=F32) + jnp.dot(e, mid, preferred_element_type=F32)
            + jnp.dot(e, hi, preferred_element_type=F32))


def _rms(x, w):
    return x * lax.rsqrt(jnp.mean(x * x, axis=-1, keepdims=True) + NORM_EPS) * w


def _silu(x):
    return x / (1.0 + jnp.exp(-x))


def _softplus(x):
    return jnp.maximum(x, 0.0) + jnp.log1p(jnp.exp(-jnp.abs(x)))


def _full(shape):
    return pl.BlockSpec(shape, lambda *_: (0,) * len(shape))


def _resident(shape):
    return pl.BlockSpec(shape, lambda *_: (0,) * len(shape), pipeline_mode=pl.Buffered(1))


def _ffn_tile(xn_b, wup_ref, cw_ref, cb_ref, wdn_ref, carry_ref):
    tm = xn_b.shape[0]
    dff = wdn_ref.shape[0]
    nchunk = dff // FFN_CHUNK

    def up(c):
        return [jnp.dot(xn_b, wup_ref[:, half * dff + c * FFN_CHUNK:half * dff + (c + 1) * FFN_CHUNK],
                        preferred_element_type=F32) for half in range(2)]

    def conv_act(c, hs):
        conv = []
        for half, h in enumerate(hs):
            cols = slice(half * dff + c * FFN_CHUNK, half * dff + (c + 1) * FFN_CHUNK)
            ext = jnp.concatenate([carry_ref[:, cols], h], axis=0)
            y = (ext * cw_ref[2:3, cols] + pltpu.roll(ext, 1, 0) * cw_ref[1:2, cols]
                 + pltpu.roll(ext, 2, 0) * cw_ref[0:1, cols])
            conv.append(y[CONV_HALO:] + cb_ref[:, cols])
            carry_ref[:, cols] = h[tm - CONV_HALO:]
        return (_silu(conv[0]) * conv[1]).astype(BF16)

    acc = None
    acts = []
    ahead = [up(c) for c in range(min(FFN_LOOKAHEAD, nchunk))]
    for c in range(nchunk):
        cur = ahead.pop(0)
        if c + FFN_LOOKAHEAD < nchunk:
            ahead.append(up(c + FFN_LOOKAHEAD))
        acts.append(conv_act(c, cur))
        if len(acts) == FFN_DOWN_GROUP or c + 1 == nchunk:
            c0 = c + 1 - len(acts)
            a = acts[0] if len(acts) == 1 else jnp.concatenate(acts, axis=1)
            d = jnp.dot(a, wdn_ref[c0 * FFN_CHUNK:(c + 1) * FFN_CHUNK, :], preferred_element_type=F32)
            acc = d if acc is None else acc + d
            acts = []
    return acc


def _prompt_layer0_kernel(x_ref, n1_ref, pw_ref, ps_ref, n2_ref, wup_ref, cw_ref, cb_ref, wdn_ref,
                          xo_ref, pool_ref, fst_ref, hbuf, carry):
    t = pl.program_id(1)
    tm = x_ref.shape[1]
    gd = pw_ref.shape[1]

    @pl.when(t == 0)
    def _():
        hbuf[0:POOL_HALO, :] = jnp.zeros((POOL_HALO, hbuf.shape[1]), F32)
        carry[...] = jnp.zeros(carry.shape, F32)

    x = x_ref[0]
    h = _rms(x, n1_ref[...])
    hbuf[POOL_HALO:POOL_HALO + tm, :] = h
    pos1 = t * tm + lax.broadcasted_iota(jnp.int32, (tm, 1), 0) + 1
    parts = []
    for g, w in enumerate(POOL_WINDOWS):
        cols = slice(g * gd, (g + 1) * gd)
        s = hbuf[:, cols]
        sh = 1
        while sh < w:
            s = s + pltpu.roll(s, sh, 0)
            sh *= 2
        cnt = jnp.minimum(w, pos1).astype(F32)
        dg = s[POOL_HALO:] / cnt - h[:, cols]
        parts.append(x[:, cols] + _dot(dg, pw_ref[g]) * ps_ref[:, cols])
    x1 = jnp.concatenate(parts, axis=1)
    tail = hbuf[tm:tm + POOL_HALO, :]
    pool_ref[0] = tail
    hbuf[0:POOL_HALO, :] = tail

    xn = _rms(x1, n2_ref[...]).astype(BF16)
    xo_ref[0] = x1 + _ffn_tile(xn, wup_ref, cw_ref, cb_ref, wdn_ref, carry)
    fst_ref[0] = carry[...]


def _prompt_ffn_final_kernel(x_ref, n2_ref, wup_ref, cw_ref, cb_ref, wdn_ref, fn_ref,
                             yo_ref, fst_ref, carry):
    @pl.when(pl.program_id(1) == 0)
    def _():
        carry[...] = jnp.zeros(carry.shape, F32)

    x = x_ref[0]
    xn = _rms(x, n2_ref[...]).astype(BF16)
    x = x + _ffn_tile(xn, wup_ref, cw_ref, cb_ref, wdn_ref, carry)
    yo_ref[0] = _rms(x, fn_ref[...])
    fst_ref[0] = carry[...]


def _interleave(streams):
    merged = []
    for si, steps in enumerate(streams):
        total = float(sum(w for w, _ in steps))
        done = 0.0
        for w, thunk in steps:
            merged.append(((done + 0.5 * w) / total, si, len(merged), thunk))
            done += w
    for _, _, _, thunk in sorted(merged, key=lambda m: m[:3]):
        thunk()


def _prompt_deltanet_kernel(xa_ref, xb_ref, n1_ref, win_ref, wab_ref, cw_ref, alog_ref, dtb_ref, onw_ref,
                            wout_ref, ltri_ref,
                            xo_ref, cst_ref, ssm_ref,
                            carry, s_ref, qkv_s, kb_s, q_s, k_s, qd_s, kd_s, vb_s, wr_s, gcol_s, z_s, uin_s, win_s,
                            qk_s, o_s, og_s, *, tiles_per_seq):
    i = pl.program_id(0)
    tm = xa_ref.shape[1]
    dk = DN_HEAD_DIM
    dn = s_ref.shape[1]
    nh = dn // dk
    ch = DN_CHUNK
    nch = tm // ch
    gh = HEAD_GROUP
    gw = gh * dk
    pw = gh * ch
    d_model = xa_ref.shape[2]
    wr_slot = i % 2
    rd_slot = 1 - wr_slot

    @pl.when(i == 0)
    def _():
        for ref in (kb_s, q_s, k_s, qd_s, kd_s, vb_s, wr_s, gcol_s, z_s):
            ref[1] = jnp.zeros(ref.shape[1:], ref.dtype)
        s_ref[...] = jnp.zeros(s_ref.shape, F32)

    @pl.when(i % tiles_per_seq == 0)
    def _():
        carry[...] = jnp.zeros(carry.shape, F32)

    @pl.when((i + tiles_per_seq - 1) % tiles_per_seq == 0)
    def _():
        s_ref[...] = jnp.zeros(s_ref.shape, F32)

    front, back = [], []
    env = {}

    def f_norm():
        env["xb"] = _rms(xa_ref[0], n1_ref[...]).astype(BF16)

    def f_qkv(c):
        cols = slice(c * QKV_CHUNK, (c + 1) * QKV_CHUNK)
        pre = jnp.dot(env["xb"], win_ref[:, cols], preferred_element_type=F32)
        ext = jnp.concatenate([carry[:, cols], pre], axis=0)
        prev = pltpu.roll(ext, 1, 0)
        y = (ext * cw_ref[3:4, cols] + prev * cw_ref[2:3, cols]
             + pltpu.roll(ext * cw_ref[1:2, cols] + prev * cw_ref[0:1, cols], 2, 0))
        qkv_s[:, cols] = _silu(y[CONV_HALO:])
        carry[:, cols] = pre[tm - CONV_HALO:]
        cst_ref[0, :, cols] = pre[tm - CONV_HALO:]

    def f_gates():
        ab = jnp.dot(env["xb"], wab_ref[...], preferred_element_type=F32)
        g = -jnp.exp(alog_ref[...]) * _softplus(ab + dtb_ref[...])
        env["gc"] = _dot_exact_lhs(ltri_ref[...], g)
        env["sig"] = 1.0 / (1.0 + jnp.exp(-ab))

    def f_z(j):
        cols = slice(j * OUT_CHUNK, (j + 1) * OUT_CHUNK)
        z = jnp.dot(env["xb"], win_ref[:, 3 * dn + cols.start:3 * dn + cols.stop], preferred_element_type=F32)
        z_s[wr_slot, :, cols] = _silu(z)

    def f_head(h):
        hs = slice(h * dk, (h + 1) * dk)
        gcol = jnp.broadcast_to(env["gc"][:, h:h + 1], (tm, dk))
        beta = jnp.broadcast_to(env["sig"][:, nh + h:nh + h + 1], (tm, dk))
        glast = jnp.concatenate(
            [jnp.broadcast_to(gcol[(c + 1) * ch - 1:(c + 1) * ch, :], (ch, dk)) for c in range(nch)], axis=0)
        egc = jnp.exp(gcol)
        qh = qkv_s[:, hs]
        kh = qkv_s[:, dn + h * dk:dn + (h + 1) * dk]
        vh = qkv_s[:, 2 * dn + h * dk:2 * dn + (h + 1) * dk]
        qh = qh * (lax.rsqrt(jnp.sum(qh * qh, axis=-1, keepdims=True) + NORM_EPS) * (dk ** -0.5))
        kh = kh * lax.rsqrt(jnp.sum(kh * kh, axis=-1, keepdims=True) + NORM_EPS)
        kb = kh * beta
        gcol_s[wr_slot, :, hs] = gcol
        q_s[wr_slot, :, hs] = qh.astype(BF16)
        k_s[wr_slot, :, hs] = kh.astype(BF16)
        kb_s[wr_slot, :, hs] = kb.astype(BF16)
        qd_s[wr_slot, :, hs] = (qh * egc).astype(BF16)
        kd_s[wr_slot, :, hs] = (kh * jnp.exp(glast - gcol)).astype(BF16)
        vb_s[wr_slot, :, hs] = vh * beta
        wr_s[wr_slot, :, hs] = kb * egc

    front.append((2, f_norm))
    per_part = dn // QKV_CHUNK
    for c in [part * per_part + j for j in range(per_part) for part in range(3)]:
        front.append((20.0 * QKV_CHUNK / 1024, functools.partial(f_qkv, c)))
    front.append((3, f_gates))
    for h in range(nh):
        front.append((3, functools.partial(f_head, h)))
    for j in range(dn // OUT_CHUNK):
        front.append((5.0 * OUT_CHUNK / dn, functools.partial(f_z, j)))

    def iota2(shape, d):
        return lax.broadcasted_iota(jnp.int32, shape, d)

    pi, pj = iota2((ch, pw), 0), iota2((ch, pw), 1) % ch
    eye_p = pi == pj
    incl_p = pi >= pj
    strict_p = pi > pj
    eye_f = eye_p.astype(F32)
    head_of_lane = iota2((ch, pw), 1) // ch
    bd_p = (iota2((pw, pw), 0) // ch) == (iota2((pw, pw), 1) // ch)
    bd_k = (iota2((2 * ch, 2 * dk), 0) // ch) == (iota2((2 * ch, 2 * dk), 1) // dk)
    bd_u = (iota2((pw, gw), 0) // ch) == (iota2((pw, gw), 1) // dk)
    bd_s = (iota2((2 * dk, 2 * dk), 0) // dk) == (iota2((2 * dk, 2 * dk), 1) // dk)

    def expand(mc):
        return jnp.where(bd_p, jnp.concatenate([mc] * gh, axis=0), 0.0).astype(BF16)

    units = [(c, gi) for c in range(nch) for gi in range(nh // gh)]
    n_cs, m_cs, p_cs = [], [None] * len(units), [None] * len(units)

    def b_scores(c, gi):
        rows = slice(c * ch, (c + 1) * ch)
        h0 = gi * gh
        sc = []
        for p in range(gh // 2):
            ps = slice((h0 + 2 * p) * dk, (h0 + 2 * p + 2) * dk)
            kc = k_s[rd_slot, rows, ps]
            kbd = jnp.where(bd_k, jnp.concatenate([kc, kc], axis=0), jnp.zeros((), BF16))
            sc.append(lax.dot_general(jnp.concatenate([kb_s[rd_slot, rows, ps], q_s[rd_slot, rows, ps]], axis=0),
                                      kbd, (((1,), (1,)), ((), ())), preferred_element_type=F32))
        sc = jnp.concatenate(sc, axis=1)
        gcol_p = jnp.broadcast_to(gcol_s[rd_slot, rows, h0 * dk:h0 * dk + 1], (ch, pw))
        for hl in range(1, gh):
            gcol_p = jnp.where(
                head_of_lane == hl,
                jnp.broadcast_to(gcol_s[rd_slot, rows, (h0 + hl) * dk:(h0 + hl) * dk + 1], (ch, pw)), gcol_p)
        grow_p = jnp.sum(jnp.where(eye_p, gcol_p, 0.0), axis=0, keepdims=True)
        decay = jnp.exp(jnp.where(incl_p, gcol_p - grow_p, -jnp.inf))
        n_cs.append(jnp.where(strict_p, sc[0:ch] * decay, 0.0))
        qk_s[c, gi] = (sc[ch:2 * ch] * decay).astype(BF16)

    def b_square(u):
        m_cs[u] = jnp.dot(n_cs[u].astype(BF16), expand(n_cs[u]), preferred_element_type=F32)
        p_cs[u] = eye_f - n_cs[u]

    def b_level(u, last):
        m_bd = expand(m_cs[u])
        if last:
            p_cs[u] = p_cs[u] + jnp.dot(p_cs[u].astype(BF16), m_bd, preferred_element_type=F32)
        else:
            both = jnp.dot(jnp.concatenate([m_cs[u], p_cs[u]], axis=0).astype(BF16), m_bd,
                           preferred_element_type=F32)
            m_cs[u] = both[0:ch]
            p_cs[u] = p_cs[u] + both[ch:2 * ch]

    def b_apply(u):
        c, gi = units[u]
        rows = slice(c * ch, (c + 1) * ch)
        h0 = gi * gh
        rhs = jnp.concatenate(
            [jnp.concatenate([vb_s[rd_slot, rows, (h0 + hl) * dk:(h0 + hl + 1) * dk],
                              wr_s[rd_slot, rows, (h0 + hl) * dk:(h0 + hl + 1) * dk]], axis=1)
             for hl in range(gh)], axis=0)
        uw = rhs + jnp.dot(expand(p_cs[u] - eye_f), rhs.astype(BF16), preferred_element_type=F32)
        for hl in range(gh):
            hs = slice((h0 + hl) * dk, (h0 + hl + 1) * dk)
            uin_s[rows, hs] = uw[hl * ch:(hl + 1) * ch, 0:dk]
            win_s[rows, hs] = uw[hl * ch:(hl + 1) * ch, dk:2 * dk].astype(BF16)

    def b_recur(c, gi):
        rows = slice(c * ch, (c + 1) * ch)
        last = (c + 1) * ch - 1
        gs = slice(gi * gw, (gi + 1) * gw)
        u_parts, qs_parts = [], []
        for p in range(gh // 2):
            ps = slice(gi * gw + 2 * p * dk, gi * gw + (2 * p + 2) * dk)
            s_pair = s_ref[:, ps]
            s_bd = jnp.where(bd_s, jnp.concatenate([s_pair, s_pair], axis=0), 0.0).astype(BF16)
            r = jnp.dot(jnp.concatenate([win_s[rows, ps], qd_s[rd_slot, rows, ps]], axis=0), s_bd,
                        preferred_element_type=F32)
            u_parts.append(uin_s[rows, ps] - r[0:ch])
            qs_parts.append(r[ch:2 * ch])
        u_g = jnp.concatenate(u_parts, axis=1)
        u_bd = jnp.where(bd_u, jnp.concatenate([u_g] * gh, axis=0), 0.0).astype(BF16)
        o_s[rows, gs] = jnp.concatenate(qs_parts, axis=1) + jnp.dot(qk_s[c, gi], u_bd,
                                                                      preferred_element_type=F32)
        kd_stack = jnp.concatenate(
            [kd_s[rd_slot, rows, gi * gw + hl * dk:gi * gw + (hl + 1) * dk] for hl in range(gh)], axis=0)
        s_ref[:, gs] = (s_ref[:, gs] * jnp.exp(gcol_s[rd_slot, last:last + 1, gs])
                        + lax.dot_general(kd_stack, u_bd, (((0,), (0,)), ((), ())),
                                          preferred_element_type=F32))

    def b_gate(h):
        hs = slice(h * dk, (h + 1) * dk)
        og_s[:, hs] = (_rms(o_s[:, hs], onw_ref[...]) * z_s[rd_slot, :, hs]).astype(BF16)

    def b_proj(j):
        cols = slice(j * OUT_CHUNK, (j + 1) * OUT_CHUNK)
        xo_ref[0, :, cols] = xb_ref[0, :, cols] + jnp.dot(og_s[...], wout_ref[:, cols],
                                                         preferred_element_type=F32)

    def b_state():
        for h in range(nh):
            ssm_ref[0, h] = s_ref[:, h * dk:(h + 1) * dk]

    for c, gi in units:
        back.append((2, functools.partial(b_scores, c, gi)))
    for u in range(len(units)):
        back.append((1, functools.partial(b_square, u)))
    sq = 2
    while sq < ch:
        for u in range(len(units)):
            back.append((1, functools.partial(b_level, u, 2 * sq >= ch)))
        sq *= 2
    for u in range(len(units)):
        back.append((1.5, functools.partial(b_apply, u)))
    for c in range(nch):
        for gi in range(nh // gh):
            back.append((3, functools.partial(b_recur, c, gi)))
    for h in range(nh):
        back.append((0.5, functools.partial(b_gate, h)))
    for j in range(d_model // OUT_CHUNK):
        back.append((1, functools.partial(b_proj, j)))
    back.append((0.5, b_state))

    _interleave([front, back])


def _ffn_sample_chunk(xn_ref, wg_ref, wv_ref, cwg_ref, cwv_ref, cbg_ref, cbv_ref, wdn_ref, stg_ref, stv_ref,
                      sto_ref):
    sc = wg_ref.shape[1]
    dff = sto_ref.shape[2] // 2
    conv = []
    for half, (w_ref, cw_ref, cb_ref, st_ref) in enumerate(((wg_ref, cwg_ref, cbg_ref, stg_ref),
                                                            (wv_ref, cwv_ref, cbv_ref, stv_ref))):
        h = jnp.dot(xn_ref[...], w_ref[...], preferred_element_type=F32)
        prev = st_ref[:, 1, :]
        conv.append(st_ref[:, 0, :] * cw_ref[0:1, :] + prev * cw_ref[1:2, :] + h * cw_ref[2:3, :] + cb_ref[...])
        for cc in range(dff // sc):
            @pl.when(pl.program_id(0) == cc)
            def _(half=half, cc=cc, prev=prev, h=h):
                cols = slice(half * dff + cc * sc, half * dff + (cc + 1) * sc)
                sto_ref[:, 0, cols] = prev
                sto_ref[:, 1, cols] = h
    a = (_silu(conv[0]) * conv[1]).astype(BF16)
    return jnp.dot(a, wdn_ref[...], preferred_element_type=F32)


def _sample_layer0_kernel(x_ref, pb_ref, n1_ref, pw_ref, ps_ref, n2_ref,
                          wg_ref, wv_ref, cwg_ref, cwv_ref, cbg_ref, cbv_ref, wdn_ref, stg_ref, stv_ref,
                          xo_ref, ho_ref, sto_ref, xn_s):
    c = pl.program_id(0)
    nbuf = pb_ref.shape[0]
    gd = pw_ref.shape[1]

    @pl.when(c == 0)
    def _():
        x = x_ref[...]
        h = _rms(x, n1_ref[...])
        parts = []
        for g, w in enumerate(POOL_WINDOWS):
            cols = slice(g * gd, (g + 1) * gd)
            s = h[:, cols]
            for j in range(1, w):
                s = s + pb_ref[nbuf - j, :, cols]
            dg = s / float(w) - h[:, cols]
            parts.append(x[:, cols] + _dot(dg, pw_ref[g]) * ps_ref[:, cols])
        x1 = jnp.concatenate(parts, axis=1)
        ho_ref[...] = h
        xn_s[...] = _rms(x1, n2_ref[...]).astype(BF16)
        xo_ref[...] = x1

    xo_ref[...] += _ffn_sample_chunk(xn_s, wg_ref, wv_ref, cwg_ref, cwv_ref, cbg_ref, cbv_ref, wdn_ref,
                                     stg_ref, stv_ref, sto_ref)


def _sample_dn_proj_kernel(x_ref, n1_ref, win_ref, wab_ref, cw_ref, alog_ref, dtb_ref,
                           selg_ref, selb_ref, cst_ref,
                           w_ref, qd_ref, k_ref, vb_ref, qk_ref, eg_ref, z_ref, pre_ref):
    dk = DN_HEAD_DIM
    dn = cw_ref.shape[1] // 3
    nh = dn // dk
    xb = _rms(x_ref[...], n1_ref[...]).astype(BF16)
    pre = jnp.dot(xb, win_ref[:, 0:3 * dn], preferred_element_type=F32)
    y = (cst_ref[0] * cw_ref[0:1, :] + cst_ref[1] * cw_ref[1:2, :] + cst_ref[2] * cw_ref[2:3, :]
         + pre * cw_ref[3:4, :])
    qkv = _silu(y)
    pre_ref[...] = pre
    z_ref[...] = jnp.dot(xb, win_ref[:, 3 * dn:4 * dn], preferred_element_type=F32)
    ab = jnp.dot(xb, wab_ref[...], preferred_element_type=F32)
    g = -jnp.exp(alog_ref[...]) * _softplus(ab + dtb_ref[...])
    eg = jnp.exp(_dot_exact_rhs(g, selg_ref[...]))
    betab = _dot_exact_rhs(1.0 / (1.0 + jnp.exp(-ab)), selb_ref[...])
    eg_ref[...] = eg
    for h in range(nh):
        hs = slice(h * dk, (h + 1) * dk)
        q = qkv[:, hs]
        k = qkv[:, dn + h * dk:dn + (h + 1) * dk]
        v = qkv[:, 2 * dn + h * dk:2 * dn + (h + 1) * dk]
        q = q * lax.rsqrt(jnp.sum(q * q, axis=-1, keepdims=True) + NORM_EPS) * (dk ** -0.5)
        k = k * lax.rsqrt(jnp.sum(k * k, axis=-1, keepdims=True) + NORM_EPS)
        kb = k * betab[:, hs]
        qk = jnp.sum(q.astype(BF16).astype(F32) * k.astype(BF16).astype(F32), axis=-1, keepdims=True)
        w_ref[:, hs] = kb * eg[:, hs]
        qd_ref[:, hs] = q * eg[:, hs]
        k_ref[:, hs] = k
        vb_ref[:, hs] = v * betab[:, hs]
        qk_ref[:, hs] = jnp.broadcast_to(qk, (q.shape[0], dk))


def _sample_dn_state_kernel(w_ref, qd_ref, k_ref, vb_ref, qk_ref, eg_ref, s_ref, o_ref, so_ref):
    dk = DN_HEAD_DIM
    nb, nh = s_ref.shape[0], s_ref.shape[1]
    eye = (lax.broadcasted_iota(jnp.int32, (dk, dk), 0) == lax.broadcasted_iota(jnp.int32, (dk, dk), 1))
    for b in range(nb):
        for h in range(nh):
            hs = slice(h * dk, (h + 1) * dk)
            s = s_ref[b, h]
            lhs = jnp.concatenate([jnp.broadcast_to(w_ref[b:b + 1, hs], (8, dk)),
                                   jnp.broadcast_to(qd_ref[b:b + 1, hs], (8, dk))], axis=0)
            r = _dot(lhs, s)
            u = vb_ref[b:b + 1, hs] - r[0:1]
            ub = u.astype(BF16).astype(F32)
            o_ref[b:b + 1, hs] = r[8:9] + qk_ref[b:b + 1, hs] * ub
            kdiag = jnp.where(eye, jnp.broadcast_to(k_ref[b:b + 1, hs], (dk, dk)), 0.0)
            so_ref[b, h] = s * eg_ref[b:b + 1, hs] + _dot(kdiag, jnp.broadcast_to(u, (dk, dk)))


def _sample_tail_kernel(x_ref, o_ref, z_ref, onw_ref, wout_ref, n2_ref,
                        wg_ref, wv_ref, cwg_ref, cwv_ref, cbg_ref, cbv_ref, wdn_ref, stg_ref, stv_ref, fn_ref,
                        yo_ref, sto_ref, xn_s, acc_s):
    c = pl.program_id(0)
    dk = DN_HEAD_DIM

    @pl.when(c == 0)
    def _():
        outs = []
        for h in range(o_ref.shape[1] // dk):
            outs.append(_rms(o_ref[:, h * dk:(h + 1) * dk], onw_ref[...]))
        o = jnp.concatenate(outs, axis=1) * _silu(z_ref[...])
        x = x_ref[...] + _dot(o, wout_ref[...])
        acc_s[...] = x
        xn_s[...] = _rms(x, n2_ref[...]).astype(BF16)

    acc_s[...] += _ffn_sample_chunk(xn_s, wg_ref, wv_ref, cwg_ref, cwv_ref, cbg_ref, cbv_ref, wdn_ref,
                                    stg_ref, stv_ref, sto_ref)

    @pl.when(c == pl.num_programs(0) - 1)
    def _():
        yo_ref[...] = _rms(acc_s[...], fn_ref[...])


def _params(*sem):
    return pltpu.CompilerParams(dimension_semantics=sem, vmem_limit_bytes=VMEM_LIMIT)


def kernel(x_prompt, x_sample, state_pool_buf, state_dn_conv, state_dn_ssm, state_ffn_conv, norm1_w, norm2_w,
           final_norm_w, pool_w, pool_scale, dn_w_in, dn_conv_w, dn_a_log, dn_dt_bias, dn_o_norm_w, dn_w_out,
           ffn_w_up, ffn_conv_w, ffn_conv_b, ffn_w_down):
    bp, seq, d = x_prompt.shape
    bs = x_sample.shape[0]
    nbuf = state_pool_buf.shape[2]
    dff = ffn_w_down.shape[1]
    nh = dn_a_log.shape[1]
    dk = DN_HEAD_DIM
    dn = nh * dk
    gd = pool_w.shape[2]
    tm = PROMPT_TILE
    nt = seq // tm
    nfc = dff // FFN_CHUNK
    lanes = V7X_LANES
    assert seq % tm == 0 and dff % FFN_CHUNK == 0 and (3 * dn) % QKV_CHUNK == 0 and dk == lanes
    assert nbuf + 1 == max(POOL_WINDOWS) and 2 * nh <= lanes and bs % SAMPLE_BLOCK == 0

    row = lambda v: v.reshape(1, -1)
    wup = ffn_w_up.astype(BF16)
    wdn = ffn_w_down.astype(BF16)
    pw = pool_w.astype(BF16)
    w_in = dn_w_in[0]
    win_all = dn_w_in.astype(BF16)
    wab = jnp.pad(w_in[:, 4 * dn:], ((0, 0), (0, lanes - 2 * nh))).astype(BF16)
    wout = dn_w_out[0].astype(BF16)
    alog = jnp.pad(dn_a_log[0], (0, lanes - nh)).reshape(1, lanes)
    dtb = jnp.pad(dn_dt_bias[0], (0, lanes - nh)).reshape(1, lanes)
    onw = row(dn_o_norm_w[0])
    lane_head = jnp.arange(dn) // dk
    selg = (jnp.arange(lanes)[:, None] == lane_head[None, :]).astype(BF16)
    selb = (jnp.arange(lanes)[:, None] == (lane_head[None, :] + nh)).astype(BF16)
    ti = jnp.arange(tm)
    ltri = ((ti[:, None] // DN_CHUNK == ti[None, :] // DN_CHUNK) & (ti[:, None] >= ti[None, :])).astype(BF16)

    tf = FFN_TILE
    assert seq % tf == 0
    x_spec = pl.BlockSpec((1, tf, d), lambda b, t: (b, t, 0))
    layer_resident = lambda shape, layer: pl.BlockSpec((None,) + shape, lambda *_: (layer,) + (0,) * len(shape),
                                                       pipeline_mode=pl.Buffered(1))
    ffn_w_specs = lambda layer: [layer_resident((d, 2 * dff), layer), _resident((3, 2 * dff)),
                                 _resident((1, 2 * dff)), layer_resident((dff, d), layer)]
    fst_spec = pl.BlockSpec((1, CONV_HALO, 2 * dff), lambda b, t: (b, 0, 0))
    fst_shape = jax.ShapeDtypeStruct((bp, CONV_HALO, 2 * dff), F32)

    x2, pool_tail, fst0 = pl.pallas_call(
        _prompt_layer0_kernel,
        grid=(bp, seq // tf),
        in_specs=[x_spec, _resident((1, d)), _resident(pw.shape[1:]), _resident((1, d)), _resident((1, d))]
        + ffn_w_specs(0),
        out_specs=[x_spec, pl.BlockSpec((1, POOL_HALO, d), lambda b, t: (b, 0, 0)), fst_spec],
        out_shape=[jax.ShapeDtypeStruct((bp, seq, d), F32), jax.ShapeDtypeStruct((bp, POOL_HALO, d), F32), fst_shape],
        scratch_shapes=[pltpu.VMEM((POOL_HALO + tf, d), F32), pltpu.VMEM((CONV_HALO, 2 * dff), F32)],
        compiler_params=_params("arbitrary", "arbitrary"),
        name="prompt_layer0",
    )(x_prompt, row(norm1_w[0]), pw[0], row(pool_scale[0]), row(norm2_w[0]),
      wup, ffn_conv_w[0], row(ffn_conv_b[0]), wdn)

    ntiles = bp * nt
    front_tile = lambda i: jnp.minimum(i, ntiles - 1)
    back_tile = lambda i: jnp.maximum(i - 1, 0)
    x_front = pl.BlockSpec((1, tm, d), lambda i: (front_tile(i) // nt, front_tile(i) % nt, 0))
    x_back = pl.BlockSpec((1, tm, d), lambda i: (back_tile(i) // nt, back_tile(i) % nt, 0))
    x3, cst, ssm = pl.pallas_call(
        functools.partial(_prompt_deltanet_kernel, tiles_per_seq=nt),
        grid=(ntiles + 1,),
        in_specs=[x_front, x_back,
                  _resident((1, d)), layer_resident((d, win_all.shape[2]), 0), _resident((d, lanes)),
                  _resident((4, 3 * dn)), _resident((1, lanes)), _resident((1, lanes)), _resident((1, dk)),
                  _resident((dn, d)), _resident((tm, tm))],
        out_specs=[x_back,
                   pl.BlockSpec((1, CONV_HALO, 3 * dn), lambda i: (front_tile(i) // nt, 0, 0)),
                   pl.BlockSpec((1, nh, dk, dk), lambda i: (back_tile(i) // nt, 0, 0, 0))],
        out_shape=[jax.ShapeDtypeStruct((bp, seq, d), F32),
                   jax.ShapeDtypeStruct((bp, CONV_HALO, 3 * dn), F32),
                   jax.ShapeDtypeStruct((bp, nh, dk, dk), F32)],
        scratch_shapes=[pltpu.VMEM((CONV_HALO, 3 * dn), F32), pltpu.VMEM((dk, dn), F32),
                        pltpu.VMEM((tm, 3 * dn), F32)]
        + [pltpu.VMEM((2, tm, dn), BF16)] * 5 + [pltpu.VMEM((2, tm, dn), F32)] * 4
        + [pltpu.VMEM((tm, dn), F32), pltpu.VMEM((tm, dn), BF16),
           pltpu.VMEM((tm // DN_CHUNK, nh // HEAD_GROUP, DN_CHUNK, HEAD_GROUP * DN_CHUNK), BF16),
           pltpu.VMEM((tm, dn), F32), pltpu.VMEM((tm, dn), BF16)],
        compiler_params=_params("arbitrary"),
        name="prompt_deltanet",
    )(x2, x2, row(norm1_w[1]), win_all, wab, dn_conv_w[0], alog, dtb, onw, wout, ltri)

    y_prompt, fst1 = pl.pallas_call(
        _prompt_ffn_final_kernel,
        grid=(bp, seq // tf),
        in_specs=[x_spec, _resident((1, d))] + ffn_w_specs(1) + [_resident((1, d))],
        out_specs=[x_spec, fst_spec],
        out_shape=[jax.ShapeDtypeStruct((bp, seq, d), F32), fst_shape],
        scratch_shapes=[pltpu.VMEM((CONV_HALO, 2 * dff), F32)],
        compiler_params=_params("arbitrary", "arbitrary"),
        name="prompt_ffn_final",
    )(x3, row(norm2_w[1]), wup, ffn_conv_w[1], row(ffn_conv_b[1]), wdn, row(final_norm_w))

    xs = x_sample[:, 0, :]
    pb = jnp.transpose(state_pool_buf[0], (1, 0, 2))
    cs = jnp.transpose(state_dn_conv[0], (1, 0, 2))
    sc = SAMPLE_FFN_CHUNK
    nsc = dff // sc
    assert dff % sc == 0

    def ffn_chunk_specs(layer):
        specs = []
        for shape in ((None, d, sc), (3, sc), (1, sc)):
            for half in range(2):
                if shape[0] is None:
                    specs.append(pl.BlockSpec(shape, lambda c, half=half: (layer, 0, half * nsc + c)))
                else:
                    specs.append(pl.BlockSpec(shape, lambda c, half=half: (0, half * nsc + c)))
        specs.append(pl.BlockSpec((None, sc, d), lambda c: (layer, c, 0)))
        for half in range(2):
            specs.append(pl.BlockSpec((None, bs, 2, sc), lambda c, half=half: (layer, 0, 0, half * nsc + c)))
        return specs

    sto_spec = _full((bs, 2, 2 * dff))
    sto_shape = jax.ShapeDtypeStruct((bs, 2, 2 * dff), F32)

    def ffn_chunk_args(layer):
        cb = row(ffn_conv_b[layer])
        return (wup, wup, ffn_conv_w[layer], ffn_conv_w[layer], cb, cb, wdn, state_ffn_conv, state_ffn_conv)

    xs2, pool_row, fso0 = pl.pallas_call(
        _sample_layer0_kernel,
        grid=(nsc,),
        in_specs=[_resident((bs, d)), _resident((nbuf, bs, d)), _resident((1, d)), _resident(pw.shape[1:]),
                  _resident((1, d)), _resident((1, d))] + ffn_chunk_specs(0),
        out_specs=[_full((bs, d)), _full((bs, d)), sto_spec],
        out_shape=[jax.ShapeDtypeStruct((bs, d), F32), jax.ShapeDtypeStruct((bs, d), F32), sto_shape],
        scratch_shapes=[pltpu.VMEM((bs, d), BF16)],
        compiler_params=_params("arbitrary"),
        name="sample_layer0",
    )(xs, pb, row(norm1_w[0]), pw[0], row(pool_scale[0]), row(norm2_w[0]), *ffn_chunk_args(0))

    vec = jax.ShapeDtypeStruct((bs, dn), F32)
    w_s, qd_s, k_s, vb_s, qk_s, eg_s, z_s, conv_row = pl.pallas_call(
        _sample_dn_proj_kernel,
        out_shape=[vec] * 7 + [jax.ShapeDtypeStruct((bs, 3 * dn), F32)],
        compiler_params=pltpu.CompilerParams(vmem_limit_bytes=VMEM_LIMIT),
        name="sample_dn_proj",
    )(xs2, row(norm1_w[1]), win_all[0], wab, dn_conv_w[0], alog, dtb, selg, selb, cs)

    nb = SAMPLE_BLOCK
    vspec = pl.BlockSpec((nb, dn), lambda i: (i, 0))
    sspec = pl.BlockSpec((nb, nh, dk, dk), lambda i: (i, 0, 0, 0))
    o_s, ssm_s = pl.pallas_call(
        _sample_dn_state_kernel,
        grid=(bs // nb,),
        in_specs=[vspec] * 6 + [sspec],
        out_specs=[vspec, sspec],
        out_shape=[vec, jax.ShapeDtypeStruct((bs, nh, dk, dk), F32)],
        compiler_params=_params("arbitrary"),
        name="sample_dn_state",
    )(w_s, qd_s, k_s, vb_s, qk_s, eg_s, state_dn_ssm[0])

    ys, fso1 = pl.pallas_call(
        _sample_tail_kernel,
        grid=(nsc,),
        in_specs=[_resident((bs, d)), _resident((bs, dn)), _resident((bs, dn)), _resident((1, dk)),
                  _resident((dn, d)), _resident((1, d))] + ffn_chunk_specs(1) + [_resident((1, d))],
        out_specs=[_full((bs, d)), sto_spec],
        out_shape=[jax.ShapeDtypeStruct((bs, d), F32), sto_shape],
        scratch_shapes=[pltpu.VMEM((bs, d), BF16), pltpu.VMEM((bs, d), F32)],
        compiler_params=_params("arbitrary"),
        name="sample_tail",
    )(xs2, o_s, z_s, onw, wout, row(norm2_w[1]), *ffn_chunk_args(1), row(final_norm_w))

    shifted = lambda state, new_row: jnp.concatenate([state[:, 1:], new_row[:, None]], axis=1)
    return (
        y_prompt,
        ys[:, None, :],
        pool_tail[None, :, POOL_HALO - nbuf:, :],
        shifted(state_pool_buf[0], pool_row)[None],
        cst[None, :, CONV_HALO - 3:, :],
        shifted(state_dn_conv[0], conv_row)[None],
        ssm[None],
        ssm_s[None],
        jnp.stack([fst0[:, CONV_HALO - 2:, :], fst1[:, CONV_HALO - 2:, :]]),
        jnp.stack([fso0, fso1]),
    )
```

```python
import functools

import jax
import jax.numpy as jnp
from jax import lax
from jax.experimental import pallas as pl
from jax.experimental.pallas import tpu as pltpu

F32 = jnp.float32
BF16 = jnp.bfloat16

NORM_EPS = 1e-6
POOL_WINDOWS = (2, 4, 8, 16)
POOL_HALO = 16
CONV_HALO = 8
DN_HEAD_DIM = 128
DN_CHUNK = 64
V7X_LANES = 128
V7X_MXU_WIDTH = 256
V7X_VMEM_BYTES = 64 * 1024 * 1024
FFN_CHUNK = V7X_MXU_WIDTH
FFN_DOWN_GROUP = 4
FFN_LOOKAHEAD = 3
QKV_CHUNK = V7X_MXU_WIDTH
OUT_CHUNK = V7X_MXU_WIDTH
PROMPT_TILE = 256
FFN_TILE = 512
HEAD_GROUP = 4
SAMPLE_FFN_CHUNK = 1408
SAMPLE_BLOCK = 16
VMEM_LIMIT = V7X_VMEM_BYTES * 7 // 8


def _dot(a, b):
    return jnp.dot(a.astype(BF16), b.astype(BF16), preferred_element_type=F32)


def _split3(x):
    hi = x.astype(BF16)
    r = x - hi.astype(F32)
    mid = r.astype(BF16)
    lo = (r - mid.astype(F32)).astype(BF16)
    return hi, mid, lo


def _dot_exact_rhs(x, e):
    hi, mid, lo = _split3(x)
    return (jnp.dot(lo, e, preferred_element_type=F32) + jnp.dot(mid, e, preferred_element_type=F32)
            + jnp.dot(hi, e, preferred_element_type=F32))


def _dot_exact_lhs(e, x):
    hi, mid, lo = _split3(x)
    return (jnp.dot(e, lo, preferred_element_type=F32) + jnp.dot(e, mid, preferred_element_type=F32)
            + jnp.dot(e, hi, preferred_element_type=F32))


def _rms(x, w):
    return x * lax.rsqrt(jnp.mean(x * x, axis=-1, keepdims=True) + NORM_EPS) * w


def _silu(x):
    h = 0.5 * x
    return h + h * jnp.tanh(h)


def _softplus(x):
    return jnp.maximum(x, 0.0) + jnp.log1p(jnp.exp(-jnp.abs(x)))


def _full(shape):
    return pl.BlockSpec(shape, lambda *_: (0,) * len(shape))


def _resident(shape):
    return pl.BlockSpec(shape, lambda *_: (0,) * len(shape), pipeline_mode=pl.Buffered(1))


def _ffn_tile(xn_b, wup_ref, cw_ref, cb_ref, wdn_ref, carry_ref):
    tm = xn_b.shape[0]
    dff = wdn_ref.shape[0]
    nchunk = dff // FFN_CHUNK

    def up(c):
        return [jnp.dot(xn_b, wup_ref[:, half * dff + c * FFN_CHUNK:half * dff + (c + 1) * FFN_CHUNK],
                        preferred_element_type=F32) for half in range(2)]

    def conv_act(c, hs):
        conv = []
        for half, h in enumerate(hs):
            cols = slice(half * dff + c * FFN_CHUNK, half * dff + (c + 1) * FFN_CHUNK)
            ext = jnp.concatenate([carry_ref[:, cols], h], axis=0)
            y = (ext * cw_ref[2:3, cols] + pltpu.roll(ext, 1, 0) * cw_ref[1:2, cols]
                 + pltpu.roll(ext, 2, 0) * cw_ref[0:1, cols])
            conv.append(y[CONV_HALO:] + cb_ref[:, cols])
            carry_ref[:, cols] = h[tm - CONV_HALO:]
        return (_silu(conv[0]) * conv[1]).astype(BF16)

    acc = None
    acts = []
    ahead = [up(c) for c in range(min(FFN_LOOKAHEAD, nchunk))]
    for c in range(nchunk):
        cur = ahead.pop(0)
        if c + FFN_LOOKAHEAD < nchunk:
            ahead.append(up(c + FFN_LOOKAHEAD))
        acts.append(conv_act(c, cur))
        if len(acts) == FFN_DOWN_GROUP or c + 1 == nchunk:
            c0 = c + 1 - len(acts)
            a = acts[0] if len(acts) == 1 else jnp.concatenate(acts, axis=1)
            d = jnp.dot(a, wdn_ref[c0 * FFN_CHUNK:(c + 1) * FFN_CHUNK, :], preferred_element_type=F32)
            acc = d if acc is None else acc + d
            acts = []
    return acc


def _prompt_layer0_kernel(x_ref, n1_ref, pw_ref, ps_ref, n2_ref, wup_ref, cw_ref, cb_ref, wdn_ref,
                          xo_ref, pool_ref, fst_ref, hbuf, carry):
    t = pl.program_id(1)
    tm = x_ref.shape[1]
    gd = pw_ref.shape[1]

    @pl.when(t == 0)
    def _():
        hbuf[0:POOL_HALO, :] = jnp.zeros((POOL_HALO, hbuf.shape[1]), F32)
        carry[...] = jnp.zeros(carry.shape, F32)

    x = x_ref[0]
    h = _rms(x, n1_ref[...])
    hbuf[POOL_HALO:POOL_HALO + tm, :] = h
    pos1 = t * tm + lax.broadcasted_iota(jnp.int32, (tm, 1), 0) + 1
    parts = []
    for g, w in enumerate(POOL_WINDOWS):
        cols = slice(g * gd, (g + 1) * gd)
        s = hbuf[:, cols]
        sh = 1
        while sh < w:
            s = s + pltpu.roll(s, sh, 0)
            sh *= 2
        cnt = jnp.minimum(w, pos1).astype(F32)
        dg = s[POOL_HALO:] / cnt - h[:, cols]
        parts.append(x[:, cols] + _dot(dg, pw_ref[g]) * ps_ref[:, cols])
    x1 = jnp.concatenate(parts, axis=1)
    tail = hbuf[tm:tm + POOL_HALO, :]
    pool_ref[0] = tail
    hbuf[0:POOL_HALO, :] = tail

    xn = _rms(x1, n2_ref[...]).astype(BF16)
    xo_ref[0] = x1 + _ffn_tile(xn, wup_ref, cw_ref, cb_ref, wdn_ref, carry)
    fst_ref[0] = carry[...]


def _prompt_ffn_final_kernel(x_ref, n2_ref, wup_ref, cw_ref, cb_ref, wdn_ref, fn_ref,
                             yo_ref, fst_ref, carry):
    @pl.when(pl.program_id(1) == 0)
    def _():
        carry[...] = jnp.zeros(carry.shape, F32)

    x = x_ref[0]
    xn = _rms(x, n2_ref[...]).astype(BF16)
    x = x + _ffn_tile(xn, wup_ref, cw_ref, cb_ref, wdn_ref, carry)
    yo_ref[0] = _rms(x, fn_ref[...])
    fst_ref[0] = carry[...]


def _interleave(streams):
    merged = []
    for si, steps in enumerate(streams):
        total = float(sum(w for w, _ in steps))
        done = 0.0
        for w, thunk in steps:
            merged.append(((done + 0.5 * w) / total, si, len(merged), thunk))
            done += w
    for _, _, _, thunk in sorted(merged, key=lambda m: m[:3]):
        thunk()


def _prompt_deltanet_kernel(xa_ref, xb_ref, n1_ref, win_ref, wab_ref, cw_ref, alog_ref, dtb_ref, onw_ref,
                            wout_ref, ltri_ref,
                            xo_ref, cst_ref, ssm_ref,
                            carry, s_ref, qkv_s, kb_s, q_s, k_s, qd_s, kd_s, vb_s, wr_s, gcol_s, z_s, uin_s, win_s,
                            qk_s, o_s, og_s, *, tiles_per_seq):
    i = pl.program_id(0)
    tm = xa_ref.shape[1]
    dk = DN_HEAD_DIM
    dn = s_ref.shape[1]
    nh = dn // dk
    ch = DN_CHUNK
    nch = tm // ch
    gh = HEAD_GROUP
    gw = gh * dk
    pw = gh * ch
    d_model = xa_ref.shape[2]
    wr_slot = i % 2
    rd_slot = 1 - wr_slot

    @pl.when(i == 0)
    def _():
        for ref in (kb_s, q_s, k_s, qd_s, kd_s, vb_s, wr_s, gcol_s, z_s):
            ref[1] = jnp.zeros(ref.shape[1:], ref.dtype)
        s_ref[...] = jnp.zeros(s_ref.shape, F32)

    @pl.when(i % tiles_per_seq == 0)
    def _():
        carry[...] = jnp.zeros(carry.shape, F32)

    @pl.when((i + tiles_per_seq - 1) % tiles_per_seq == 0)
    def _():
        s_ref[...] = jnp.zeros(s_ref.shape, F32)

    front, back = [], []
    env = {}

    def f_norm():
        env["xb"] = _rms(xa_ref[0], n1_ref[...]).astype(BF16)

    def f_qkv(c):
        cols = slice(c * QKV_CHUNK, (c + 1) * QKV_CHUNK)
        pre = jnp.dot(env["xb"], win_ref[:, cols], preferred_element_type=F32)
        ext = jnp.concatenate([carry[:, cols], pre], axis=0)
        prev = pltpu.roll(ext, 1, 0)
        y = (ext * cw_ref[3:4, cols] + prev * cw_ref[2:3, cols]
             + pltpu.roll(ext * cw_ref[1:2, cols] + prev * cw_ref[0:1, cols], 2, 0))
        qkv_s[:, cols] = _silu(y[CONV_HALO:])
        carry[:, cols] = pre[tm - CONV_HALO:]
        cst_ref[0, :, cols] = pre[tm - CONV_HALO:]

    def f_gates():
        ab = jnp.dot(env["xb"], wab_ref[...], preferred_element_type=F32)
        g = -jnp.exp(alog_ref[...]) * _softplus(ab + dtb_ref[...])
        env["gc"] = _dot_exact_lhs(ltri_ref[...], g)
        env["sig"] = 1.0 / (1.0 + jnp.exp(-ab))

    def f_z(j):
        cols = slice(j * OUT_CHUNK, (j + 1) * OUT_CHUNK)
        z = jnp.dot(env["xb"], win_ref[:, 3 * dn + cols.start:3 * dn + cols.stop], preferred_element_type=F32)
        z_s[wr_slot, :, cols] = _silu(z)

    def f_head(h):
        hs = slice(h * dk, (h + 1) * dk)
        gcol = jnp.broadcast_to(env["gc"][:, h:h + 1], (tm, dk))
        beta = jnp.broadcast_to(env["sig"][:, nh + h:nh + h + 1], (tm, dk))
        glast = jnp.concatenate(
            [jnp.broadcast_to(gcol[(c + 1) * ch - 1:(c + 1) * ch, :], (ch, dk)) for c in range(nch)], axis=0)
        egc = jnp.exp(gcol)
        qh = qkv_s[:, hs]
        kh = qkv_s[:, dn + h * dk:dn + (h + 1) * dk]
        vh = qkv_s[:, 2 * dn + h * dk:2 * dn + (h + 1) * dk]
        qh = qh * (lax.rsqrt(jnp.sum(qh * qh, axis=-1, keepdims=True) + NORM_EPS) * (dk ** -0.5))
        kh = kh * lax.rsqrt(jnp.sum(kh * kh, axis=-1, keepdims=True) + NORM_EPS)
        kb = kh * beta
        gcol_s[wr_slot, :, hs] = gcol
        q_s[wr_slot, :, hs] = qh.astype(BF16)
        k_s[wr_slot, :, hs] = kh.astype(BF16)
        kb_s[wr_slot, :, hs] = kb.astype(BF16)
        qd_s[wr_slot, :, hs] = (qh * egc).astype(BF16)
        kd_s[wr_slot, :, hs] = (kh * jnp.exp(glast - gcol)).astype(BF16)
        vb_s[wr_slot, :, hs] = vh * beta
        wr_s[wr_slot, :, hs] = kb * egc

    front.append((2, f_norm))
    per_part = dn // QKV_CHUNK
    for c in [part * per_part + j for j in range(per_part) for part in range(3)]:
        front.append((20.0 * QKV_CHUNK / 1024, functools.partial(f_qkv, c)))
    front.append((3, f_gates))
    for h in range(nh):
        front.append((3, functools.partial(f_head, h)))
    for j in range(dn // OUT_CHUNK):
        front.append((5.0 * OUT_CHUNK / dn, functools.partial(f_z, j)))

    def iota2(shape, d):
        return lax.broadcasted_iota(jnp.int32, shape, d)

    pi, pj = iota2((ch, pw), 0), iota2((ch, pw), 1) % ch
    eye_p = pi == pj
    incl_p = pi >= pj
    strict_p = pi > pj
    eye_f = eye_p.astype(F32)
    head_of_lane = iota2((ch, pw), 1) // ch
    bd_p = (iota2((pw, pw), 0) // ch) == (iota2((pw, pw), 1) // ch)
    bd_k = (iota2((2 * ch, 2 * dk), 0) // ch) == (iota2((2 * ch, 2 * dk), 1) // dk)
    bd_u = (iota2((pw, gw), 0) // ch) == (iota2((pw, gw), 1) // dk)
    bd_s = (iota2((2 * dk, 2 * dk), 0) // dk) == (iota2((2 * dk, 2 * dk), 1) // dk)

    def expand(mc):
        return jnp.where(bd_p, jnp.concatenate([mc] * gh, axis=0), 0.0).astype(BF16)

    units = [(c, gi) for c in range(nch) for gi in range(nh // gh)]
    n_cs, m_cs, p_cs = [], [None] * len(units), [None] * len(units)

    def b_scores(c, gi):
        rows = slice(c * ch, (c + 1) * ch)
        h0 = gi * gh
        sc = []
        for p in range(gh // 2):
            ps = slice((h0 + 2 * p) * dk, (h0 + 2 * p + 2) * dk)
            kc = k_s[rd_slot, rows, ps]
            kbd = jnp.where(bd_k, jnp.concatenate([kc, kc], axis=0), jnp.zeros((), BF16))
            sc.append(lax.dot_general(jnp.concatenate([kb_s[rd_slot, rows, ps], q_s[rd_slot, rows, ps]], axis=0),
                                      kbd, (((1,), (1,)), ((), ())), preferred_element_type=F32))
        sc = jnp.concatenate(sc, axis=1)
        gcol_p = jnp.broadcast_to(gcol_s[rd_slot, rows, h0 * dk:h0 * dk + 1], (ch, pw))
        for hl in range(1, gh):
            gcol_p = jnp.where(
                head_of_lane == hl,
                jnp.broadcast_to(gcol_s[rd_slot, rows, (h0 + hl) * dk:(h0 + hl) * dk + 1], (ch, pw)), gcol_p)
        grow_p = jnp.sum(jnp.where(eye_p, gcol_p, 0.0), axis=0, keepdims=True)
        decay = jnp.exp(jnp.where(incl_p, gcol_p - grow_p, -jnp.inf))
        n_cs.append(jnp.where(strict_p, sc[0:ch] * decay, 0.0))
        qk_s[c, gi] = (sc[ch:2 * ch] * decay).astype(BF16)

    def b_square(u):
        m_cs[u] = jnp.dot(n_cs[u].astype(BF16), expand(n_cs[u]), preferred_element_type=F32)
        p_cs[u] = eye_f - n_cs[u]

    def b_level(u, last):
        m_bd = expand(m_cs[u])
        if last:
            p_cs[u] = p_cs[u] + jnp.dot(p_cs[u].astype(BF16), m_bd, preferred_element_type=F32)
        else:
            both = jnp.dot(jnp.concatenate([m_cs[u], p_cs[u]], axis=0).astype(BF16), m_bd,
                           preferred_element_type=F32)
            m_cs[u] = both[0:ch]
            p_cs[u] = p_cs[u] + both[ch:2 * ch]

    def b_apply(u):
        c, gi = units[u]
        rows = slice(c * ch, (c + 1) * ch)
        h0 = gi * gh
        rhs = jnp.concatenate(
            [jnp.concatenate([vb_s[rd_slot, rows, (h0 + hl) * dk:(h0 + hl + 1) * dk],
                              wr_s[rd_slot, rows, (h0 + hl) * dk:(h0 + hl + 1) * dk]], axis=1)
             for hl in range(gh)], axis=0)
        uw = rhs + jnp.dot(expand(p_cs[u] - eye_f), rhs.astype(BF16), preferred_element_type=F32)
        for hl in range(gh):
            hs = slice((h0 + hl) * dk, (h0 + hl + 1) * dk)
            uin_s[rows, hs] = uw[hl * ch:(hl + 1) * ch, 0:dk]
            win_s[rows, hs] = uw[hl * ch:(hl + 1) * ch, dk:2 * dk].astype(BF16)

    def b_recur(c, gi):
        rows = slice(c * ch, (c + 1) * ch)
        last = (c + 1) * ch - 1
        gs = slice(gi * gw, (gi + 1) * gw)
        u_parts, qs_parts = [], []
        for p in range(gh // 2):
            ps = slice(gi * gw + 2 * p * dk, gi * gw + (2 * p + 2) * dk)
            s_pair = s_ref[:, ps]
            s_bd = jnp.where(bd_s, jnp.concatenate([s_pair, s_pair], axis=0), 0.0).astype(BF16)
            r = jnp.dot(jnp.concatenate([win_s[rows, ps], qd_s[rd_slot, rows, ps]], axis=0), s_bd,
                        preferred_element_type=F32)
            u_parts.append(uin_s[rows, ps] - r[0:ch])
            qs_parts.append(r[ch:2 * ch])
        u_g = jnp.concatenate(u_parts, axis=1)
        u_bd = jnp.where(bd_u, jnp.concatenate([u_g] * gh, axis=0), 0.0).astype(BF16)
        o_s[rows, gs] = jnp.concatenate(qs_parts, axis=1) + jnp.dot(qk_s[c, gi], u_bd,
                                                                      preferred_element_type=F32)
        kd_stack = jnp.concatenate(
            [kd_s[rd_slot, rows, gi * gw + hl * dk:gi * gw + (hl + 1) * dk] for hl in range(gh)], axis=0)
        s_ref[:, gs] = (s_ref[:, gs] * jnp.exp(gcol_s[rd_slot, last:last + 1, gs])
                        + lax.dot_general(kd_stack, u_bd, (((0,), (0,)), ((), ())),
                                          preferred_element_type=F32))

    def b_gate(h):
        hs = slice(h * dk, (h + 1) * dk)
        og_s[:, hs] = (_rms(o_s[:, hs], onw_ref[...]) * z_s[rd_slot, :, hs]).astype(BF16)

    def b_proj(j):
        cols = slice(j * OUT_CHUNK, (j + 1) * OUT_CHUNK)
        xo_ref[0, :, cols] = xb_ref[0, :, cols] + jnp.dot(og_s[...], wout_ref[:, cols],
                                                         preferred_element_type=F32)

    def b_state():
        for h in range(nh):
            ssm_ref[0, h] = s_ref[:, h * dk:(h + 1) * dk]

    for c, gi in units:
        back.append((2, functools.partial(b_scores, c, gi)))
    for u in range(len(units)):
        back.append((1, functools.partial(b_square, u)))
    sq = 2
    while sq < ch:
        for u in range(len(units)):
            back.append((1, functools.partial(b_level, u, 2 * sq >= ch)))
        sq *= 2
    for u in range(len(units)):
        back.append((1.5, functools.partial(b_apply, u)))
    for c in range(nch):
        for gi in range(nh // gh):
            back.append((3, functools.partial(b_recur, c, gi)))
    for h in range(nh):
        back.append((0.5, functools.partial(b_gate, h)))
    for j in range(d_model // OUT_CHUNK):
        back.append((1, functools.partial(b_proj, j)))
    back.append((0.5, b_state))

    _interleave([front, back])


def _ffn_sample_chunk(xn_ref, wg_ref, wv_ref, cwg_ref, cwv_ref, cbg_ref, cbv_ref, wdn_ref, stg_ref, stv_ref,
                      sto_ref):
    sc = wg_ref.shape[1]
    dff = sto_ref.shape[2] // 2
    conv = []
    for half, (w_ref, cw_ref, cb_ref, st_ref) in enumerate(((wg_ref, cwg_ref, cbg_ref, stg_ref),
                                                            (wv_ref, cwv_ref, cbv_ref, stv_ref))):
        h = jnp.dot(xn_ref[...], w_ref[...], preferred_element_type=F32)
        prev = st_ref[:, 1, :]
        conv.append(st_ref[:, 0, :] * cw_ref[0:1, :] + prev * cw_ref[1:2, :] + h * cw_ref[2:3, :] + cb_ref[...])
        for cc in range(dff // sc):
            @pl.when(pl.program_id(0) == cc)
            def _(half=half, cc=cc, prev=prev, h=h):
                cols = slice(half * dff + cc * sc, half * dff + (cc + 1) * sc)
                sto_ref[:, 0, cols] = prev
                sto_ref[:, 1, cols] = h
    a = (_silu(conv[0]) * conv[1]).astype(BF16)
    return jnp.dot(a, wdn_ref[...], preferred_element_type=F32)


def _sample_layer0_kernel(x_ref, pb_ref, n1_ref, pw_ref, ps_ref, n2_ref,
                          wg_ref, wv_ref, cwg_ref, cwv_ref, cbg_ref, cbv_ref, wdn_ref, stg_ref, stv_ref,
                          xo_ref, ho_ref, sto_ref, xn_s):
    c = pl.program_id(0)
    nbuf = pb_ref.shape[0]
    gd = pw_ref.shape[1]

    @pl.when(c == 0)
    def _():
        x = x_ref[...]
        h = _rms(x, n1_ref[...])
        parts = []
        for g, w in enumerate(POOL_WINDOWS):
            cols = slice(g * gd, (g + 1) * gd)
            s = h[:, cols]
            for j in range(1, w):
                s = s + pb_ref[nbuf - j, :, cols]
            dg = s / float(w) - h[:, cols]
            parts.append(x[:, cols] + _dot(dg, pw_ref[g]) * ps_ref[:, cols])
        x1 = jnp.concatenate(parts, axis=1)
        ho_ref[...] = h
        xn_s[...] = _rms(x1, n2_ref[...]).astype(BF16)
        xo_ref[...] = x1

    xo_ref[...] += _ffn_sample_chunk(xn_s, wg_ref, wv_ref, cwg_ref, cwv_ref, cbg_ref, cbv_ref, wdn_ref,
                                     stg_ref, stv_ref, sto_ref)


def _sample_dn_proj_kernel(x_ref, n1_ref, win_ref, wab_ref, cw_ref, alog_ref, dtb_ref,
                           selg_ref, selb_ref, cst_ref,
                           w_ref, qd_ref, k_ref, vb_ref, qk_ref, eg_ref, z_ref, pre_ref):
    dk = DN_HEAD_DIM
    dn = cw_ref.shape[1] // 3
    nh = dn // dk
    xb = _rms(x_ref[...], n1_ref[...]).astype(BF16)
    pre = jnp.dot(xb, win_ref[:, 0:3 * dn], preferred_element_type=F32)
    y = (cst_ref[0] * cw_ref[0:1, :] + cst_ref[1] * cw_ref[1:2, :] + cst_ref[2] * cw_ref[2:3, :]
         + pre * cw_ref[3:4, :])
    qkv = _silu(y)
    pre_ref[...] = pre
    z_ref[...] = jnp.dot(xb, win_ref[:, 3 * dn:4 * dn], preferred_element_type=F32)
    ab = jnp.dot(xb, wab_ref[...], preferred_element_type=F32)
    g = -jnp.exp(alog_ref[...]) * _softplus(ab + dtb_ref[...])
    eg = jnp.exp(_dot_exact_rhs(g, selg_ref[...]))
    betab = _dot_exact_rhs(1.0 / (1.0 + jnp.exp(-ab)), selb_ref[...])
    eg_ref[...] = eg
    for h in range(nh):
        hs = slice(h * dk, (h + 1) * dk)
        q = qkv[:, hs]
        k = qkv[:, dn + h * dk:dn + (h + 1) * dk]
        v = qkv[:, 2 * dn + h * dk:2 * dn + (h + 1) * dk]
        q = q * lax.rsqrt(jnp.sum(q * q, axis=-1, keepdims=True) + NORM_EPS) * (dk ** -0.5)
        k = k * lax.rsqrt(jnp.sum(k * k, axis=-1, keepdims=True) + NORM_EPS)
        kb = k * betab[:, hs]
        qk = jnp.sum(q.astype(BF16).astype(F32) * k.astype(BF16).astype(F32), axis=-1, keepdims=True)
        w_ref[:, hs] = kb * eg[:, hs]
        qd_ref[:, hs] = q * eg[:, hs]
        k_ref[:, hs] = k
        vb_ref[:, hs] = v * betab[:, hs]
        qk_ref[:, hs] = jnp.broadcast_to(qk, (q.shape[0], dk))


def _sample_dn_state_kernel(w_ref, qd_ref, k_ref, vb_ref, qk_ref, eg_ref, s_ref, o_ref, so_ref):
    dk = DN_HEAD_DIM
    nb, nh = s_ref.shape[0], s_ref.shape[1]
    eye = (lax.broadcasted_iota(jnp.int32, (dk, dk), 0) == lax.broadcasted_iota(jnp.int32, (dk, dk), 1))
    for b in range(nb):
        for h in range(nh):
            hs = slice(h * dk, (h + 1) * dk)
            s = s_ref[b, h]
            lhs = jnp.concatenate([jnp.broadcast_to(w_ref[b:b + 1, hs], (8, dk)),
                                   jnp.broadcast_to(qd_ref[b:b + 1, hs], (8, dk))], axis=0)
            r = _dot(lhs, s)
            u = vb_ref[b:b + 1, hs] - r[0:1]
            ub = u.astype(BF16).astype(F32)
            o_ref[b:b + 1, hs] = r[8:9] + qk_ref[b:b + 1, hs] * ub
            kdiag = jnp.where(eye, jnp.broadcast_to(k_ref[b:b + 1, hs], (dk, dk)), 0.0)
            so_ref[b, h] = s * eg_ref[b:b + 1, hs] + _dot(kdiag, jnp.broadcast_to(u, (dk, dk)))


def _sample_tail_kernel(x_ref, o_ref, z_ref, onw_ref, wout_ref, n2_ref,
                        wg_ref, wv_ref, cwg_ref, cwv_ref, cbg_ref, cbv_ref, wdn_ref, stg_ref, stv_ref, fn_ref,
                        yo_ref, sto_ref, xn_s, acc_s):
    c = pl.program_id(0)
    dk = DN_HEAD_DIM

    @pl.when(c == 0)
    def _():
        outs = []
        for h in range(o_ref.shape[1] // dk):
            outs.append(_rms(o_ref[:, h * dk:(h + 1) * dk], onw_ref[...]))
        o = jnp.concatenate(outs, axis=1) * _silu(z_ref[...])
        x = x_ref[...] + _dot(o, wout_ref[...])
        acc_s[...] = x
        xn_s[...] = _rms(x, n2_ref[...]).astype(BF16)

    acc_s[...] += _ffn_sample_chunk(xn_s, wg_ref, wv_ref, cwg_ref, cwv_ref, cbg_ref, cbv_ref, wdn_ref,
                                    stg_ref, stv_ref, sto_ref)

    @pl.when(c == pl.num_programs(0) - 1)
    def _():
        yo_ref[...] = _rms(acc_s[...], fn_ref[...])


def _params(*sem):
    return pltpu.CompilerParams(dimension_semantics=sem, vmem_limit_bytes=VMEM_LIMIT)


def kernel(x_prompt, x_sample, state_pool_buf, state_dn_conv, state_dn_ssm, state_ffn_conv, norm1_w, norm2_w,
           final_norm_w, pool_w, pool_scale, dn_w_in, dn_conv_w, dn_a_log, dn_dt_bias, dn_o_norm_w, dn_w_out,
           ffn_w_up, ffn_conv_w, ffn_conv_b, ffn_w_down):
    bp, seq, d = x_prompt.shape
    bs = x_sample.shape[0]
    nbuf = state_pool_buf.shape[2]
    dff = ffn_w_down.shape[1]
    nh = dn_a_log.shape[1]
    dk = DN_HEAD_DIM
    dn = nh * dk
    gd = pool_w.shape[2]
    tm = PROMPT_TILE
    nt = seq // tm
    nfc = dff // FFN_CHUNK
    lanes = V7X_LANES
    assert seq % tm == 0 and dff % FFN_CHUNK == 0 and (3 * dn) % QKV_CHUNK == 0 and dk == lanes
    assert nbuf + 1 == max(POOL_WINDOWS) and 2 * nh <= lanes and bs % SAMPLE_BLOCK == 0

    row = lambda v: v.reshape(1, -1)
    wup = ffn_w_up.astype(BF16)
    wdn = ffn_w_down.astype(BF16)
    pw = pool_w.astype(BF16)
    w_in = dn_w_in[0]
    win_all = dn_w_in.astype(BF16)
    wab = jnp.pad(w_in[:, 4 * dn:], ((0, 0), (0, lanes - 2 * nh))).astype(BF16)
    wout = dn_w_out[0].astype(BF16)
    alog = jnp.pad(dn_a_log[0], (0, lanes - nh)).reshape(1, lanes)
    dtb = jnp.pad(dn_dt_bias[0], (0, lanes - nh)).reshape(1, lanes)
    onw = row(dn_o_norm_w[0])
    lane_head = jnp.arange(dn) // dk
    selg = (jnp.arange(lanes)[:, None] == lane_head[None, :]).astype(BF16)
    selb = (jnp.arange(lanes)[:, None] == (lane_head[None, :] + nh)).astype(BF16)
    ti = jnp.arange(tm)
    ltri = ((ti[:, None] // DN_CHUNK == ti[None, :] // DN_CHUNK) & (ti[:, None] >= ti[None, :])).astype(BF16)

    tf = FFN_TILE
    assert seq % tf == 0
    x_spec = pl.BlockSpec((1, tf, d), lambda b, t: (b, t, 0))
    layer_resident = lambda shape, layer: pl.BlockSpec((None,) + shape, lambda *_: (layer,) + (0,) * len(shape),
                                                       pipeline_mode=pl.Buffered(1))
    ffn_w_specs = lambda layer: [layer_resident((d, 2 * dff), layer), _resident((3, 2 * dff)),
                                 _resident((1, 2 * dff)), layer_resident((dff, d), layer)]
    fst_spec = pl.BlockSpec((1, CONV_HALO, 2 * dff), lambda b, t: (b, 0, 0))
    fst_shape = jax.ShapeDtypeStruct((bp, CONV_HALO, 2 * dff), F32)

    x2, pool_tail, fst0 = pl.pallas_call(
        _prompt_layer0_kernel,
        grid=(bp, seq // tf),
        in_specs=[x_spec, _resident((1, d)), _resident(pw.shape[1:]), _resident((1, d)), _resident((1, d))]
        + ffn_w_specs(0),
        out_specs=[x_spec, pl.BlockSpec((1, POOL_HALO, d), lambda b, t: (b, 0, 0)), fst_spec],
        out_shape=[jax.ShapeDtypeStruct((bp, seq, d), F32), jax.ShapeDtypeStruct((bp, POOL_HALO, d), F32), fst_shape],
        scratch_shapes=[pltpu.VMEM((POOL_HALO + tf, d), F32), pltpu.VMEM((CONV_HALO, 2 * dff), F32)],
        compiler_params=_params("arbitrary", "arbitrary"),
        name="prompt_layer0",
    )(x_prompt, row(norm1_w[0]), pw[0], row(pool_scale[0]), row(norm2_w[0]),
      wup, ffn_conv_w[0], row(ffn_conv_b[0]), wdn)

    ntiles = bp * nt
    front_tile = lambda i: jnp.minimum(i, ntiles - 1)
    back_tile = lambda i: jnp.maximum(i - 1, 0)
    x_front = pl.BlockSpec((1, tm, d), lambda i: (front_tile(i) // nt, front_tile(i) % nt, 0))
    x_back = pl.BlockSpec((1, tm, d), lambda i: (back_tile(i) // nt, back_tile(i) % nt, 0))
    x3, cst, ssm = pl.pallas_call(
        functools.partial(_prompt_deltanet_kernel, tiles_per_seq=nt),
        grid=(ntiles + 1,),
        in_specs=[x_front, x_back,
                  _resident((1, d)), layer_resident((d, win_all.shape[2]), 0), _resident((d, lanes)),
                  _resident((4, 3 * dn)), _resident((1, lanes)), _resident((1, lanes)), _resident((1, dk)),
                  _resident((dn, d)), _resident((tm, tm))],
        out_specs=[x_back,
                   pl.BlockSpec((1, CONV_HALO, 3 * dn), lambda i: (front_tile(i) // nt, 0, 0)),
                   pl.BlockSpec((1, nh, dk, dk), lambda i: (back_tile(i) // nt, 0, 0, 0))],
        out_shape=[jax.ShapeDtypeStruct((bp, seq, d), F32),
                   jax.ShapeDtypeStruct((bp, CONV_HALO, 3 * dn), F32),
                   jax.ShapeDtypeStruct((bp, nh, dk, dk), F32)],
        scratch_shapes=[pltpu.VMEM((CONV_HALO, 3 * dn), F32), pltpu.VMEM((dk, dn), F32),
                        pltpu.VMEM((tm, 3 * dn), F32)]
        + [pltpu.VMEM((2, tm, dn), BF16)] * 5 + [pltpu.VMEM((2, tm, dn), F32)] * 4
        + [pltpu.VMEM((tm, dn), F32), pltpu.VMEM((tm, dn), BF16),
           pltpu.VMEM((tm // DN_CHUNK, nh // HEAD_GROUP, DN_CHUNK, HEAD_GROUP * DN_CHUNK), BF16),
           pltpu.VMEM((tm, dn), F32), pltpu.VMEM((tm, dn), BF16)],
        compiler_params=_params("arbitrary"),
        name="prompt_deltanet",
    )(x2, x2, row(norm1_w[1]), win_all, wab, dn_conv_w[0], alog, dtb, onw, wout, ltri)

    y_prompt, fst1 = pl.pallas_call(
        _prompt_ffn_final_kernel,
        grid=(bp, seq // tf),
        in_specs=[x_spec, _resident((1, d))] + ffn_w_specs(1) + [_resident((1, d))],
        out_specs=[x_spec, fst_spec],
        out_shape=[jax.ShapeDtypeStruct((bp, seq, d), F32), fst_shape],
        scratch_shapes=[pltpu.VMEM((CONV_HALO, 2 * dff), F32)],
        compiler_params=_params("arbitrary", "arbitrary"),
        name="prompt_ffn_final",
    )(x3, row(norm2_w[1]), wup, ffn_conv_w[1], row(ffn_conv_b[1]), wdn, row(final_norm_w))

    xs = x_sample[:, 0, :]
    pb = jnp.transpose(state_pool_buf[0], (1, 0, 2))
    cs = jnp.transpose(state_dn_conv[0], (1, 0, 2))
    sc = SAMPLE_FFN_CHUNK
    nsc = dff // sc
    assert dff % sc == 0

    def ffn_chunk_specs(layer):
        specs = []
        for shape in ((None, d, sc), (3, sc), (1, sc)):
            for half in range(2):
                if shape[0] is None:
                    specs.append(pl.BlockSpec(shape, lambda c, half=half: (layer, 0, half * nsc + c)))
                else:
                    specs.append(pl.BlockSpec(shape, lambda c, half=half: (0, half * nsc + c)))
        specs.append(pl.BlockSpec((None, sc, d), lambda c: (layer, c, 0)))
        for half in range(2):
            specs.append(pl.BlockSpec((None, bs, 2, sc), lambda c, half=half: (layer, 0, 0, half * nsc + c)))
        return specs

    sto_spec = _full((bs, 2, 2 * dff))
    sto_shape = jax.ShapeDtypeStruct((bs, 2, 2 * dff), F32)

    def ffn_chunk_args(layer):
        cb = row(ffn_conv_b[layer])
        return (wup, wup, ffn_conv_w[layer], ffn_conv_w[layer], cb, cb, wdn, state_ffn_conv, state_ffn_conv)

    xs2, pool_row, fso0 = pl.pallas_call(
        _sample_layer0_kernel,
        grid=(nsc,),
        in_specs=[_resident((bs, d)), _resident((nbuf, bs, d)), _resident((1, d)), _resident(pw.shape[1:]),
                  _resident((1, d)), _resident((1, d))] + ffn_chunk_specs(0),
        out_specs=[_full((bs, d)), _full((bs, d)), sto_spec],
        out_shape=[jax.ShapeDtypeStruct((bs, d), F32), jax.ShapeDtypeStruct((bs, d), F32), sto_shape],
        scratch_shapes=[pltpu.VMEM((bs, d), BF16)],
        compiler_params=_params("arbitrary"),
        name="sample_layer0",
    )(xs, pb, row(norm1_w[0]), pw[0], row(pool_scale[0]), row(norm2_w[0]), *ffn_chunk_args(0))

    vec = jax.ShapeDtypeStruct((bs, dn), F32)
    w_s, qd_s, k_s, vb_s, qk_s, eg_s, z_s, conv_row = pl.pallas_call(
        _sample_dn_proj_kernel,
        out_shape=[vec] * 7 + [jax.ShapeDtypeStruct((bs, 3 * dn), F32)],
        compiler_params=pltpu.CompilerParams(vmem_limit_bytes=VMEM_LIMIT),
        name="sample_dn_proj",
    )(xs2, row(norm1_w[1]), win_all[0], wab, dn_conv_w[0], alog, dtb, selg, selb, cs)

    nb = SAMPLE_BLOCK
    vspec = pl.BlockSpec((nb, dn), lambda i: (i, 0))
    sspec = pl.BlockSpec((nb, nh, dk, dk), lambda i: (i, 0, 0, 0))
    o_s, ssm_s = pl.pallas_call(
        _sample_dn_state_kernel,
        grid=(bs // nb,),
        in_specs=[vspec] * 6 + [sspec],
        out_specs=[vspec, sspec],
        out_shape=[vec, jax.ShapeDtypeStruct((bs, nh, dk, dk), F32)],
        compiler_params=_params("arbitrary"),
        name="sample_dn_state",
    )(w_s, qd_s, k_s, vb_s, qk_s, eg_s, state_dn_ssm[0])

    ys, fso1 = pl.pallas_call(
        _sample_tail_kernel,
        grid=(nsc,),
        in_specs=[_resident((bs, d)), _resident((bs, dn)), _resident((bs, dn)), _resident((1, dk)),
                  _resident((dn, d)), _resident((1, d))] + ffn_chunk_specs(1) + [_resident((1, d))],
        out_specs=[_full((bs, d)), sto_spec],
        out_shape=[jax.ShapeDtypeStruct((bs, d), F32), sto_shape],
        scratch_shapes=[pltpu.VMEM((bs, d), BF16), pltpu.VMEM((bs, d), F32)],
        compiler_params=_params("arbitrary"),
        name="sample_tail",
    )(xs2, o_s, z_s, onw, wout, row(norm2_w[1]), *ffn_chunk_args(1), row(final_norm_w))

    shifted = lambda state, new_row: jnp.concatenate([state[:, 1:], new_row[:, None]], axis=1)
    return (
        y_prompt,
        ys[:, None, :],
        pool_tail[None, :, POOL_HALO - nbuf:, :],
        shifted(state_pool_buf[0], pool_row)[None],
        cst[None, :, CONV_HALO - 3:, :],
        shifted(state_dn_conv[0], conv_row)[None],
        ssm[None],
        ssm_s[None],
        jnp.stack([fst0[:, CONV_HALO - 2:, :], fst1[:, CONV_HALO - 2:, :]]),
        jnp.stack([fso0, fso1]),
    )
```

```python
import functools

import jax
import jax.numpy as jnp
from jax import lax
from jax.experimental import pallas as pl
from jax.experimental.pallas import tpu as pltpu

F32 = jnp.float32
BF16 = jnp.bfloat16

NORM_EPS = 1e-6
POOL_WINDOWS = (2, 4, 8, 16)
POOL_HALO = 16
CONV_HALO = 8
DN_HEAD_DIM = 128
DN_CHUNK = 64
V7X_LANES = 128
V7X_MXU_WIDTH = 256
V7X_VMEM_BYTES = 64 * 1024 * 1024
FFN_CHUNK = V7X_MXU_WIDTH
FFN_DOWN_GROUP = 4
FFN_LOOKAHEAD = 3
QKV_CHUNK = V7X_MXU_WIDTH
OUT_CHUNK = V7X_MXU_WIDTH
PROMPT_TILE = 256
FFN_TILE = 512
HEAD_GROUP = 4
SAMPLE_FFN_CHUNK = 1408
SAMPLE_BLOCK = 16
VMEM_LIMIT = V7X_VMEM_BYTES * 7 // 8


def _dot(a, b):
    return jnp.dot(a.astype(BF16), b.astype(BF16), preferred_element_type=F32)


def _split3(x):
    hi = x.astype(BF16)
    r = x - hi.astype(F32)
    mid = r.astype(BF16)
    lo = (r - mid.astype(F32)).astype(BF16)
    return hi, mid, lo


def _dot_exact_rhs(x, e):
    hi, mid, lo = _split3(x)
    return (jnp.dot(lo, e, preferred_element_type=F32) + jnp.dot(mid, e, preferred_element_type=F32)
            + jnp.dot(hi, e, preferred_element_type=F32))


def _dot_exact_lhs(e, x):
    hi, mid, lo = _split3(x)
    return (jnp.dot(e, lo, preferred_element_type=F32) + jnp.dot(e, mid, preferred_element_type=F32)
            + jnp.dot(e, hi, preferred_element_type=F32))


def _rms(x, w):
    return x * lax.rsqrt(jnp.mean(x * x, axis=-1, keepdims=True) + NORM_EPS) * w


def _silu(x):
    h = 0.5 * x
    return h + h * jnp.tanh(h)


def _softplus(x):
    return jnp.maximum(x, 0.0) + jnp.log1p(jnp.exp(-jnp.abs(x)))


def _full(shape):
    return pl.BlockSpec(shape, lambda *_: (0,) * len(shape))


def _resident(shape):
    return pl.BlockSpec(shape, lambda *_: (0,) * len(shape), pipeline_mode=pl.Buffered(1))


def _ffn_tile(xn_b, wup_ref, cw_ref, cb_ref, wdn_ref, carry_ref):
    tm = xn_b.shape[0]
    dff = wdn_ref.shape[0]
    nchunk = dff // FFN_CHUNK

    def up(c):
        return [jnp.dot(xn_b, wup_ref[:, half * dff + c * FFN_CHUNK:half * dff + (c + 1) * FFN_CHUNK],
                        preferred_element_type=F32) for half in range(2)]

    def conv_act(c, hs):
        conv = []
        for half, h in enumerate(hs):
            cols = slice(half * dff + c * FFN_CHUNK, half * dff + (c + 1) * FFN_CHUNK)
            ext = jnp.concatenate([carry_ref[:, cols], h], axis=0)
            y = (ext * cw_ref[2:3, cols] + pltpu.roll(ext, 1, 0) * cw_ref[1:2, cols]
                 + pltpu.roll(ext, 2, 0) * cw_ref[0:1, cols])
            conv.append(y[CONV_HALO:] + cb_ref[:, cols])
            carry_ref[:, cols] = h[tm - CONV_HALO:]
        return (_silu(conv[0]) * conv[1]).astype(BF16)

    acc = None
    acts = []
    ahead = [up(c) for c in range(min(FFN_LOOKAHEAD, nchunk))]
    for c in range(nchunk):
        cur = ahead.pop(0)
        if c + FFN_LOOKAHEAD < nchunk:
            ahead.append(up(c + FFN_LOOKAHEAD))
        acts.append(conv_act(c, cur))
        if len(acts) == FFN_DOWN_GROUP or c + 1 == nchunk:
            c0 = c + 1 - len(acts)
            a = acts[0] if len(acts) == 1 else jnp.concatenate(acts, axis=1)
            d = jnp.dot(a, wdn_ref[c0 * FFN_CHUNK:(c + 1) * FFN_CHUNK, :], preferred_element_type=F32)
            acc = d if acc is None else acc + d
            acts = []
    return acc


def _prompt_layer0_kernel(x_ref, n1_ref, pw_ref, ps_ref, n2_ref, wup_ref, cw_ref, cb_ref, wdn_ref,
                          xo_ref, pool_ref, fst_ref, hbuf, carry):
    t = pl.program_id(1)
    tm = x_ref.shape[1]
    gd = pw_ref.shape[1]

    @pl.when(t == 0)
    def _():
        hbuf[0:POOL_HALO, :] = jnp.zeros((POOL_HALO, hbuf.shape[1]), F32)
        carry[...] = jnp.zeros(carry.shape, F32)

    x = x_ref[0]
    h = _rms(x, n1_ref[...])
    hbuf[POOL_HALO:POOL_HALO + tm, :] = h
    pos1 = t * tm + lax.broadcasted_iota(jnp.int32, (tm, 1), 0) + 1
    parts = []
    for g, w in enumerate(POOL_WINDOWS):
        cols = slice(g * gd, (g + 1) * gd)
        s = hbuf[:, cols]
        sh = 1
        while sh < w:
            s = s + pltpu.roll(s, sh, 0)
            sh *= 2
        cnt = jnp.minimum(w, pos1).astype(F32)
        dg = s[POOL_HALO:] / cnt - h[:, cols]
        parts.append(x[:, cols] + _dot(dg, pw_ref[g]) * ps_ref[:, cols])
    x1 = jnp.concatenate(parts, axis=1)
    tail = hbuf[tm:tm + POOL_HALO, :]
    pool_ref[0] = tail
    hbuf[0:POOL_HALO, :] = tail

    xn = _rms(x1, n2_ref[...]).astype(BF16)
    xo_ref[0] = x1 + _ffn_tile(xn, wup_ref, cw_ref, cb_ref, wdn_ref, carry)
    fst_ref[0] = carry[...]


def _prompt_ffn_final_kernel(x_ref, n2_ref, wup_ref, cw_ref, cb_ref, wdn_ref, fn_ref,
                             yo_ref, fst_ref, carry):
    @pl.when(pl.program_id(1) == 0)
    def _():
        carry[...] = jnp.zeros(carry.shape, F32)

    x = x_ref[0]
    xn = _rms(x, n2_ref[...]).astype(BF16)
    x = x + _ffn_tile(xn, wup_ref, cw_ref, cb_ref, wdn_ref, carry)
    yo_ref[0] = _rms(x, fn_ref[...])
    fst_ref[0] = carry[...]


def _interleave(streams):
    merged = []
    for si, steps in enumerate(streams):
        total = float(sum(w for w, _ in steps))
        done = 0.0
        for w, thunk in steps:
            merged.append(((done + 0.5 * w) / total, si, len(merged), thunk))
            done += w
    for _, _, _, thunk in sorted(merged, key=lambda m: m[:3]):
        thunk()


def _prompt_deltanet_kernel(xa_ref, xb_ref, n1_ref, win_ref, wab_ref, cw_ref, alog_ref, dtb_ref, onw_ref,
                            wout_ref, ltri_ref,
                            xo_ref, cst_ref, ssm_ref,
                            carry, s_ref, qkv_s, kb_s, q_s, k_s, qd_s, kd_s, vb_s, wr_s, gcol_s, z_s, uin_s, win_s,
                            qk_s, o_s, og_s, *, tiles_per_seq):
    i = pl.program_id(0)
    tm = xa_ref.shape[1]
    dk = DN_HEAD_DIM
    dn = s_ref.shape[1]
    nh = dn // dk
    ch = DN_CHUNK
    nch = tm // ch
    gh = HEAD_GROUP
    gw = gh * dk
    pw = gh * ch
    d_model = xa_ref.shape[2]
    wr_slot = i % 2
    rd_slot = 1 - wr_slot

    @pl.when(i == 0)
    def _():
        for ref in (kb_s, q_s, k_s, qd_s, kd_s, vb_s, wr_s, gcol_s, z_s):
            ref[1] = jnp.zeros(ref.shape[1:], ref.dtype)
        s_ref[...] = jnp.zeros(s_ref.shape, F32)

    @pl.when(i % tiles_per_seq == 0)
    def _():
        carry[...] = jnp.zeros(carry.shape, F32)

    @pl.when((i + tiles_per_seq - 1) % tiles_per_seq == 0)
    def _():
        s_ref[...] = jnp.zeros(s_ref.shape, F32)

    front, back = [], []
    env = {}

    def f_norm():
        env["xb"] = _rms(xa_ref[0], n1_ref[...]).astype(BF16)

    def f_qkv(c):
        cols = slice(c * QKV_CHUNK, (c + 1) * QKV_CHUNK)
        pre = jnp.dot(env["xb"], win_ref[:, cols], preferred_element_type=F32)
        ext = jnp.concatenate([carry[:, cols], pre], axis=0)
        prev = pltpu.roll(ext, 1, 0)
        y = (ext * cw_ref[3:4, cols] + prev * cw_ref[2:3, cols]
             + pltpu.roll(ext * cw_ref[1:2, cols] + prev * cw_ref[0:1, cols], 2, 0))
        qkv_s[:, cols] = _silu(y[CONV_HALO:])
        carry[:, cols] = pre[tm - CONV_HALO:]
        cst_ref[0, :, cols] = pre[tm - CONV_HALO:]

    def f_gates():
        ab = jnp.dot(env["xb"], wab_ref[...], preferred_element_type=F32)
        g = -jnp.exp(alog_ref[...]) * _softplus(ab + dtb_ref[...])
        gc = _dot_exact_lhs(ltri_ref[...], g)
        glast = jnp.concatenate(
            [jnp.broadcast_to(gc[(c + 1) * ch - 1:(c + 1) * ch, :], (ch, gc.shape[1])) for c in range(nch)], axis=0)
        env["gc"] = gc
        env["egc"] = jnp.exp(gc)
        env["kdec"] = jnp.exp(glast - gc)
        env["sig"] = 1.0 / (1.0 + jnp.exp(-ab))

    def f_z(j):
        cols = slice(j * OUT_CHUNK, (j + 1) * OUT_CHUNK)
        z = jnp.dot(env["xb"], win_ref[:, 3 * dn + cols.start:3 * dn + cols.stop], preferred_element_type=F32)
        z_s[wr_slot, :, cols] = _silu(z)

    def f_head(h):
        hs = slice(h * dk, (h + 1) * dk)
        gcol = jnp.broadcast_to(env["gc"][:, h:h + 1], (tm, dk))
        beta = jnp.broadcast_to(env["sig"][:, nh + h:nh + h + 1], (tm, dk))
        egc = jnp.broadcast_to(env["egc"][:, h:h + 1], (tm, dk))
        kdec = jnp.broadcast_to(env["kdec"][:, h:h + 1], (tm, dk))
        qh = qkv_s[:, hs]
        kh = qkv_s[:, dn + h * dk:dn + (h + 1) * dk]
        vh = qkv_s[:, 2 * dn + h * dk:2 * dn + (h + 1) * dk]
        qh = qh * (lax.rsqrt(jnp.sum(qh * qh, axis=-1, keepdims=True) + NORM_EPS) * (dk ** -0.5))
        kh = kh * lax.rsqrt(jnp.sum(kh * kh, axis=-1, keepdims=True) + NORM_EPS)
        kb = kh * beta
        gcol_s[wr_slot, :, hs] = gcol
        q_s[wr_slot, :, hs] = qh.astype(BF16)
        k_s[wr_slot, :, hs] = kh.astype(BF16)
        kb_s[wr_slot, :, hs] = kb.astype(BF16)
        qd_s[wr_slot, :, hs] = (qh * egc).astype(BF16)
        kd_s[wr_slot, :, hs] = (kh * kdec).astype(BF16)
        vb_s[wr_slot, :, hs] = vh * beta
        wr_s[wr_slot, :, hs] = kb * egc

    front.append((2, f_norm))
    per_part = dn // QKV_CHUNK
    for c in [part * per_part + j for j in range(per_part) for part in range(3)]:
        front.append((20.0 * QKV_CHUNK / 1024, functools.partial(f_qkv, c)))
    front.append((3, f_gates))
    for h in range(nh):
        front.append((3, functools.partial(f_head, h)))
    for j in range(dn // OUT_CHUNK):
        front.append((5.0 * OUT_CHUNK / dn, functools.partial(f_z, j)))

    def iota2(shape, d):
        return lax.broadcasted_iota(jnp.int32, shape, d)

    pi, pj = iota2((ch, pw), 0), iota2((ch, pw), 1) % ch
    eye_p = pi == pj
    incl_p = pi >= pj
    strict_p = pi > pj
    eye_f = eye_p.astype(F32)
    head_of_lane = iota2((ch, pw), 1) // ch
    bd_p = (iota2((pw, pw), 0) // ch) == (iota2((pw, pw), 1) // ch)
    bd_k = (iota2((2 * ch, 2 * dk), 0) // ch) == (iota2((2 * ch, 2 * dk), 1) // dk)
    bd_u = (iota2((pw, gw), 0) // ch) == (iota2((pw, gw), 1) // dk)
    bd_s = (iota2((2 * dk, 2 * dk), 0) // dk) == (iota2((2 * dk, 2 * dk), 1) // dk)

    def expand(mc):
        return jnp.where(bd_p, jnp.concatenate([mc] * gh, axis=0), 0.0).astype(BF16)

    units = [(c, gi) for c in range(nch) for gi in range(nh // gh)]
    n_cs, m_cs, p_cs = [], [None] * len(units), [None] * len(units)

    def b_scores(c, gi):
        rows = slice(c * ch, (c + 1) * ch)
        h0 = gi * gh
        sc = []
        for p in range(gh // 2):
            ps = slice((h0 + 2 * p) * dk, (h0 + 2 * p + 2) * dk)
            kc = k_s[rd_slot, rows, ps]
            kbd = jnp.where(bd_k, jnp.concatenate([kc, kc], axis=0), jnp.zeros((), BF16))
            sc.append(lax.dot_general(jnp.concatenate([kb_s[rd_slot, rows, ps], q_s[rd_slot, rows, ps]], axis=0),
                                      kbd, (((1,), (1,)), ((), ())), preferred_element_type=F32))
        sc = jnp.concatenate(sc, axis=1)
        gcol_p = jnp.broadcast_to(gcol_s[rd_slot, rows, h0 * dk:h0 * dk + 1], (ch, pw))
        for hl in range(1, gh):
            gcol_p = jnp.where(
                head_of_lane == hl,
                jnp.broadcast_to(gcol_s[rd_slot, rows, (h0 + hl) * dk:(h0 + hl) * dk + 1], (ch, pw)), gcol_p)
        grow_p = jnp.sum(jnp.where(eye_p, gcol_p, 0.0), axis=0, keepdims=True)
        decay = jnp.exp(jnp.where(incl_p, gcol_p - grow_p, -jnp.inf))
        n_cs.append(jnp.where(strict_p, sc[0:ch] * decay, 0.0))
        qk_s[c, gi] = (sc[ch:2 * ch] * decay).astype(BF16)

    def b_square(u):
        m_cs[u] = jnp.dot(n_cs[u].astype(BF16), expand(n_cs[u]), preferred_element_type=F32)
        p_cs[u] = eye_f - n_cs[u]

    def b_level(u, last):
        m_bd = expand(m_cs[u])
        if last:
            p_cs[u] = p_cs[u] + jnp.dot(p_cs[u].astype(BF16), m_bd, preferred_element_type=F32)
        else:
            both = jnp.dot(jnp.concatenate([m_cs[u], p_cs[u]], axis=0).astype(BF16), m_bd,
                           preferred_element_type=F32)
            m_cs[u] = both[0:ch]
            p_cs[u] = p_cs[u] + both[ch:2 * ch]

    def b_apply(u):
        c, gi = units[u]
        rows = slice(c * ch, (c + 1) * ch)
        h0 = gi * gh
        rhs = jnp.concatenate(
            [jnp.concatenate([vb_s[rd_slot, rows, (h0 + hl) * dk:(h0 + hl + 1) * dk],
                              wr_s[rd_slot, rows, (h0 + hl) * dk:(h0 + hl + 1) * dk]], axis=1)
             for hl in range(gh)], axis=0)
        uw = rhs + jnp.dot(expand(p_cs[u] - eye_f), rhs.astype(BF16), preferred_element_type=F32)
        for hl in range(gh):
            hs = slice((h0 + hl) * dk, (h0 + hl + 1) * dk)
            uin_s[rows, hs] = uw[hl * ch:(hl + 1) * ch, 0:dk]
            win_s[rows, hs] = uw[hl * ch:(hl + 1) * ch, dk:2 * dk].astype(BF16)

    def b_recur(c, gi):
        rows = slice(c * ch, (c + 1) * ch)
        last = (c + 1) * ch - 1
        gs = slice(gi * gw, (gi + 1) * gw)
        u_parts, qs_parts = [], []
        for p in range(gh // 2):
            ps = slice(gi * gw + 2 * p * dk, gi * gw + (2 * p + 2) * dk)
            s_pair = s_ref[:, ps]
            s_bd = jnp.where(bd_s, jnp.concatenate([s_pair, s_pair], axis=0), 0.0).astype(BF16)
            r = jnp.dot(jnp.concatenate([win_s[rows, ps], qd_s[rd_slot, rows, ps]], axis=0), s_bd,
                        preferred_element_type=F32)
            u_parts.append(uin_s[rows, ps] - r[0:ch])
            qs_parts.append(r[ch:2 * ch])
        u_g = jnp.concatenate(u_parts, axis=1)
        u_bd = jnp.where(bd_u, jnp.concatenate([u_g] * gh, axis=0), 0.0).astype(BF16)
        o_s[rows, gs] = jnp.concatenate(qs_parts, axis=1) + jnp.dot(qk_s[c, gi], u_bd,
                                                                      preferred_element_type=F32)
        kd_stack = jnp.concatenate(
            [kd_s[rd_slot, rows, gi * gw + hl * dk:gi * gw + (hl + 1) * dk] for hl in range(gh)], axis=0)
        s_ref[:, gs] = (s_ref[:, gs] * jnp.exp(gcol_s[rd_slot, last:last + 1, gs])
                        + lax.dot_general(kd_stack, u_bd, (((0,), (0,)), ((), ())),
                                          preferred_element_type=F32))

    def b_gate(h):
        hs = slice(h * dk, (h + 1) * dk)
        og_s[:, hs] = (_rms(o_s[:, hs], onw_ref[...]) * z_s[rd_slot, :, hs]).astype(BF16)

    def b_proj(j):
        cols = slice(j * OUT_CHUNK, (j + 1) * OUT_CHUNK)
        xo_ref[0, :, cols] = xb_ref[0, :, cols] + jnp.dot(og_s[...], wout_ref[:, cols],
                                                         preferred_element_type=F32)

    def b_state():
        for h in range(nh):
            ssm_ref[0, h] = s_ref[:, h * dk:(h + 1) * dk]

    for c, gi in units:
        back.append((2, functools.partial(b_scores, c, gi)))
    for u in range(len(units)):
        back.append((1, functools.partial(b_square, u)))
    sq = 2
    while sq < ch:
        for u in range(len(units)):
            back.append((1, functools.partial(b_level, u, 2 * sq >= ch)))
        sq *= 2
    for u in range(len(units)):
        back.append((1.5, functools.partial(b_apply, u)))
    for c in range(nch):
        for gi in range(nh // gh):
            back.append((3, functools.partial(b_recur, c, gi)))
    for h in range(nh):
        back.append((0.5, functools.partial(b_gate, h)))
    for j in range(d_model // OUT_CHUNK):
        back.append((1, functools.partial(b_proj, j)))
    back.append((0.5, b_state))

    _interleave([front, back])


def _ffn_sample_chunk(xn_ref, wg_ref, wv_ref, cwg_ref, cwv_ref, cbg_ref, cbv_ref, wdn_ref, stg_ref, stv_ref,
                      sto_ref):
    sc = wg_ref.shape[1]
    dff = sto_ref.shape[2] // 2
    conv = []
    for half, (w_ref, cw_ref, cb_ref, st_ref) in enumerate(((wg_ref, cwg_ref, cbg_ref, stg_ref),
                                                            (wv_ref, cwv_ref, cbv_ref, stv_ref))):
        h = jnp.dot(xn_ref[...], w_ref[...], preferred_element_type=F32)
        prev = st_ref[:, 1, :]
        conv.append(st_ref[:, 0, :] * cw_ref[0:1, :] + prev * cw_ref[1:2, :] + h * cw_ref[2:3, :] + cb_ref[...])
        for cc in range(dff // sc):
            @pl.when(pl.program_id(0) == cc)
            def _(half=half, cc=cc, prev=prev, h=h):
                cols = slice(half * dff + cc * sc, half * dff + (cc + 1) * sc)
                sto_ref[:, 0, cols] = prev
                sto_ref[:, 1, cols] = h
    a = (_silu(conv[0]) * conv[1]).astype(BF16)
    return jnp.dot(a, wdn_ref[...], preferred_element_type=F32)


def _sample_layer0_kernel(x_ref, pb_ref, n1_ref, pw_ref, ps_ref, n2_ref,
                          wg_ref, wv_ref, cwg_ref, cwv_ref, cbg_ref, cbv_ref, wdn_ref, stg_ref, stv_ref,
                          xo_ref, ho_ref, sto_ref, xn_s):
    c = pl.program_id(0)
    nbuf = pb_ref.shape[0]
    gd = pw_ref.shape[1]

    @pl.when(c == 0)
    def _():
        x = x_ref[...]
        h = _rms(x, n1_ref[...])
        parts = []
        for g, w in enumerate(POOL_WINDOWS):
            cols = slice(g * gd, (g + 1) * gd)
            s = h[:, cols]
            for j in range(1, w):
                s = s + pb_ref[nbuf - j, :, cols]
            dg = s / float(w) - h[:, cols]
            parts.append(x[:, cols] + _dot(dg, pw_ref[g]) * ps_ref[:, cols])
        x1 = jnp.concatenate(parts, axis=1)
        ho_ref[...] = h
        xn_s[...] = _rms(x1, n2_ref[...]).astype(BF16)
        xo_ref[...] = x1

    xo_ref[...] += _ffn_sample_chunk(xn_s, wg_ref, wv_ref, cwg_ref, cwv_ref, cbg_ref, cbv_ref, wdn_ref,
                                     stg_ref, stv_ref, sto_ref)


def _sample_dn_proj_kernel(x_ref, n1_ref, win_ref, wab_ref, cw_ref, alog_ref, dtb_ref,
                           selg_ref, selb_ref, cst_ref,
                           w_ref, qd_ref, k_ref, vb_ref, qk_ref, eg_ref, z_ref, pre_ref):
    dk = DN_HEAD_DIM
    dn = cw_ref.shape[1] // 3
    nh = dn // dk
    xb = _rms(x_ref[...], n1_ref[...]).astype(BF16)
    pre = jnp.dot(xb, win_ref[:, 0:3 * dn], preferred_element_type=F32)
    y = (cst_ref[0] * cw_ref[0:1, :] + cst_ref[1] * cw_ref[1:2, :] + cst_ref[2] * cw_ref[2:3, :]
         + pre * cw_ref[3:4, :])
    qkv = _silu(y)
    pre_ref[...] = pre
    z_ref[...] = jnp.dot(xb, win_ref[:, 3 * dn:4 * dn], preferred_element_type=F32)
    ab = jnp.dot(xb, wab_ref[...], preferred_element_type=F32)
    g = -jnp.exp(alog_ref[...]) * _softplus(ab + dtb_ref[...])
    eg = jnp.exp(_dot_exact_rhs(g, selg_ref[...]))
    betab = _dot_exact_rhs(1.0 / (1.0 + jnp.exp(-ab)), selb_ref[...])
    eg_ref[...] = eg
    for h in range(nh):
        hs = slice(h * dk, (h + 1) * dk)
        q = qkv[:, hs]
        k = qkv[:, dn + h * dk:dn + (h + 1) * dk]
        v = qkv[:, 2 * dn + h * dk:2 * dn + (h + 1) * dk]
        q = q * lax.rsqrt(jnp.sum(q * q, axis=-1, keepdims=True) + NORM_EPS) * (dk ** -0.5)
        k = k * lax.rsqrt(jnp.sum(k * k, axis=-1, keepdims=True) + NORM_EPS)
        kb = k * betab[:, hs]
        qk = jnp.sum(q.astype(BF16).astype(F32) * k.astype(BF16).astype(F32), axis=-1, keepdims=True)
        w_ref[:, hs] = kb * eg[:, hs]
        qd_ref[:, hs] = q * eg[:, hs]
        k_ref[:, hs] = k
        vb_ref[:, hs] = v * betab[:, hs]
        qk_ref[:, hs] = jnp.broadcast_to(qk, (q.shape[0], dk))


def _sample_dn_state_kernel(w_ref, qd_ref, k_ref, vb_ref, qk_ref, eg_ref, s_ref, o_ref, so_ref):
    dk = DN_HEAD_DIM
    nb, nh = s_ref.shape[0], s_ref.shape[1]
    eye = (lax.broadcasted_iota(jnp.int32, (dk, dk), 0) == lax.broadcasted_iota(jnp.int32, (dk, dk), 1))
    for b in range(nb):
        for h in range(nh):
            hs = slice(h * dk, (h + 1) * dk)
            s = s_ref[b, h]
            lhs = jnp.concatenate([jnp.broadcast_to(w_ref[b:b + 1, hs], (8, dk)),
                                   jnp.broadcast_to(qd_ref[b:b + 1, hs], (8, dk))], axis=0)
            r = _dot(lhs, s)
            u = vb_ref[b:b + 1, hs] - r[0:1]
            ub = u.astype(BF16).astype(F32)
            o_ref[b:b + 1, hs] = r[8:9] + qk_ref[b:b + 1, hs] * ub
            kdiag = jnp.where(eye, jnp.broadcast_to(k_ref[b:b + 1, hs], (dk, dk)), 0.0)
            so_ref[b, h] = s * eg_ref[b:b + 1, hs] + _dot(kdiag, jnp.broadcast_to(u, (dk, dk)))


def _sample_tail_kernel(x_ref, o_ref, z_ref, onw_ref, wout_ref, n2_ref,
                        wg_ref, wv_ref, cwg_ref, cwv_ref, cbg_ref, cbv_ref, wdn_ref, stg_ref, stv_ref, fn_ref,
                        yo_ref, sto_ref, xn_s, acc_s):
    c = pl.program_id(0)
    dk = DN_HEAD_DIM

    @pl.when(c == 0)
    def _():
        outs = []
        for h in range(o_ref.shape[1] // dk):
            outs.append(_rms(o_ref[:, h * dk:(h + 1) * dk], onw_ref[...]))
        o = jnp.concatenate(outs, axis=1) * _silu(z_ref[...])
        x = x_ref[...] + _dot(o, wout_ref[...])
        acc_s[...] = x
        xn_s[...] = _rms(x, n2_ref[...]).astype(BF16)

    acc_s[...] += _ffn_sample_chunk(xn_s, wg_ref, wv_ref, cwg_ref, cwv_ref, cbg_ref, cbv_ref, wdn_ref,
                                    stg_ref, stv_ref, sto_ref)

    @pl.when(c == pl.num_programs(0) - 1)
    def _():
        yo_ref[...] = _rms(acc_s[...], fn_ref[...])


def _params(*sem):
    return pltpu.CompilerParams(dimension_semantics=sem, vmem_limit_bytes=VMEM_LIMIT)


def kernel(x_prompt, x_sample, state_pool_buf, state_dn_conv, state_dn_ssm, state_ffn_conv, norm1_w, norm2_w,
           final_norm_w, pool_w, pool_scale, dn_w_in, dn_conv_w, dn_a_log, dn_dt_bias, dn_o_norm_w, dn_w_out,
           ffn_w_up, ffn_conv_w, ffn_conv_b, ffn_w_down):
    bp, seq, d = x_prompt.shape
    bs = x_sample.shape[0]
    nbuf = state_pool_buf.shape[2]
    dff = ffn_w_down.shape[1]
    nh = dn_a_log.shape[1]
    dk = DN_HEAD_DIM
    dn = nh * dk
    gd = pool_w.shape[2]
    tm = PROMPT_TILE
    nt = seq // tm
    nfc = dff // FFN_CHUNK
    lanes = V7X_LANES
    assert seq % tm == 0 and dff % FFN_CHUNK == 0 and (3 * dn) % QKV_CHUNK == 0 and dk == lanes
    assert nbuf + 1 == max(POOL_WINDOWS) and 2 * nh <= lanes and bs % SAMPLE_BLOCK == 0

    row = lambda v: v.reshape(1, -1)
    wup = ffn_w_up.astype(BF16)
    wdn = ffn_w_down.astype(BF16)
    pw = pool_w.astype(BF16)
    w_in = dn_w_in[0]
    win_all = dn_w_in.astype(BF16)
    wab = jnp.pad(w_in[:, 4 * dn:], ((0, 0), (0, lanes - 2 * nh))).astype(BF16)
    wout = dn_w_out[0].astype(BF16)
    alog = jnp.pad(dn_a_log[0], (0, lanes - nh)).reshape(1, lanes)
    dtb = jnp.pad(dn_dt_bias[0], (0, lanes - nh)).reshape(1, lanes)
    onw = row(dn_o_norm_w[0])
    lane_head = jnp.arange(dn) // dk
    selg = (jnp.arange(lanes)[:, None] == lane_head[None, :]).astype(BF16)
    selb = (jnp.arange(lanes)[:, None] == (lane_head[None, :] + nh)).astype(BF16)
    ti = jnp.arange(tm)
    ltri = ((ti[:, None] // DN_CHUNK == ti[None, :] // DN_CHUNK) & (ti[:, None] >= ti[None, :])).astype(BF16)

    tf = FFN_TILE
    assert seq % tf == 0
    x_spec = pl.BlockSpec((1, tf, d), lambda b, t: (b, t, 0))
    layer_resident = lambda shape, layer: pl.BlockSpec((None,) + shape, lambda *_: (layer,) + (0,) * len(shape),
                                                       pipeline_mode=pl.Buffered(1))
    ffn_w_specs = lambda layer: [layer_resident((d, 2 * dff), layer), _resident((3, 2 * dff)),
                                 _resident((1, 2 * dff)), layer_resident((dff, d), layer)]
    fst_spec = pl.BlockSpec((1, CONV_HALO, 2 * dff), lambda b, t: (b, 0, 0))
    fst_shape = jax.ShapeDtypeStruct((bp, CONV_HALO, 2 * dff), F32)

    x2, pool_tail, fst0 = pl.pallas_call(
        _prompt_layer0_kernel,
        grid=(bp, seq // tf),
        in_specs=[x_spec, _resident((1, d)), _resident(pw.shape[1:]), _resident((1, d)), _resident((1, d))]
        + ffn_w_specs(0),
        out_specs=[x_spec, pl.BlockSpec((1, POOL_HALO, d), lambda b, t: (b, 0, 0)), fst_spec],
        out_shape=[jax.ShapeDtypeStruct((bp, seq, d), F32), jax.ShapeDtypeStruct((bp, POOL_HALO, d), F32), fst_shape],
        scratch_shapes=[pltpu.VMEM((POOL_HALO + tf, d), F32), pltpu.VMEM((CONV_HALO, 2 * dff), F32)],
        compiler_params=_params("arbitrary", "arbitrary"),
        name="prompt_layer0",
    )(x_prompt, row(norm1_w[0]), pw[0], row(pool_scale[0]), row(norm2_w[0]),
      wup, ffn_conv_w[0], row(ffn_conv_b[0]), wdn)

    ntiles = bp * nt
    front_tile = lambda i: jnp.minimum(i, ntiles - 1)
    back_tile = lambda i: jnp.maximum(i - 1, 0)
    x_front = pl.BlockSpec((1, tm, d), lambda i: (front_tile(i) // nt, front_tile(i) % nt, 0))
    x_back = pl.BlockSpec((1, tm, d), lambda i: (back_tile(i) // nt, back_tile(i) % nt, 0))
    x3, cst, ssm = pl.pallas_call(
        functools.partial(_prompt_deltanet_kernel, tiles_per_seq=nt),
        grid=(ntiles + 1,),
        in_specs=[x_front, x_back,
                  _resident((1, d)), layer_resident((d, win_all.shape[2]), 0), _resident((d, lanes)),
                  _resident((4, 3 * dn)), _resident((1, lanes)), _resident((1, lanes)), _resident((1, dk)),
                  _resident((dn, d)), _resident((tm, tm))],
        out_specs=[x_back,
                   pl.BlockSpec((1, CONV_HALO, 3 * dn), lambda i: (front_tile(i) // nt, 0, 0)),
                   pl.BlockSpec((1, nh, dk, dk), lambda i: (back_tile(i) // nt, 0, 0, 0))],
        out_shape=[jax.ShapeDtypeStruct((bp, seq, d), F32),
                   jax.ShapeDtypeStruct((bp, CONV_HALO, 3 * dn), F32),
                   jax.ShapeDtypeStruct((bp, nh, dk, dk), F32)],
        scratch_shapes=[pltpu.VMEM((CONV_HALO, 3 * dn), F32), pltpu.VMEM((dk, dn), F32),
                        pltpu.VMEM((tm, 3 * dn), F32)]
        + [pltpu.VMEM((2, tm, dn), BF16)] * 5 + [pltpu.VMEM((2, tm, dn), F32)] * 4
        + [pltpu.VMEM((tm, dn), F32), pltpu.VMEM((tm, dn), BF16),
           pltpu.VMEM((tm // DN_CHUNK, nh // HEAD_GROUP, DN_CHUNK, HEAD_GROUP * DN_CHUNK), BF16),
           pltpu.VMEM((tm, dn), F32), pltpu.VMEM((tm, dn), BF16)],
        compiler_params=_params("arbitrary"),
        name="prompt_deltanet",
    )(x2, x2, row(norm1_w[1]), win_all, wab, dn_conv_w[0], alog, dtb, onw, wout, ltri)

    y_prompt, fst1 = pl.pallas_call(
        _prompt_ffn_final_kernel,
        grid=(bp, seq // tf),
        in_specs=[x_spec, _resident((1, d))] + ffn_w_specs(1) + [_resident((1, d))],
        out_specs=[x_spec, fst_spec],
        out_shape=[jax.ShapeDtypeStruct((bp, seq, d), F32), fst_shape],
        scratch_shapes=[pltpu.VMEM((CONV_HALO, 2 * dff), F32)],
        compiler_params=_params("arbitrary", "arbitrary"),
        name="prompt_ffn_final",
    )(x3, row(norm2_w[1]), wup, ffn_conv_w[1], row(ffn_conv_b[1]), wdn, row(final_norm_w))

    xs = x_sample[:, 0, :]
    pb = jnp.transpose(state_pool_buf[0], (1, 0, 2))
    cs = jnp.transpose(state_dn_conv[0], (1, 0, 2))
    sc = SAMPLE_FFN_CHUNK
    nsc = dff // sc
    assert dff % sc == 0

    def ffn_chunk_specs(layer):
        specs = []
        for shape in ((None, d, sc), (3, sc), (1, sc)):
            for half in range(2):
                if shape[0] is None:
                    specs.append(pl.BlockSpec(shape, lambda c, half=half: (layer, 0, half * nsc + c)))
                else:
                    specs.append(pl.BlockSpec(shape, lambda c, half=half: (0, half * nsc + c)))
        specs.append(pl.BlockSpec((None, sc, d), lambda c: (layer, c, 0)))
        for half in range(2):
            specs.append(pl.BlockSpec((None, bs, 2, sc), lambda c, half=half: (layer, 0, 0, half * nsc + c)))
        return specs

    sto_spec = _full((bs, 2, 2 * dff))
    sto_shape = jax.ShapeDtypeStruct((bs, 2, 2 * dff), F32)

    def ffn_chunk_args(layer):
        cb = row(ffn_conv_b[layer])
        return (wup, wup, ffn_conv_w[layer], ffn_conv_w[layer], cb, cb, wdn, state_ffn_conv, state_ffn_conv)

    xs2, pool_row, fso0 = pl.pallas_call(
        _sample_layer0_kernel,
        grid=(nsc,),
        in_specs=[_resident((bs, d)), _resident((nbuf, bs, d)), _resident((1, d)), _resident(pw.shape[1:]),
                  _resident((1, d)), _resident((1, d))] + ffn_chunk_specs(0),
        out_specs=[_full((bs, d)), _full((bs, d)), sto_spec],
        out_shape=[jax.ShapeDtypeStruct((bs, d), F32), jax.ShapeDtypeStruct((bs, d), F32), sto_shape],
        scratch_shapes=[pltpu.VMEM((bs, d), BF16)],
        compiler_params=_params("arbitrary"),
        name="sample_layer0",
    )(xs, pb, row(norm1_w[0]), pw[0], row(pool_scale[0]), row(norm2_w[0]), *ffn_chunk_args(0))

    vec = jax.ShapeDtypeStruct((bs, dn), F32)
    w_s, qd_s, k_s, vb_s, qk_s, eg_s, z_s, conv_row = pl.pallas_call(
        _sample_dn_proj_kernel,
        out_shape=[vec] * 7 + [jax.ShapeDtypeStruct((bs, 3 * dn), F32)],
        compiler_params=pltpu.CompilerParams(vmem_limit_bytes=VMEM_LIMIT),
        name="sample_dn_proj",
    )(xs2, row(norm1_w[1]), win_all[0], wab, dn_conv_w[0], alog, dtb, selg, selb, cs)

    nb = SAMPLE_BLOCK
    vspec = pl.BlockSpec((nb, dn), lambda i: (i, 0))
    sspec = pl.BlockSpec((nb, nh, dk, dk), lambda i: (i, 0, 0, 0))
    o_s, ssm_s = pl.pallas_call(
        _sample_dn_state_kernel,
        grid=(bs // nb,),
        in_specs=[vspec] * 6 + [sspec],
        out_specs=[vspec, sspec],
        out_shape=[vec, jax.ShapeDtypeStruct((bs, nh, dk, dk), F32)],
        compiler_params=_params("arbitrary"),
        name="sample_dn_state",
    )(w_s, qd_s, k_s, vb_s, qk_s, eg_s, state_dn_ssm[0])

    ys, fso1 = pl.pallas_call(
        _sample_tail_kernel,
        grid=(nsc,),
        in_specs=[_resident((bs, d)), _resident((bs, dn)), _resident((bs, dn)), _resident((1, dk)),
                  _resident((dn, d)), _resident((1, d))] + ffn_chunk_specs(1) + [_resident((1, d))],
        out_specs=[_full((bs, d)), sto_spec],
        out_shape=[jax.ShapeDtypeStruct((bs, d), F32), sto_shape],
        scratch_shapes=[pltpu.VMEM((bs, d), BF16), pltpu.VMEM((bs, d), F32)],
        compiler_params=_params("arbitrary"),
        name="sample_tail",
    )(xs2, o_s, z_s, onw, wout, row(norm2_w[1]), *ffn_chunk_args(1), row(final_norm_w))

    shifted = lambda state, new_row: jnp.concatenate([state[:, 1:], new_row[:, None]], axis=1)
    return (
        y_prompt,
        ys[:, None, :],
        pool_tail[None, :, POOL_HALO - nbuf:, :],
        shifted(state_pool_buf[0], pool_row)[None],
        cst[None, :, CONV_HALO - 3:, :],
        shifted(state_dn_conv[0], conv_row)[None],
        ssm[None],
        ssm_s[None],
        jnp.stack([fst0[:, CONV_HALO - 2:, :], fst1[:, CONV_HALO - 2:, :]]),
        jnp.stack([fso0, fso1]),
    )
```

```python
import functools

import jax
import jax.numpy as jnp
from jax import lax
from jax.experimental import pallas as pl
from jax.experimental.pallas import tpu as pltpu

F32 = jnp.float32
BF16 = jnp.bfloat16

NORM_EPS = 1e-6
POOL_WINDOWS = (2, 4, 8, 16)
POOL_HALO = 16
CONV_HALO = 8
DN_HEAD_DIM = 128
DN_CHUNK = 64
V7X_LANES = 128
V7X_MXU_WIDTH = 256
V7X_VMEM_BYTES = 64 * 1024 * 1024
FFN_CHUNK = V7X_MXU_WIDTH
FFN_DOWN_GROUP = 4
FFN_LOOKAHEAD = 3
QKV_CHUNK = V7X_MXU_WIDTH
OUT_CHUNK = V7X_MXU_WIDTH
PROMPT_TILE = 256
FFN_TILE = 512
HEAD_GROUP = 4
SAMPLE_FFN_CHUNK = 1408
SAMPLE_BLOCK = 16
VMEM_LIMIT = V7X_VMEM_BYTES * 7 // 8


def _dot(a, b):
    return jnp.dot(a.astype(BF16), b.astype(BF16), preferred_element_type=F32)


def _split3(x):
    hi = x.astype(BF16)
    r = x - hi.astype(F32)
    mid = r.astype(BF16)
    lo = (r - mid.astype(F32)).astype(BF16)
    return hi, mid, lo


def _dot_exact_rhs(x, e):
    hi, mid, lo = _split3(x)
    return (jnp.dot(lo, e, preferred_element_type=F32) + jnp.dot(mid, e, preferred_element_type=F32)
            + jnp.dot(hi, e, preferred_element_type=F32))


def _dot_exact_lhs(e, x):
    hi, mid, lo = _split3(x)
    return (jnp.dot(e, lo, preferred_element_type=F32) + jnp.dot(e, mid, preferred_element_type=F32)
            + jnp.dot(e, hi, preferred_element_type=F32))


def _rms(x, w):
    return x * lax.rsqrt(jnp.mean(x * x, axis=-1, keepdims=True) + NORM_EPS) * w


def _silu(x):
    h = 0.5 * x
    return h + h * jnp.tanh(h)


def _softplus(x):
    return jnp.maximum(x, 0.0) + jnp.log1p(jnp.exp(-jnp.abs(x)))


def _full(shape):
    return pl.BlockSpec(shape, lambda *_: (0,) * len(shape))


def _resident(shape):
    return pl.BlockSpec(shape, lambda *_: (0,) * len(shape), pipeline_mode=pl.Buffered(1))


def _ffn_tile(xn_b, wup_ref, cw_ref, cb_ref, wdn_ref, carry_ref):
    tm = xn_b.shape[0]
    dff = wdn_ref.shape[0]
    nchunk = dff // FFN_CHUNK

    def up(c):
        return [jnp.dot(xn_b, wup_ref[:, half * dff + c * FFN_CHUNK:half * dff + (c + 1) * FFN_CHUNK],
                        preferred_element_type=F32) for half in range(2)]

    def conv_act(c, hs):
        conv = []
        for half, h in enumerate(hs):
            cols = slice(half * dff + c * FFN_CHUNK, half * dff + (c + 1) * FFN_CHUNK)
            ext = jnp.concatenate([carry_ref[:, cols], h], axis=0)
            y = (ext * cw_ref[2:3, cols] + pltpu.roll(ext, 1, 0) * cw_ref[1:2, cols]
                 + pltpu.roll(ext, 2, 0) * cw_ref[0:1, cols])
            conv.append(y[CONV_HALO:] + cb_ref[:, cols])
            carry_ref[:, cols] = h[tm - CONV_HALO:]
        return (_silu(conv[0]) * conv[1]).astype(BF16)

    acc = None
    acts = []
    ahead = [up(c) for c in range(min(FFN_LOOKAHEAD, nchunk))]
    for c in range(nchunk):
        cur = ahead.pop(0)
        if c + FFN_LOOKAHEAD < nchunk:
            ahead.append(up(c + FFN_LOOKAHEAD))
        acts.append(conv_act(c, cur))
        if len(acts) == FFN_DOWN_GROUP or c + 1 == nchunk:
            c0 = c + 1 - len(acts)
            a = acts[0] if len(acts) == 1 else jnp.concatenate(acts, axis=1)
            d = jnp.dot(a, wdn_ref[c0 * FFN_CHUNK:(c + 1) * FFN_CHUNK, :], preferred_element_type=F32)
            acc = d if acc is None else acc + d
            acts = []
    return acc


def _prompt_layer0_kernel(x_ref, n1_ref, pw_ref, ps_ref, n2_ref, wup_ref, cw_ref, cb_ref, wdn_ref,
                          xo_ref, pool_ref, fst_ref, hbuf, carry):
    t = pl.program_id(1)
    tm = x_ref.shape[1]
    gd = pw_ref.shape[1]

    @pl.when(t == 0)
    def _():
        hbuf[0:POOL_HALO, :] = jnp.zeros((POOL_HALO, hbuf.shape[1]), F32)
        carry[...] = jnp.zeros(carry.shape, F32)

    x = x_ref[0]
    h = _rms(x, n1_ref[...])
    hbuf[POOL_HALO:POOL_HALO + tm, :] = h
    pos1 = t * tm + lax.broadcasted_iota(jnp.int32, (tm, 1), 0) + 1
    parts = []
    for g, w in enumerate(POOL_WINDOWS):
        cols = slice(g * gd, (g + 1) * gd)
        s = hbuf[:, cols]
        sh = 1
        while sh < w:
            s = s + pltpu.roll(s, sh, 0)
            sh *= 2
        cnt = jnp.minimum(w, pos1).astype(F32)
        dg = s[POOL_HALO:] / cnt - h[:, cols]
        parts.append(x[:, cols] + _dot(dg, pw_ref[g]) * ps_ref[:, cols])
    x1 = jnp.concatenate(parts, axis=1)
    tail = hbuf[tm:tm + POOL_HALO, :]
    pool_ref[0] = tail
    hbuf[0:POOL_HALO, :] = tail

    xn = _rms(x1, n2_ref[...]).astype(BF16)
    xo_ref[0] = x1 + _ffn_tile(xn, wup_ref, cw_ref, cb_ref, wdn_ref, carry)
    fst_ref[0] = carry[...]


def _prompt_ffn_final_kernel(x_ref, n2_ref, wup_ref, cw_ref, cb_ref, wdn_ref, fn_ref,
                             yo_ref, fst_ref, carry):
    @pl.when(pl.program_id(1) == 0)
    def _():
        carry[...] = jnp.zeros(carry.shape, F32)

    x = x_ref[0]
    xn = _rms(x, n2_ref[...]).astype(BF16)
    x = x + _ffn_tile(xn, wup_ref, cw_ref, cb_ref, wdn_ref, carry)
    yo_ref[0] = _rms(x, fn_ref[...])
    fst_ref[0] = carry[...]


def _interleave(streams):
    merged = []
    for si, steps in enumerate(streams):
        total = float(sum(w for w, _ in steps))
        done = 0.0
        for w, thunk in steps:
            merged.append(((done + 0.5 * w) / total, si, len(merged), thunk))
            done += w
    for _, _, _, thunk in sorted(merged, key=lambda m: m[:3]):
        thunk()


def _prompt_deltanet_kernel(xa_ref, xb_ref, n1_ref, win_ref, wab_ref, cw_ref, alog_ref, dtb_ref, onw_ref,
                            wout_ref, ltri_ref,
                            xo_ref, cst_ref, ssm_ref,
                            carry, s_ref, qkv_s, kb_s, q_s, k_s, qd_s, kd_s, vb_s, wr_s, gcol_s, z_s, uin_s, win_s,
                            qk_s, o_s, og_s, *, tiles_per_seq):
    i = pl.program_id(0)
    tm = xa_ref.shape[1]
    dk = DN_HEAD_DIM
    dn = s_ref.shape[1]
    nh = dn // dk
    ch = DN_CHUNK
    nch = tm // ch
    gh = HEAD_GROUP
    gw = gh * dk
    pw = gh * ch
    d_model = xa_ref.shape[2]
    wr_slot = i % 2
    rd_slot = 1 - wr_slot

    @pl.when(i == 0)
    def _():
        for ref in (kb_s, q_s, k_s, qd_s, kd_s, vb_s, wr_s, gcol_s, z_s):
            ref[1] = jnp.zeros(ref.shape[1:], ref.dtype)
        s_ref[...] = jnp.zeros(s_ref.shape, F32)

    @pl.when(i % tiles_per_seq == 0)
    def _():
        carry[...] = jnp.zeros(carry.shape, F32)

    @pl.when((i + tiles_per_seq - 1) % tiles_per_seq == 0)
    def _():
        s_ref[...] = jnp.zeros(s_ref.shape, F32)

    front, back = [], []
    env = {}

    def f_norm():
        env["xb"] = _rms(xa_ref[0], n1_ref[...]).astype(BF16)

    def f_qkv(c):
        cols = slice(c * QKV_CHUNK, (c + 1) * QKV_CHUNK)
        pre = jnp.dot(env["xb"], win_ref[:, cols], preferred_element_type=F32)
        ext = jnp.concatenate([carry[:, cols], pre], axis=0)
        prev = pltpu.roll(ext, 1, 0)
        y = (ext * cw_ref[3:4, cols] + prev * cw_ref[2:3, cols]
             + pltpu.roll(ext * cw_ref[1:2, cols] + prev * cw_ref[0:1, cols], 2, 0))
        qkv_s[:, cols] = _silu(y[CONV_HALO:])
        carry[:, cols] = pre[tm - CONV_HALO:]
        cst_ref[0, :, cols] = pre[tm - CONV_HALO:]

    def f_gates():
        ab = jnp.dot(env["xb"], wab_ref[...], preferred_element_type=F32)
        g = -jnp.exp(alog_ref[...]) * _softplus(ab + dtb_ref[...])
        env["gc"] = _dot_exact_lhs(ltri_ref[...], g)
        env["sig"] = 1.0 / (1.0 + jnp.exp(-ab))

    def f_z(j):
        cols = slice(j * OUT_CHUNK, (j + 1) * OUT_CHUNK)
        z = jnp.dot(env["xb"], win_ref[:, 3 * dn + cols.start:3 * dn + cols.stop], preferred_element_type=F32)
        z_s[wr_slot, :, cols] = _silu(z)

    def f_head(h):
        hs = slice(h * dk, (h + 1) * dk)
        gcol = jnp.broadcast_to(env["gc"][:, h:h + 1], (tm, dk))
        beta = jnp.broadcast_to(env["sig"][:, nh + h:nh + h + 1], (tm, dk))
        glast = jnp.concatenate(
            [jnp.broadcast_to(gcol[(c + 1) * ch - 1:(c + 1) * ch, :], (ch, dk)) for c in range(nch)], axis=0)
        egc = jnp.exp(gcol)
        qh = qkv_s[:, hs]
        kh = qkv_s[:, dn + h * dk:dn + (h + 1) * dk]
        vh = qkv_s[:, 2 * dn + h * dk:2 * dn + (h + 1) * dk]
        qh = qh * (lax.rsqrt(jnp.sum(qh * qh, axis=-1, keepdims=True) + NORM_EPS) * (dk ** -0.5))
        kh = kh * lax.rsqrt(jnp.sum(kh * kh, axis=-1, keepdims=True) + NORM_EPS)
        kb = kh * beta
        gcol_s[wr_slot, :, hs] = gcol
        q_s[wr_slot, :, hs] = qh.astype(BF16)
        k_s[wr_slot, :, hs] = kh.astype(BF16)
        kb_s[wr_slot, :, hs] = kb.astype(BF16)
        qd_s[wr_slot, :, hs] = (qh * egc).astype(BF16)
        kd_s[wr_slot, :, hs] = (kh * jnp.exp(glast - gcol)).astype(BF16)
        vb_s[wr_slot, :, hs] = vh * beta
        wr_s[wr_slot, :, hs] = kb * egc

    front.append((2, f_norm))
    per_part = dn // QKV_CHUNK
    for c in [part * per_part + j for j in range(per_part) for part in range(3)]:
        front.append((20.0 * QKV_CHUNK / 1024, functools.partial(f_qkv, c)))
    front.append((3, f_gates))
    for h in range(nh):
        front.append((3, functools.partial(f_head, h)))
    for j in range(dn // OUT_CHUNK):
        front.append((5.0 * OUT_CHUNK / dn, functools.partial(f_z, j)))

    def iota2(shape, d):
        return lax.broadcasted_iota(jnp.int32, shape, d)

    pi, pj = iota2((ch, pw), 0), iota2((ch, pw), 1) % ch
    eye_p = pi == pj
    incl_p = pi >= pj
    strict_p = pi > pj
    eye_f = eye_p.astype(F32)
    head_of_lane = iota2((ch, pw), 1) // ch
    bd_p = (iota2((pw, pw), 0) // ch) == (iota2((pw, pw), 1) // ch)
    bd_k = (iota2((2 * ch, 2 * dk), 0) // ch) == (iota2((2 * ch, 2 * dk), 1) // dk)
    bd_u = (iota2((pw, gw), 0) // ch) == (iota2((pw, gw), 1) // dk)
    bd_s = (iota2((2 * dk, 2 * dk), 0) // dk) == (iota2((2 * dk, 2 * dk), 1) // dk)

    def expand(mc):
        return jnp.where(bd_p, jnp.concatenate([mc] * gh, axis=0), 0.0).astype(BF16)

    units = [(c, gi) for c in range(nch) for gi in range(nh // gh)]
    n_cs, m_cs, p_cs = [], [None] * len(units), [None] * len(units)

    def b_scores(c, gi):
        rows = slice(c * ch, (c + 1) * ch)
        h0 = gi * gh
        sc = []
        for p in range(gh // 2):
            ps = slice((h0 + 2 * p) * dk, (h0 + 2 * p + 2) * dk)
            kc = k_s[rd_slot, rows, ps]
            kbd = jnp.where(bd_k, jnp.concatenate([kc, kc], axis=0), jnp.zeros((), BF16))
            sc.append(lax.dot_general(jnp.concatenate([kb_s[rd_slot, rows, ps], q_s[rd_slot, rows, ps]], axis=0),
                                      kbd, (((1,), (1,)), ((), ())), preferred_element_type=F32))
        sc = jnp.concatenate(sc, axis=1)
        wide = lambda hl: jnp.concatenate([gcol_s[rd_slot, rows, (h0 + hl) * dk:(h0 + hl + 1) * dk]] * (pw // dk),
                                          axis=1)
        gcol_p = wide(0)
        for hl in range(1, gh):
            gcol_p = jnp.where(head_of_lane == hl, wide(hl), gcol_p)
        grow_p = jnp.sum(jnp.where(eye_p, gcol_p, 0.0), axis=0, keepdims=True)
        decay = jnp.exp(jnp.where(incl_p, gcol_p - grow_p, -jnp.inf))
        n_cs.append(jnp.where(strict_p, sc[0:ch] * decay, 0.0))
        qk_s[c, gi] = (sc[ch:2 * ch] * decay).astype(BF16)

    def b_square(u):
        m_cs[u] = jnp.dot(n_cs[u].astype(BF16), expand(n_cs[u]), preferred_element_type=F32)
        p_cs[u] = eye_f - n_cs[u]

    def b_level(u, last):
        m_bd = expand(m_cs[u])
        if last:
            p_cs[u] = p_cs[u] + jnp.dot(p_cs[u].astype(BF16), m_bd, preferred_element_type=F32)
        else:
            both = jnp.dot(jnp.concatenate([m_cs[u], p_cs[u]], axis=0).astype(BF16), m_bd,
                           preferred_element_type=F32)
            m_cs[u] = both[0:ch]
            p_cs[u] = p_cs[u] + both[ch:2 * ch]

    def b_apply(u):
        c, gi = units[u]
        rows = slice(c * ch, (c + 1) * ch)
        h0 = gi * gh
        rhs = jnp.concatenate(
            [jnp.concatenate([vb_s[rd_slot, rows, (h0 + hl) * dk:(h0 + hl + 1) * dk],
                              wr_s[rd_slot, rows, (h0 + hl) * dk:(h0 + hl + 1) * dk]], axis=1)
             for hl in range(gh)], axis=0)
        uw = rhs + jnp.dot(expand(p_cs[u] - eye_f), rhs.astype(BF16), preferred_element_type=F32)
        for hl in range(gh):
            hs = slice((h0 + hl) * dk, (h0 + hl + 1) * dk)
            uin_s[rows, hs] = uw[hl * ch:(hl + 1) * ch, 0:dk]
            win_s[rows, hs] = uw[hl * ch:(hl + 1) * ch, dk:2 * dk].astype(BF16)

    def b_recur(c, gi):
        rows = slice(c * ch, (c + 1) * ch)
        last = (c + 1) * ch - 1
        gs = slice(gi * gw, (gi + 1) * gw)
        u_parts, qs_parts = [], []
        for p in range(gh // 2):
            ps = slice(gi * gw + 2 * p * dk, gi * gw + (2 * p + 2) * dk)
            s_pair = s_ref[:, ps]
            s_bd = jnp.where(bd_s, jnp.concatenate([s_pair, s_pair], axis=0), 0.0).astype(BF16)
            r = jnp.dot(jnp.concatenate([win_s[rows, ps], qd_s[rd_slot, rows, ps]], axis=0), s_bd,
                        preferred_element_type=F32)
            u_parts.append(uin_s[rows, ps] - r[0:ch])
            qs_parts.append(r[ch:2 * ch])
        u_g = jnp.concatenate(u_parts, axis=1)
        u_bd = jnp.where(bd_u, jnp.concatenate([u_g] * gh, axis=0), 0.0).astype(BF16)
        o_s[rows, gs] = jnp.concatenate(qs_parts, axis=1) + jnp.dot(qk_s[c, gi], u_bd,
                                                                      preferred_element_type=F32)
        kd_stack = jnp.concatenate(
            [kd_s[rd_slot, rows, gi * gw + hl * dk:gi * gw + (hl + 1) * dk] for hl in range(gh)], axis=0)
        s_ref[:, gs] = (s_ref[:, gs] * jnp.exp(gcol_s[rd_slot, last:last + 1, gs])
                        + lax.dot_general(kd_stack, u_bd, (((0,), (0,)), ((), ())),
                                          preferred_element_type=F32))

    def b_gate(h):
        hs = slice(h * dk, (h + 1) * dk)
        og_s[:, hs] = (_rms(o_s[:, hs], onw_ref[...]) * z_s[rd_slot, :, hs]).astype(BF16)

    def b_proj(j):
        cols = slice(j * OUT_CHUNK, (j + 1) * OUT_CHUNK)
        xo_ref[0, :, cols] = xb_ref[0, :, cols] + jnp.dot(og_s[...], wout_ref[:, cols],
                                                         preferred_element_type=F32)

    def b_state():
        for h in range(nh):
            ssm_ref[0, h] = s_ref[:, h * dk:(h + 1) * dk]

    for c, gi in units:
        back.append((2, functools.partial(b_scores, c, gi)))
    for u in range(len(units)):
        back.append((1, functools.partial(b_square, u)))
    sq = 2
    while sq < ch:
        for u in range(len(units)):
            back.append((1, functools.partial(b_level, u, 2 * sq >= ch)))
        sq *= 2
    for u in range(len(units)):
        back.append((1.5, functools.partial(b_apply, u)))
    for c in range(nch):
        for gi in range(nh // gh):
            back.append((3, functools.partial(b_recur, c, gi)))
    for h in range(nh):
        back.append((0.5, functools.partial(b_gate, h)))
    for j in range(d_model // OUT_CHUNK):
        back.append((1, functools.partial(b_proj, j)))
    back.append((0.5, b_state))

    _interleave([front, back])


def _ffn_sample_chunk(xn_ref, wg_ref, wv_ref, cwg_ref, cwv_ref, cbg_ref, cbv_ref, wdn_ref, stg_ref, stv_ref,
                      sto_ref):
    sc = wg_ref.shape[1]
    dff = sto_ref.shape[2] // 2
    conv = []
    for half, (w_ref, cw_ref, cb_ref, st_ref) in enumerate(((wg_ref, cwg_ref, cbg_ref, stg_ref),
                                                            (wv_ref, cwv_ref, cbv_ref, stv_ref))):
        h = jnp.dot(xn_ref[...], w_ref[...], preferred_element_type=F32)
        prev = st_ref[:, 1, :]
        conv.append(st_ref[:, 0, :] * cw_ref[0:1, :] + prev * cw_ref[1:2, :] + h * cw_ref[2:3, :] + cb_ref[...])
        for cc in range(dff // sc):
            @pl.when(pl.program_id(0) == cc)
            def _(half=half, cc=cc, prev=prev, h=h):
                cols = slice(half * dff + cc * sc, half * dff + (cc + 1) * sc)
                sto_ref[:, 0, cols] = prev
                sto_ref[:, 1, cols] = h
    a = (_silu(conv[0]) * conv[1]).astype(BF16)
    return jnp.dot(a, wdn_ref[...], preferred_element_type=F32)


def _sample_layer0_kernel(x_ref, pb_ref, n1_ref, pw_ref, ps_ref, n2_ref,
                          wg_ref, wv_ref, cwg_ref, cwv_ref, cbg_ref, cbv_ref, wdn_ref, stg_ref, stv_ref,
                          xo_ref, ho_ref, sto_ref, xn_s):
    c = pl.program_id(0)
    nbuf = pb_ref.shape[0]
    gd = pw_ref.shape[1]

    @pl.when(c == 0)
    def _():
        x = x_ref[...]
        h = _rms(x, n1_ref[...])
        parts = []
        for g, w in enumerate(POOL_WINDOWS):
            cols = slice(g * gd, (g + 1) * gd)
            s = h[:, cols]
            for j in range(1, w):
                s = s + pb_ref[nbuf - j, :, cols]
            dg = s / float(w) - h[:, cols]
            parts.append(x[:, cols] + _dot(dg, pw_ref[g]) * ps_ref[:, cols])
        x1 = jnp.concatenate(parts, axis=1)
        ho_ref[...] = h
        xn_s[...] = _rms(x1, n2_ref[...]).astype(BF16)
        xo_ref[...] = x1

    xo_ref[...] += _ffn_sample_chunk(xn_s, wg_ref, wv_ref, cwg_ref, cwv_ref, cbg_ref, cbv_ref, wdn_ref,
                                     stg_ref, stv_ref, sto_ref)


def _sample_dn_proj_kernel(x_ref, n1_ref, win_ref, wab_ref, cw_ref, alog_ref, dtb_ref,
                           selg_ref, selb_ref, cst_ref,
                           w_ref, qd_ref, k_ref, vb_ref, qk_ref, eg_ref, z_ref, pre_ref):
    dk = DN_HEAD_DIM
    dn = cw_ref.shape[1] // 3
    nh = dn // dk
    xb = _rms(x_ref[...], n1_ref[...]).astype(BF16)
    pre = jnp.dot(xb, win_ref[:, 0:3 * dn], preferred_element_type=F32)
    y = (cst_ref[0] * cw_ref[0:1, :] + cst_ref[1] * cw_ref[1:2, :] + cst_ref[2] * cw_ref[2:3, :]
         + pre * cw_ref[3:4, :])
    qkv = _silu(y)
    pre_ref[...] = pre
    z_ref[...] = jnp.dot(xb, win_ref[:, 3 * dn:4 * dn], preferred_element_type=F32)
    ab = jnp.dot(xb, wab_ref[...], preferred_element_type=F32)
    g = -jnp.exp(alog_ref[...]) * _softplus(ab + dtb_ref[...])
    eg = jnp.exp(_dot_exact_rhs(g, selg_ref[...]))
    betab = _dot_exact_rhs(1.0 / (1.0 + jnp.exp(-ab)), selb_ref[...])
    eg_ref[...] = eg
    for h in range(nh):
        hs = slice(h * dk, (h + 1) * dk)
        q = qkv[:, hs]
        k = qkv[:, dn + h * dk:dn + (h + 1) * dk]
        v = qkv[:, 2 * dn + h * dk:2 * dn + (h + 1) * dk]
        q = q * lax.rsqrt(jnp.sum(q * q, axis=-1, keepdims=True) + NORM_EPS) * (dk ** -0.5)
        k = k * lax.rsqrt(jnp.sum(k * k, axis=-1, keepdims=True) + NORM_EPS)
        kb = k * betab[:, hs]
        qk = jnp.sum(q.astype(BF16).astype(F32) * k.astype(BF16).astype(F32), axis=-1, keepdims=True)
        w_ref[:, hs] = kb * eg[:, hs]
        qd_ref[:, hs] = q * eg[:, hs]
        k_ref[:, hs] = k
        vb_ref[:, hs] = v * betab[:, hs]
        qk_ref[:, hs] = jnp.broadcast_to(qk, (q.shape[0], dk))


def _sample_dn_state_kernel(w_ref, qd_ref, k_ref, vb_ref, qk_ref, eg_ref, s_ref, o_ref, so_ref):
    dk = DN_HEAD_DIM
    nb, nh = s_ref.shape[0], s_ref.shape[1]
    eye = (lax.broadcasted_iota(jnp.int32, (dk, dk), 0) == lax.broadcasted_iota(jnp.int32, (dk, dk), 1))
    for b in range(nb):
        for h in range(nh):
            hs = slice(h * dk, (h + 1) * dk)
            s = s_ref[b, h]
            lhs = jnp.concatenate([jnp.broadcast_to(w_ref[b:b + 1, hs], (8, dk)),
                                   jnp.broadcast_to(qd_ref[b:b + 1, hs], (8, dk))], axis=0)
            r = _dot(lhs, s)
            u = vb_ref[b:b + 1, hs] - r[0:1]
            ub = u.astype(BF16).astype(F32)
            o_ref[b:b + 1, hs] = r[8:9] + qk_ref[b:b + 1, hs] * ub
            kdiag = jnp.where(eye, jnp.broadcast_to(k_ref[b:b + 1, hs], (dk, dk)), 0.0)
            so_ref[b, h] = s * eg_ref[b:b + 1, hs] + _dot(kdiag, jnp.broadcast_to(u, (dk, dk)))


def _sample_tail_kernel(x_ref, o_ref, z_ref, onw_ref, wout_ref, n2_ref,
                        wg_ref, wv_ref, cwg_ref, cwv_ref, cbg_ref, cbv_ref, wdn_ref, stg_ref, stv_ref, fn_ref,
                        yo_ref, sto_ref, xn_s, acc_s):
    c = pl.program_id(0)
    dk = DN_HEAD_DIM

    @pl.when(c == 0)
    def _():
        outs = []
        for h in range(o_ref.shape[1] // dk):
            outs.append(_rms(o_ref[:, h * dk:(h + 1) * dk], onw_ref[...]))
        o = jnp.concatenate(outs, axis=1) * _silu(z_ref[...])
        x = x_ref[...] + _dot(o, wout_ref[...])
        acc_s[...] = x
        xn_s[...] = _rms(x, n2_ref[...]).astype(BF16)

    acc_s[...] += _ffn_sample_chunk(xn_s, wg_ref, wv_ref, cwg_ref, cwv_ref, cbg_ref, cbv_ref, wdn_ref,
                                    stg_ref, stv_ref, sto_ref)

    @pl.when(c == pl.num_programs(0) - 1)
    def _():
        yo_ref[...] = _rms(acc_s[...], fn_ref[...])


def _params(*sem):
    return pltpu.CompilerParams(dimension_semantics=sem, vmem_limit_bytes=VMEM_LIMIT)


def kernel(x_prompt, x_sample, state_pool_buf, state_dn_conv, state_dn_ssm, state_ffn_conv, norm1_w, norm2_w,
           final_norm_w, pool_w, pool_scale, dn_w_in, dn_conv_w, dn_a_log, dn_dt_bias, dn_o_norm_w, dn_w_out,
           ffn_w_up, ffn_conv_w, ffn_conv_b, ffn_w_down):
    bp, seq, d = x_prompt.shape
    bs = x_sample.shape[0]
    nbuf = state_pool_buf.shape[2]
    dff = ffn_w_down.shape[1]
    nh = dn_a_log.shape[1]
    dk = DN_HEAD_DIM
    dn = nh * dk
    gd = pool_w.shape[2]
    tm = PROMPT_TILE
    nt = seq // tm
    nfc = dff // FFN_CHUNK
    lanes = V7X_LANES
    assert seq % tm == 0 and dff % FFN_CHUNK == 0 and (3 * dn) % QKV_CHUNK == 0 and dk == lanes
    assert nbuf + 1 == max(POOL_WINDOWS) and 2 * nh <= lanes and bs % SAMPLE_BLOCK == 0

    row = lambda v: v.reshape(1, -1)
    wup = ffn_w_up.astype(BF16)
    wdn = ffn_w_down.astype(BF16)
    pw = pool_w.astype(BF16)
    w_in = dn_w_in[0]
    win_all = dn_w_in.astype(BF16)
    wab = jnp.pad(w_in[:, 4 * dn:], ((0, 0), (0, lanes - 2 * nh))).astype(BF16)
    wout = dn_w_out[0].astype(BF16)
    alog = jnp.pad(dn_a_log[0], (0, lanes - nh)).reshape(1, lanes)
    dtb = jnp.pad(dn_dt_bias[0], (0, lanes - nh)).reshape(1, lanes)
    onw = row(dn_o_norm_w[0])
    lane_head = jnp.arange(dn) // dk
    selg = (jnp.arange(lanes)[:, None] == lane_head[None, :]).astype(BF16)
    selb = (jnp.arange(lanes)[:, None] == (lane_head[None, :] + nh)).astype(BF16)
    ti = jnp.arange(tm)
    ltri = ((ti[:, None] // DN_CHUNK == ti[None, :] // DN_CHUNK) & (ti[:, None] >= ti[None, :])).astype(BF16)

    tf = FFN_TILE
    assert seq % tf == 0
    x_spec = pl.BlockSpec((1, tf, d), lambda b, t: (b, t, 0))
    layer_resident = lambda shape, layer: pl.BlockSpec((None,) + shape, lambda *_: (layer,) + (0,) * len(shape),
                                                       pipeline_mode=pl.Buffered(1))
    ffn_w_specs = lambda layer: [layer_resident((d, 2 * dff), layer), _resident((3, 2 * dff)),
                                 _resident((1, 2 * dff)), layer_resident((dff, d), layer)]
    fst_spec = pl.BlockSpec((1, CONV_HALO, 2 * dff), lambda b, t: (b, 0, 0))
    fst_shape = jax.ShapeDtypeStruct((bp, CONV_HALO, 2 * dff), F32)

    x2, pool_tail, fst0 = pl.pallas_call(
        _prompt_layer0_kernel,
        grid=(bp, seq // tf),
        in_specs=[x_spec, _resident((1, d)), _resident(pw.shape[1:]), _resident((1, d)), _resident((1, d))]
        + ffn_w_specs(0),
        out_specs=[x_spec, pl.BlockSpec((1, POOL_HALO, d), lambda b, t: (b, 0, 0)), fst_spec],
        out_shape=[jax.ShapeDtypeStruct((bp, seq, d), F32), jax.ShapeDtypeStruct((bp, POOL_HALO, d), F32), fst_shape],
        scratch_shapes=[pltpu.VMEM((POOL_HALO + tf, d), F32), pltpu.VMEM((CONV_HALO, 2 * dff), F32)],
        compiler_params=_params("arbitrary", "arbitrary"),
        name="prompt_layer0",
    )(x_prompt, row(norm1_w[0]), pw[0], row(pool_scale[0]), row(norm2_w[0]),
      wup, ffn_conv_w[0], row(ffn_conv_b[0]), wdn)

    ntiles = bp * nt
    front_tile = lambda i: jnp.minimum(i, ntiles - 1)
    back_tile = lambda i: jnp.maximum(i - 1, 0)
    x_front = pl.BlockSpec((1, tm, d), lambda i: (front_tile(i) // nt, front_tile(i) % nt, 0))
    x_back = pl.BlockSpec((1, tm, d), lambda i: (back_tile(i) // nt, back_tile(i) % nt, 0))
    x3, cst, ssm = pl.pallas_call(
        functools.partial(_prompt_deltanet_kernel, tiles_per_seq=nt),
        grid=(ntiles + 1,),
        in_specs=[x_front, x_back,
                  _resident((1, d)), layer_resident((d, win_all.shape[2]), 0), _resident((d, lanes)),
                  _resident((4, 3 * dn)), _resident((1, lanes)), _resident((1, lanes)), _resident((1, dk)),
                  _resident((dn, d)), _resident((tm, tm))],
        out_specs=[x_back,
                   pl.BlockSpec((1, CONV_HALO, 3 * dn), lambda i: (front_tile(i) // nt, 0, 0)),
                   pl.BlockSpec((1, nh, dk, dk), lambda i: (back_tile(i) // nt, 0, 0, 0))],
        out_shape=[jax.ShapeDtypeStruct((bp, seq, d), F32),
                   jax.ShapeDtypeStruct((bp, CONV_HALO, 3 * dn), F32),
                   jax.ShapeDtypeStruct((bp, nh, dk, dk), F32)],
        scratch_shapes=[pltpu.VMEM((CONV_HALO, 3 * dn), F32), pltpu.VMEM((dk, dn), F32),
                        pltpu.VMEM((tm, 3 * dn), F32)]
        + [pltpu.VMEM((2, tm, dn), BF16)] * 5 + [pltpu.VMEM((2, tm, dn), F32)] * 4
        + [pltpu.VMEM((tm, dn), F32), pltpu.VMEM((tm, dn), BF16),
           pltpu.VMEM((tm // DN_CHUNK, nh // HEAD_GROUP, DN_CHUNK, HEAD_GROUP * DN_CHUNK), BF16),
           pltpu.VMEM((tm, dn), F32), pltpu.VMEM((tm, dn), BF16)],
        compiler_params=_params("arbitrary"),
        name="prompt_deltanet",
    )(x2, x2, row(norm1_w[1]), win_all, wab, dn_conv_w[0], alog, dtb, onw, wout, ltri)

    y_prompt, fst1 = pl.pallas_call(
        _prompt_ffn_final_kernel,
        grid=(bp, seq // tf),
        in_specs=[x_spec, _resident((1, d))] + ffn_w_specs(1) + [_resident((1, d))],
        out_specs=[x_spec, fst_spec],
        out_shape=[jax.ShapeDtypeStruct((bp, seq, d), F32), fst_shape],
        scratch_shapes=[pltpu.VMEM((CONV_HALO, 2 * dff), F32)],
        compiler_params=_params("arbitrary", "arbitrary"),
        name="prompt_ffn_final",
    )(x3, row(norm2_w[1]), wup, ffn_conv_w[1], row(ffn_conv_b[1]), wdn, row(final_norm_w))

    xs = x_sample[:, 0, :]
    pb = jnp.transpose(state_pool_buf[0], (1, 0, 2))
    cs = jnp.transpose(state_dn_conv[0], (1, 0, 2))
    sc = SAMPLE_FFN_CHUNK
    nsc = dff // sc
    assert dff % sc == 0

    def ffn_chunk_specs(layer):
        specs = []
        for shape in ((None, d, sc), (3, sc), (1, sc)):
            for half in range(2):
                if shape[0] is None:
                    specs.append(pl.BlockSpec(shape, lambda c, half=half: (layer, 0, half * nsc + c)))
                else:
                    specs.append(pl.BlockSpec(shape, lambda c, half=half: (0, half * nsc + c)))
        specs.append(pl.BlockSpec((None, sc, d), lambda c: (layer, c, 0)))
        for half in range(2):
            specs.append(pl.BlockSpec((None, bs, 2, sc), lambda c, half=half: (layer, 0, 0, half * nsc + c)))
        return specs

    sto_spec = _full((bs, 2, 2 * dff))
    sto_shape = jax.ShapeDtypeStruct((bs, 2, 2 * dff), F32)

    def ffn_chunk_args(layer):
        cb = row(ffn_conv_b[layer])
        return (wup, wup, ffn_conv_w[layer], ffn_conv_w[layer], cb, cb, wdn, state_ffn_conv, state_ffn_conv)

    xs2, pool_row, fso0 = pl.pallas_call(
        _sample_layer0_kernel,
        grid=(nsc,),
        in_specs=[_resident((bs, d)), _resident((nbuf, bs, d)), _resident((1, d)), _resident(pw.shape[1:]),
                  _resident((1, d)), _resident((1, d))] + ffn_chunk_specs(0),
        out_specs=[_full((bs, d)), _full((bs, d)), sto_spec],
        out_shape=[jax.ShapeDtypeStruct((bs, d), F32), jax.ShapeDtypeStruct((bs, d), F32), sto_shape],
        scratch_shapes=[pltpu.VMEM((bs, d), BF16)],
        compiler_params=_params("arbitrary"),
        name="sample_layer0",
    )(xs, pb, row(norm1_w[0]), pw[0], row(pool_scale[0]), row(norm2_w[0]), *ffn_chunk_args(0))

    vec = jax.ShapeDtypeStruct((bs, dn), F32)
    w_s, qd_s, k_s, vb_s, qk_s, eg_s, z_s, conv_row = pl.pallas_call(
        _sample_dn_proj_kernel,
        out_shape=[vec] * 7 + [jax.ShapeDtypeStruct((bs, 3 * dn), F32)],
        compiler_params=pltpu.CompilerParams(vmem_limit_bytes=VMEM_LIMIT),
        name="sample_dn_proj",
    )(xs2, row(norm1_w[1]), win_all[0], wab, dn_conv_w[0], alog, dtb, selg, selb, cs)

    nb = SAMPLE_BLOCK
    vspec = pl.BlockSpec((nb, dn), lambda i: (i, 0))
    sspec = pl.BlockSpec((nb, nh, dk, dk), lambda i: (i, 0, 0, 0))
    o_s, ssm_s = pl.pallas_call(
        _sample_dn_state_kernel,
        grid=(bs // nb,),
        in_specs=[vspec] * 6 + [sspec],
        out_specs=[vspec, sspec],
        out_shape=[vec, jax.ShapeDtypeStruct((bs, nh, dk, dk), F32)],
        compiler_params=_params("arbitrary"),
        name="sample_dn_state",
    )(w_s, qd_s, k_s, vb_s, qk_s, eg_s, state_dn_ssm[0])

    ys, fso1 = pl.pallas_call(
        _sample_tail_kernel,
        grid=(nsc,),
        in_specs=[_resident((bs, d)), _resident((bs, dn)), _resident((bs, dn)), _resident((1, dk)),
                  _resident((dn, d)), _resident((1, d))] + ffn_chunk_specs(1) + [_resident((1, d))],
        out_specs=[_full((bs, d)), sto_spec],
        out_shape=[jax.ShapeDtypeStruct((bs, d), F32), sto_shape],
        scratch_shapes=[pltpu.VMEM((bs, d), BF16), pltpu.VMEM((bs, d), F32)],
        compiler_params=_params("arbitrary"),
        name="sample_tail",
    )(xs2, o_s, z_s, onw, wout, row(norm2_w[1]), *ffn_chunk_args(1), row(final_norm_w))

    shifted = lambda state, new_row: jnp.concatenate([state[:, 1:], new_row[:, None]], axis=1)
    return (
        y_prompt,
        ys[:, None, :],
        pool_tail[None, :, POOL_HALO - nbuf:, :],
        shifted(state_pool_buf[0], pool_row)[None],
        cst[None, :, CONV_HALO - 3:, :],
        shifted(state_dn_conv[0], conv_row)[None],
        ssm[None],
        ssm_s[None],
        jnp.stack([fst0[:, CONV_HALO - 2:, :], fst1[:, CONV_HALO - 2:, :]]),
        jnp.stack([fso0, fso1]),
    )
```
